```python
import math
import jax, jax.numpy as jnp
from jax import lax
import numpy as np

D_MODEL = 2048
BATCH = 8
SEQ = 2048
DEPTH = 1

MLA_HEADS = 16
MLA_Q_RANK = 768
MLA_KV_RANK = 512
MLA_NOPE_DIM = 128
MLA_ROPE_DIM = 64
MLA_V_DIM = 128
MLA_QK_DIM = MLA_NOPE_DIM + MLA_ROPE_DIM
ROPE_THETA = 10000.0
Q_BLOCK = 128
DIL_GROUPS = ((128, 1), (512, 4), (2048, 16))
DIL_HEADS_PER_GROUP = 4
DIL_HEADS = DIL_HEADS_PER_GROUP * len(DIL_GROUPS)
DIL_HEAD_DIM = 128
DIL_WIDTH = DIL_HEADS * DIL_HEAD_DIM
IN_SIZES = (MLA_Q_RANK, MLA_KV_RANK, MLA_ROPE_DIM, DIL_WIDTH, DIL_WIDTH, DIL_WIDTH, D_MODEL, D_MODEL)
IN_COLS = sum(IN_SIZES)
IN_OFFSETS = tuple(int(v) for v in np.cumsum(IN_SIZES)[:-1])
N_EXPERTS = 32
TOP_K = 4
D_FF = 2048
SWIGLU_LIMIT = 7.0
SWIGLU_ALPHA = 1.702
MOE_BLOCK = 128
DN_ALPHA = (2.0 * DEPTH) ** 0.25
DN_BETA = (8.0 * DEPTH) ** -0.25
LN_EPS = 1e-5
RMS_EPS = 1e-6

kernel_name = "hybrid_mla_dilated_moe_deepnorm"


def layer_norm(x, g, b):
    xf = x.astype(jnp.float32)
    mu = jnp.mean(xf, axis=-1, keepdims=True)
    var = jnp.mean(jnp.square(xf - mu), axis=-1, keepdims=True)
    return ((xf - mu) * lax.rsqrt(var + LN_EPS) * g.astype(jnp.float32) + b.astype(jnp.float32)).astype(x.dtype)


def rms_norm(x, g):
    xf = x.astype(jnp.float32)
    ms = jnp.mean(jnp.square(xf), axis=-1, keepdims=True)
    return (xf * lax.rsqrt(ms + RMS_EPS) * g.astype(jnp.float32)).astype(x.dtype)


def rope(x, pos):
    half = x.shape[-1] // 2
    inv = ROPE_THETA ** (-jnp.arange(half, dtype=jnp.float32) / half)
    ang = pos.astype(jnp.float32)[:, None] * inv[None, :]
    cos = jnp.cos(ang)[:, None, :]
    sin = jnp.sin(ang)[:, None, :]
    xf = x.astype(jnp.float32)
    x1, x2 = xf[..., :half], xf[..., half:]
    return jnp.concatenate([x1 * cos - x2 * sin, x2 * cos + x1 * sin], axis=-1).astype(x.dtype)


def mla_attention(c_q, c_kv, k_pe, q_norm_g, kv_norm_g, w_uq, w_ukv):
    B, S, _ = c_q.shape
    pos = jnp.arange(S)
    q = (rms_norm(c_q, q_norm_g) @ w_uq).reshape(B, S, MLA_HEADS, MLA_QK_DIM)
    q = jnp.concatenate([q[..., :MLA_NOPE_DIM], rope(q[..., MLA_NOPE_DIM:], pos)], axis=-1)
    kv = (rms_norm(c_kv, kv_norm_g) @ w_ukv).reshape(B, S, MLA_HEADS, MLA_NOPE_DIM + MLA_V_DIM)
    k_nope, v = kv[..., :MLA_NOPE_DIM], kv[..., MLA_NOPE_DIM:]
    k_rot = rope(k_pe[:, :, None, :], pos)
    k = jnp.concatenate([k_nope, jnp.broadcast_to(k_rot, (B, S, MLA_HEADS, MLA_ROPE_DIM))], axis=-1)
    scale = MLA_QK_DIM ** -0.5
    nb = S // Q_BLOCK
    qb = q.reshape(B, nb, Q_BLOCK, MLA_HEADS, MLA_QK_DIM).transpose(1, 0, 2, 3, 4)

    def block(args):
        qi, bi = args
        s = jnp.einsum('bqhd,bkhd->bhqk', qi, k).astype(jnp.float32) * scale
        qpos = bi * Q_BLOCK + jnp.arange(Q_BLOCK)
        causal = pos[None, :] <= qpos[:, None]
        s = jnp.where(causal[None, None], s, -jnp.inf)
        p = jax.nn.softmax(s, axis=-1).astype(v.dtype)
        return jnp.einsum('bhqk,bkhd->bqhd', p, v)

    o = lax.map(block, (qb, jnp.arange(nb)))
    return o.transpose(1, 0, 2, 3, 4).reshape(B, S, MLA_HEADS * MLA_V_DIM)


def dilated_group(q, k, v, slopes, window, dilation):
    B, S, Hg, Dh = q.shape
    n_win = window // dilation
    blk = n_win
    n = S // dilation
    nb = -(-n // blk)
    n_pad = nb * blk

    def to_sub(t):
        return t.reshape(B, n, dilation, Hg, Dh).transpose(0, 2, 1, 3, 4)

    qs = jnp.pad(to_sub(q), ((0, 0), (0, 0), (0, n_pad - n), (0, 0), (0, 0)))
    qs = qs.reshape(B, dilation, nb, blk, Hg, Dh)

    def key_blocks(t):
        tp = jnp.pad(to_sub(t), ((0, 0), (0, 0), (blk, n_pad - n), (0, 0), (0, 0)))
        prev = tp[:, :, :n_pad].reshape(B, dilation, nb, blk, Hg, Dh)
        cur = tp[:, :, blk:].reshape(B, dilation, nb, blk, Hg, Dh)
        return jnp.concatenate([prev, cur], axis=3)

    kb, vb = key_blocks(k), key_blocks(v)
    s = jnp.einsum('brnqhd,brnkhd->brnhqk', qs, kb).astype(jnp.float32) * (Dh ** -0.5)
    i = jnp.arange(blk)[:, None]
    j = jnp.arange(2 * blk)[None, :] - blk
    steps = i - j
    kpos = jnp.arange(nb)[:, None, None] * blk + j[None]
    valid = ((steps >= 0) & (steps <= n_win))[None] & (kpos >= 0)
    alibi = -slopes.astype(jnp.float32)[:, None, None] * (steps * dilation).astype(jnp.float32)[None]
    s = jnp.where(valid[None, None, :, None], s + alibi, -jnp.inf)
    lse = jax.nn.logsumexp(s, axis=-1)
    p = jnp.exp(s - lse[..., None]).astype(v.dtype)
    o = jnp.einsum('brnhqk,brnkhd->brnqhd', p, vb)
    o = o.reshape(B, dilation, n_pad, Hg, Dh)[:, :, :n].transpose(0, 2, 1, 3, 4).reshape(B, S, Hg, Dh)
    lse = lse.transpose(0, 1, 2, 4, 3).reshape(B, dilation, n_pad, Hg)[:, :, :n]
    lse = lse.transpose(0, 2, 1, 3).reshape(B, S, Hg)
    return o, lse


def dilated_attention(q, k, v):
    B, S = q.shape[:2]
    slopes = 2.0 ** (-8.0 * jnp.arange(1, DIL_HEADS + 1, dtype=jnp.float32) / DIL_HEADS)
    outs, lses = [], []
    for g, (window, dilation) in enumerate(DIL_GROUPS):
        sl = slice(g * DIL_HEADS_PER_GROUP, (g + 1) * DIL_HEADS_PER_GROUP)
        o, l = dilated_group(q[:, :, sl], k[:, :, sl], v[:, :, sl], slopes[sl], window, dilation)
        outs.append(o)
        lses.append(l)
    wts = jax.nn.softmax(jnp.stack(lses, axis=0), axis=0)
    o = jnp.concatenate([o_g * w_g[..., None].astype(o_g.dtype) for o_g, w_g in zip(outs, wts)], axis=2)
    return o.reshape(B, S, DIL_WIDTH)


def token_mixer(x, w_in, q_norm_g, kv_norm_g, w_uq, w_ukv, w_o_mla, w_o_dil, w_out):
    B, S, _ = x.shape
    proj = x @ w_in
    c_q, c_kv, k_pe, q_d, k_d, v_d, gate_mla, gate_dil = jnp.split(proj, IN_OFFSETS, axis=-1)
    o_mla = mla_attention(c_q, c_kv, k_pe, q_norm_g, kv_norm_g, w_uq, w_ukv)
    hs = (B, S, DIL_HEADS, DIL_HEAD_DIM)
    o_dil = dilated_attention(q_d.reshape(hs), k_d.reshape(hs), v_d.reshape(hs))
    merged = jax.nn.sigmoid(gate_mla) * (o_mla @ w_o_mla) + jax.nn.sigmoid(gate_dil) * (o_dil @ w_o_dil)
    return merged @ w_out


def moe_ffn(x, w_router, b_router, w1, b1, w2, b2):
    B, S, D = x.shape
    xt = x.reshape(-1, D)
    N = xt.shape[0]
    logits = (xt @ w_router + b_router).astype(jnp.float32)
    top_val, top_idx = lax.top_k(logits, TOP_K)
    gates = jax.nn.softmax(top_val, axis=-1)
    NK = N * TOP_K
    e_flat = top_idx.reshape(-1).astype(jnp.int32)
    g_flat = gates.reshape(-1)
    tok_flat = jnp.arange(NK, dtype=jnp.int32) // TOP_K
    order = jnp.argsort(e_flat)
    e_sorted, g_sorted, tok_sorted = e_flat[order], g_flat[order], tok_flat[order]
    counts = jnp.bincount(e_flat, length=N_EXPERTS)
    padded = (counts + MOE_BLOCK - 1) // MOE_BLOCK * MOE_BLOCK
    pad_end = jnp.cumsum(padded)
    pad_start = pad_end - padded
    start = jnp.cumsum(counts) - counts
    dest = pad_start[e_sorted] + jnp.arange(NK, dtype=jnp.int32) - start[e_sorted]
    cap = (-(-NK // MOE_BLOCK) + N_EXPERTS) * MOE_BLOCK
    n_blocks = cap // MOE_BLOCK
    tok_buf = jnp.zeros((cap,), jnp.int32).at[dest].set(tok_sorted)
    gate_buf = jnp.zeros((cap,), jnp.float32).at[dest].set(g_sorted)
    blk_expert = jnp.minimum(
        jnp.searchsorted(pad_end, jnp.arange(n_blocks) * MOE_BLOCK, side='right'), N_EXPERTS - 1)

    def run(args):
        toks, g, e = args
        h = xt[toks] @ w1[e] + b1[e]
        x_glu = jnp.minimum(h[:, ::2], SWIGLU_LIMIT)
        x_lin = jnp.clip(h[:, 1::2], -SWIGLU_LIMIT, SWIGLU_LIMIT)
        a = x_glu * jax.nn.sigmoid(SWIGLU_ALPHA * x_glu) * (x_lin + 1.0)
        return ((a @ w2[e] + b2[e]) * g[:, None]).astype(x.dtype)

    out = lax.map(run, (tok_buf.reshape(n_blocks, MOE_BLOCK), gate_buf.reshape(n_blocks, MOE_BLOCK), blk_expert))
    y = jax.ops.segment_sum(out.reshape(cap, D), tok_buf, num_segments=N)
    return y.reshape(B, S, D).astype(x.dtype)


def setup_inputs(seed: int = 0) -> dict:
    key = jax.random.key(seed)
    ks = jax.random.split(key, 24)
    f32 = jnp.float32
    L = DEPTH

    def nrm(k, shape, scale):
        return jax.random.normal(k, shape, f32) * scale

    return {
        "x": jax.random.normal(ks[0], (BATCH, SEQ, D_MODEL), f32),
        "w_in": nrm(ks[1], (L, D_MODEL, IN_COLS), D_MODEL ** -0.5),
        "q_norm_g": 1.0 + nrm(ks[2], (L, MLA_Q_RANK), 0.02),
        "kv_norm_g": 1.0 + nrm(ks[3], (L, MLA_KV_RANK), 0.02),
        "w_uq": nrm(ks[4], (L, MLA_Q_RANK, MLA_HEADS * MLA_QK_DIM), MLA_Q_RANK ** -0.5),
        "w_ukv": nrm(ks[5], (L, MLA_KV_RANK, MLA_HEADS * (MLA_NOPE_DIM + MLA_V_DIM)), MLA_KV_RANK ** -0.5),
        "w_o_mla": nrm(ks[6], (L, MLA_HEADS * MLA_V_DIM, D_MODEL), DN_BETA * (MLA_HEADS * MLA_V_DIM) ** -0.5),
        "w_o_dil": nrm(ks[7], (L, DIL_WIDTH, D_MODEL), DN_BETA * DIL_WIDTH ** -0.5),
        "w_out": nrm(ks[8], (L, D_MODEL, D_MODEL), DN_BETA * D_MODEL ** -0.5),
        "ln1_g": 1.0 + nrm(ks[9], (L, D_MODEL), 0.02),
        "ln1_b": nrm(ks[10], (L, D_MODEL), 0.02),
        "w_router": nrm(ks[11], (L, D_MODEL, N_EXPERTS), D_MODEL ** -0.5),
        "b_router": nrm(ks[12], (L, N_EXPERTS), 0.01),
        "w1": nrm(ks[13], (L, N_EXPERTS, D_MODEL, 2 * D_FF), D_MODEL ** -0.5),
        "b1": nrm(ks[14], (L, N_EXPERTS, 2 * D_FF), 0.02),
        "w2": nrm(ks[15], (L, N_EXPERTS, D_FF, D_MODEL), DN_BETA * D_FF ** -0.5),
        "b2": nrm(ks[16], (L, N_EXPERTS, D_MODEL), 0.02),
        "ln2_g": 1.0 + nrm(ks[17], (L, D_MODEL), 0.02),
        "ln2_b": nrm(ks[18], (L, D_MODEL), 0.02),
    }


def reference(x, w_in, q_norm_g, kv_norm_g, w_uq, w_ukv, w_o_mla, w_o_dil, w_out, ln1_g, ln1_b,
              w_router, b_router, w1, b1, w2, b2, ln2_g, ln2_b):
    for l in range(DEPTH):
        mix = token_mixer(x, w_in[l], q_norm_g[l], kv_norm_g[l], w_uq[l], w_ukv[l],
                          w_o_mla[l], w_o_dil[l], w_out[l])
        x = layer_norm(DN_ALPHA * x + mix, ln1_g[l], ln1_b[l])
        ffn = moe_ffn(x, w_router[l], b_router[l], w1[l], b1[l], w2[l], b2[l])
        x = layer_norm(DN_ALPHA * x + ffn, ln2_g[l], ln2_b[l])
    return x
```

```python
import functools

import numpy as np
import jax
import jax.numpy as jnp
from jax import lax
from jax.experimental import pallas as pl
from jax.experimental.pallas import tpu as pltpu

F32 = jnp.float32
BF16 = jnp.bfloat16

MLA_HEADS = 16
MLA_Q_RANK = 768
MLA_KV_RANK = 512
MLA_NOPE_DIM = 128
MLA_ROPE_DIM = 64
MLA_V_DIM = 128
MLA_QK_DIM = MLA_NOPE_DIM + MLA_ROPE_DIM
MLA_QK_PAD = 256
ROPE_THETA = 10000.0
DIL_GROUPS = ((128, 1), (512, 4), (2048, 16))
DIL_HEADS_PER_GROUP = 4
DIL_HEADS = DIL_HEADS_PER_GROUP * len(DIL_GROUPS)
DIL_HEAD_DIM = 128
DIL_GROUP_WIDTH = DIL_HEADS_PER_GROUP * DIL_HEAD_DIM
DIL_WIDTH = DIL_HEADS * DIL_HEAD_DIM
DIL_BLOCK = 128
N_EXPERTS = 32
TOP_K = 4
SWIGLU_LIMIT = 7.0
SWIGLU_ALPHA = 1.702
LN_EPS = 1e-5
RMS_EPS = 1e-6
LANES = 128
NEG_BIG = -1e30

VMEM_LIMIT = 56 * 1024 * 1024
TM_PROJ = 1024
TM_UP = 512
TM_MERGE = 512
TM_LN = 256
TM_MOE = 256
TM_COMBINE = 256
FF_CHUNK = 1024


def _cparams(n_axes):
    return pltpu.CompilerParams(dimension_semantics=("arbitrary",) * n_axes,
                                vmem_limit_bytes=VMEM_LIMIT)


def _sigmoid(x):
    return 1.0 / (1.0 + jnp.exp(-x))


def _proj_kernel(x_ref, w_ref, o_ref, xb_ref, *, act):
    @pl.when(pl.program_id(1) == 0)
    def _():
        xb_ref[...] = x_ref[...].astype(BF16)

    acc = jnp.dot(xb_ref[...], w_ref[...], preferred_element_type=F32)
    if act:
        acc = _sigmoid(acc)
    o_ref[...] = acc.astype(o_ref.dtype)


def _proj(x, w, out_dtype, tn, act, name):
    m, k = x.shape
    nc = w.shape[1]
    tm = min(TM_PROJ, m)
    return pl.pallas_call(
        functools.partial(_proj_kernel, act=act),
        grid=(m // tm, nc // tn),
        in_specs=[pl.BlockSpec((tm, k), lambda i, j: (i, 0)),
                  pl.BlockSpec((k, tn), lambda i, j: (0, j))],
        out_specs=pl.BlockSpec((tm, tn), lambda i, j: (i, j)),
        out_shape=jax.ShapeDtypeStruct((m, nc), out_dtype),
        scratch_shapes=[pltpu.VMEM((tm, k), BF16)],
        compiler_params=_cparams(2),
        name=name,
    )(x, w)


def _rope_tile(r, cos, sin_lo, sin_hi):
    half = MLA_ROPE_DIM // 2
    return (r * cos + pltpu.roll(r, LANES - half, 1) * sin_lo + pltpu.roll(r, half, 1) * sin_hi)


def _rms(x, g):
    ms = jnp.mean(x * x, axis=-1, keepdims=True)
    return x * lax.rsqrt(ms + RMS_EPS) * g


def _q_up_kernel(cq_ref, g_ref, w_ref, cos_ref, slo_ref, shi_ref, o_ref):
    cqn = _rms(cq_ref[...], g_ref[...]).astype(BF16)
    cos, slo, shi = cos_ref[...], slo_ref[...], shi_ref[...]
    for h in range(MLA_HEADS):
        c0 = h * MLA_QK_PAD
        qh = jnp.dot(cqn, w_ref[:, c0:c0 + MLA_QK_PAD], preferred_element_type=F32)
        o_ref[:, c0:c0 + LANES] = qh[:, :LANES].astype(BF16)
        o_ref[:, c0 + LANES:c0 + MLA_QK_PAD] = _rope_tile(qh[:, LANES:], cos, slo, shi).astype(BF16)


def _kv_up_kernel(ckv_ref, kpe_ref, g_ref, w_ref, cos_ref, slo_ref, shi_ref, k_ref, v_ref):
    ckvn = _rms(ckv_ref[...], g_ref[...]).astype(BF16)
    k_rot = _rope_tile(kpe_ref[...], cos_ref[...], slo_ref[...], shi_ref[...]).astype(BF16)
    for h in range(MLA_HEADS):
        c0 = h * (MLA_NOPE_DIM + MLA_V_DIM)
        kv = jnp.dot(ckvn, w_ref[:, c0:c0 + MLA_NOPE_DIM + MLA_V_DIM], preferred_element_type=F32)
        k0 = h * MLA_QK_PAD
        k_ref[:, k0:k0 + LANES] = kv[:, :MLA_NOPE_DIM].astype(BF16)
        k_ref[:, k0 + LANES:k0 + MLA_QK_PAD] = k_rot
        v_ref[:, h * MLA_V_DIM:(h + 1) * MLA_V_DIM] = kv[:, MLA_NOPE_DIM:].astype(BF16)


def _mla_up(proj_a, q_g, kv_g, w_uq, w_ukv, tabs, seq):
    n = proj_a.shape[0]
    tm = min(TM_UP, seq)
    sb = seq // tm
    tab_specs = [pl.BlockSpec((tm, LANES), lambda i: (i % sb, 0))] * 3
    qw = MLA_HEADS * MLA_QK_PAD
    q = pl.pallas_call(
        _q_up_kernel,
        grid=(n // tm,),
        in_specs=[pl.BlockSpec((tm, MLA_Q_RANK), lambda i: (i, 1)),
                  pl.BlockSpec((1, MLA_Q_RANK), lambda i: (0, 0)),
                  pl.BlockSpec((MLA_Q_RANK, qw), lambda i: (0, 0))] + tab_specs,
        out_specs=pl.BlockSpec((tm, qw), lambda i: (i, 0)),
        out_shape=jax.ShapeDtypeStruct((n, qw), BF16),
        compiler_params=_cparams(1),
        name="mla_q_up",
    )(proj_a, q_g, w_uq, *tabs)
    kvw = MLA_HEADS * (MLA_NOPE_DIM + MLA_V_DIM)
    k, v = pl.pallas_call(
        _kv_up_kernel,
        grid=(n // tm,),
        in_specs=[pl.BlockSpec((tm, MLA_KV_RANK), lambda i: (i, 0)),
                  pl.BlockSpec((tm, LANES), lambda i: (i, MLA_KV_RANK // LANES)),
                  pl.BlockSpec((1, MLA_KV_RANK), lambda i: (0, 0)),
                  pl.BlockSpec((MLA_KV_RANK, kvw), lambda i: (0, 0))] + tab_specs,
        out_specs=[pl.BlockSpec((tm, qw), lambda i: (i, 0)),
                   pl.BlockSpec((tm, MLA_HEADS * MLA_V_DIM), lambda i: (i, 0))],
        out_shape=[jax.ShapeDtypeStruct((n, qw), BF16),
                   jax.ShapeDtypeStruct((n, MLA_HEADS * MLA_V_DIM), BF16)],
        compiler_params=_cparams(1),
        name="mla_kv_up",
    )(proj_a, proj_a, kv_g, w_ukv, *tabs)
    return q, k, v


def _nt_dot(a, b):
    return lax.dot_general(a, b, (((1,), (1,)), ((), ())), preferred_element_type=F32)


def _mla_attn_kernel(q_ref, k_ref, v_ref, o_ref, *, tq, scale):
    seq = q_ref.shape[0]
    row = lax.broadcasted_iota(jnp.int32, (tq, tq), 0)
    col = lax.broadcasted_iota(jnp.int32, (tq, tq), 1)
    causal = col <= row
    for qi in range(seq // tq):
        r0 = qi * tq
        q = q_ref[r0:r0 + tq, :]
        s_d = jnp.where(causal, _nt_dot(q, k_ref[r0:r0 + tq, :]) * scale, -jnp.inf)
        m = jnp.max(s_d, axis=-1, keepdims=True)
        if qi > 0:
            s_p = _nt_dot(q, k_ref[0:r0, :]) * scale
            m = jnp.maximum(m, jnp.max(s_p, axis=-1, keepdims=True))
        p_d = jnp.exp(s_d - m)
        l = jnp.sum(p_d, axis=-1, keepdims=True)
        acc = jnp.dot(p_d.astype(BF16), v_ref[r0:r0 + tq, :], preferred_element_type=F32)
        if qi > 0:
            p_p = jnp.exp(s_p - m)
            l = l + jnp.sum(p_p, axis=-1, keepdims=True)
            acc = acc + jnp.dot(p_p.astype(BF16), v_ref[0:r0, :], preferred_element_type=F32)
        o_ref[r0:r0 + tq, :] = (acc / l).astype(o_ref.dtype)


def _mla_attention(q, k, v, batch, seq):
    n = q.shape[0]
    return pl.pallas_call(
        functools.partial(_mla_attn_kernel, tq=min(256, seq), scale=MLA_QK_DIM ** -0.5),
        grid=(batch, MLA_HEADS),
        in_specs=[pl.BlockSpec((seq, MLA_QK_PAD), lambda b, h: (b, h)),
                  pl.BlockSpec((seq, MLA_QK_PAD), lambda b, h: (b, h)),
                  pl.BlockSpec((seq, MLA_V_DIM), lambda b, h: (b, h))],
        out_specs=pl.BlockSpec((seq, MLA_V_DIM), lambda b, h: (b, h)),
        out_shape=jax.ShapeDtypeStruct((n, MLA_HEADS * MLA_V_DIM), BF16),
        compiler_params=_cparams(2),
        name="mla_attention",
    )(q, k, v)


def _dil_kernel(q_ref, k_ref, v_ref, o_ref, lse_ref, *, dilation, slopes, scale):
    n = q_ref.shape[0]
    blk = DIL_BLOCK
    ri = lax.broadcasted_iota(jnp.int32, (blk, 2 * blk), 0)
    cj = lax.broadcasted_iota(jnp.int32, (blk, 2 * blk), 1) - blk
    steps = ri - cj
    valid2 = (steps >= 0) & (steps <= blk)
    dist2 = (steps * dilation).astype(F32)
    valid1, dist1 = valid2[:, blk:], dist2[:, blk:]
    lane = lax.broadcasted_iota(jnp.int32, (blk, LANES), 1)

    def block(q0, k0, klen, valid, dist):
        lse_tile = jnp.zeros((blk, LANES), F32)
        for j in range(DIL_HEADS_PER_GROUP):
            cs = slice(j * DIL_HEAD_DIM, (j + 1) * DIL_HEAD_DIM)
            q = q_ref[pl.ds(q0, blk), cs]
            kk = k_ref[pl.ds(k0, klen), cs]
            vv = v_ref[pl.ds(k0, klen), cs]
            s = _nt_dot(q, kk) * scale - slopes[j] * dist
            s = jnp.where(valid, s, -jnp.inf)
            m = jnp.max(s, axis=-1, keepdims=True)
            p = jnp.exp(s - m)
            l = jnp.sum(p, axis=-1, keepdims=True)
            o = jnp.dot(p.astype(BF16), vv, preferred_element_type=F32) / l
            o_ref[pl.ds(q0, blk), cs] = o
            lse_tile = jnp.where(lane == j, m + jnp.log(l), lse_tile)
        lse_ref[pl.ds(q0, blk), :] = lse_tile

    block(0, 0, blk, valid1, dist1)

    def body(i, carry):
        q0 = pl.multiple_of(i * blk, blk)
        k0 = pl.multiple_of(i * blk - blk, blk)
        block(q0, k0, 2 * blk, valid2, dist2)
        return carry

    lax.fori_loop(1, n // blk, body, 0)


def _dilated_group(qkv, g, batch, seq):
    window, dilation = DIL_GROUPS[g]
    assert window // dilation == DIL_BLOCK and seq % (dilation * DIL_BLOCK) == 0
    n = seq // dilation
    w = DIL_GROUP_WIDTH
    per_row = 3 * DIL_WIDTH // w
    qkv_v = qkv.reshape(batch, n, dilation * 3 * DIL_WIDTH)
    slopes = np.float32(2.0) ** (np.float32(-8.0) * np.arange(1, DIL_HEADS + 1, dtype=np.float32)
                                 / np.float32(DIL_HEADS))
    slopes = tuple(float(s) for s in slopes[g * DIL_HEADS_PER_GROUP:(g + 1) * DIL_HEADS_PER_GROUP])

    def in_spec(which):
        off = which * (DIL_WIDTH // w) + g
        return pl.BlockSpec((None, n, w), lambda b, r: (b, 0, r * per_row + off))

    o, lse = pl.pallas_call(
        functools.partial(_dil_kernel, dilation=dilation, slopes=slopes, scale=DIL_HEAD_DIM ** -0.5),
        grid=(batch, dilation),
        in_specs=[in_spec(0), in_spec(1), in_spec(2)],
        out_specs=[pl.BlockSpec((None, n, w), lambda b, r: (b, 0, r)),
                   pl.BlockSpec((None, n, LANES), lambda b, r: (b, 0, r))],
        out_shape=[jax.ShapeDtypeStruct((batch, n, dilation * w), F32),
                   jax.ShapeDtypeStruct((batch, n, dilation * LANES), F32)],
        compiler_params=_cparams(2),
        name=f"dilated_attn_g{g}",
    )(qkv_v, qkv_v, qkv_v)
    return o.reshape(batch * seq, w), lse.reshape(batch * seq, LANES)


def _merge_kernel(omla_ref, o0_ref, o1_ref, o2_ref, l0_ref, l1_ref, l2_ref, sgm_ref, sgd_ref,
                  wm_ref, wd_ref, out_ref, od_ref):
    @pl.when(pl.program_id(1) == 0)
    def _():
        lses = [l0_ref[...], l1_ref[...], l2_ref[...]]
        m = jnp.maximum(jnp.maximum(lses[0], lses[1]), lses[2])
        es = [jnp.exp(l - m) for l in lses]
        den = es[0] + es[1] + es[2]
        for g, o_ref in enumerate((o0_ref, o1_ref, o2_ref)):
            wg = es[g] / den
            for j in range(DIL_HEADS_PER_GROUP):
                c0 = (g * DIL_HEADS_PER_GROUP + j) * DIL_HEAD_DIM
                od_ref[:, c0:c0 + DIL_HEAD_DIM] = (
                    o_ref[:, j * DIL_HEAD_DIM:(j + 1) * DIL_HEAD_DIM] * wg[:, j:j + 1]).astype(BF16)

    a = jnp.dot(omla_ref[...], wm_ref[...], preferred_element_type=F32)
    b = jnp.dot(od_ref[...], wd_ref[...], preferred_element_type=F32)
    out_ref[...] = (sgm_ref[...] * a + sgd_ref[...] * b).astype(out_ref.dtype)


def _merge(o_mla, o_dil, lse_dil, gates, w_o_mla, w_o_dil, d_model):
    n = o_mla.shape[0]
    tm = min(TM_MERGE, n)
    tn = min(1024, d_model)
    nj = d_model // tn
    row = lambda width: pl.BlockSpec((tm, width), lambda i, j: (i, 0))
    return pl.pallas_call(
        _merge_kernel,
        grid=(n // tm, nj),
        in_specs=[row(o_mla.shape[1])] + [row(DIL_GROUP_WIDTH)] * 3 + [row(LANES)] * 3 + [
            pl.BlockSpec((tm, tn), lambda i, j: (i, j)),
            pl.BlockSpec((tm, tn), lambda i, j: (i, j + nj)),
            pl.BlockSpec((w_o_mla.shape[0], tn), lambda i, j: (0, j)),
            pl.BlockSpec((w_o_dil.shape[0], tn), lambda i, j: (0, j))],
        out_specs=pl.BlockSpec((tm, tn), lambda i, j: (i, j)),
        out_shape=jax.ShapeDtypeStruct((n, d_model), BF16),
        scratch_shapes=[pltpu.VMEM((tm, DIL_WIDTH), BF16)],
        compiler_params=_cparams(2),
        name="branch_merge",
    )(o_mla, *o_dil, *lse_dil, gates, gates, w_o_mla, w_o_dil)


def _layer_norm(y, g, b):
    mu = jnp.mean(y, axis=-1, keepdims=True)
    yc = y - mu
    var = jnp.mean(yc * yc, axis=-1, keepdims=True)
    return yc * lax.rsqrt(var + LN_EPS) * g + b


def _ln1_router_kernel(mg_ref, x_ref, wo_ref, g_ref, b_ref, wr_ref, br_ref,
                       x1_ref, idx_ref, gate_ref, *, alpha):
    out = jnp.dot(mg_ref[...], wo_ref[...], preferred_element_type=F32)
    x1 = _layer_norm(alpha * x_ref[...] + out, g_ref[...], b_ref[...])
    x1_ref[...] = x1
    logits = jnp.dot(x1, wr_ref[...], preferred_element_type=F32,
                     precision=lax.Precision.HIGHEST) + br_ref[...]
    lane = lax.broadcasted_iota(jnp.int32, logits.shape, 1).astype(F32)
    vals, idxs = [], []
    for _ in range(TOP_K):
        m = jnp.max(logits, axis=-1, keepdims=True)
        idx = jnp.min(jnp.where(logits == m, lane, float(LANES)), axis=-1, keepdims=True)
        vals.append(m)
        idxs.append(idx)
        logits = jnp.where(lane == idx, -jnp.inf, logits)
    es = [jnp.exp(v - vals[0]) for v in vals]
    den = es[0] + es[1] + es[2] + es[3]
    idx_tile = jnp.zeros(lane.shape, F32)
    gate_tile = jnp.zeros(lane.shape, F32)
    for k in range(TOP_K):
        idx_tile = jnp.where(lane == float(k), idxs[k], idx_tile)
        gate_tile = jnp.where(lane == float(k), es[k] / den, gate_tile)
    idx_ref[...] = idx_tile.astype(jnp.int32)
    gate_ref[...] = gate_tile


def _ln1_router(merged, x, w_out, ln_g, ln_b, w_r, b_r, alpha):
    n, d = x.shape
    tm = min(TM_LN, n)
    full = lambda a: pl.BlockSpec(a.shape, lambda i: (0,) * a.ndim)
    row = lambda width: pl.BlockSpec((tm, width), lambda i: (i, 0))
    return pl.pallas_call(
        functools.partial(_ln1_router_kernel, alpha=alpha),
        grid=(n // tm,),
        in_specs=[row(d), row(d), full(w_out), full(ln_g), full(ln_b), full(w_r), full(b_r)],
        out_specs=[row(d), row(LANES), row(LANES)],
        out_shape=[jax.ShapeDtypeStruct((n, d), F32),
                   jax.ShapeDtypeStruct((n, LANES), jnp.int32),
                   jax.ShapeDtypeStruct((n, LANES), F32)],
        compiler_params=_cparams(1),
        name="out_proj_ln1_router",
    )(merged, x, w_out, ln_g, ln_b, w_r, b_r)


def _gather_kernel(tok_ref, nt_ref, x_hbm, o_ref, buf, sem):
    t = pl.program_id(0)
    rows = buf.shape[0]

    @pl.when(t < nt_ref[0])
    def _():
        def issue(r, carry):
            tok = tok_ref[t * rows + r]
            pltpu.make_async_copy(x_hbm.at[pl.ds(tok, 1)], buf.at[pl.ds(r, 1)], sem).start()
            return carry

        lax.fori_loop(0, rows, issue, 0)
        pltpu.make_async_copy(x_hbm.at[pl.ds(0, rows)], buf, sem).wait()
        o_ref[...] = buf[...].astype(o_ref.dtype)

    @pl.when(t >= nt_ref[0])
    def _():
        o_ref[...] = jnp.zeros(o_ref.shape, o_ref.dtype)


def _moe_gather(x1, tok_buf, n_valid, n_tiles):
    d = x1.shape[1]
    return pl.pallas_call(
        _gather_kernel,
        grid_spec=pltpu.PrefetchScalarGridSpec(
            num_scalar_prefetch=2,
            grid=(n_tiles,),
            in_specs=[pl.BlockSpec(memory_space=pl.ANY)],
            out_specs=pl.BlockSpec((TM_MOE, d), lambda t, tok, nt: (t, 0)),
            scratch_shapes=[pltpu.VMEM((TM_MOE, d), F32), pltpu.SemaphoreType.DMA]),
        out_shape=jax.ShapeDtypeStruct((n_tiles * TM_MOE, d), BF16),
        compiler_params=_cparams(1),
        name="moe_gather",
    )(tok_buf, n_valid, x1)


def _ffn_a_kernel(te_ref, nt_ref, xs_ref, wg_ref, wl_ref, bg_ref, bl_ref, a_ref):
    @pl.when(pl.program_id(1) < nt_ref[0])
    def _():
        x = xs_ref[...]
        hg = jnp.dot(x, wg_ref[...], preferred_element_type=F32) + bg_ref[...]
        hl = jnp.dot(x, wl_ref[...], preferred_element_type=F32) + bl_ref[...]
        x_glu = jnp.minimum(hg, SWIGLU_LIMIT)
        x_lin = jnp.clip(hl, -SWIGLU_LIMIT, SWIGLU_LIMIT)
        a_ref[...] = (x_glu * _sigmoid(SWIGLU_ALPHA * x_glu) * (x_lin + 1.0)).astype(a_ref.dtype)

    @pl.when(pl.program_id(1) >= nt_ref[0])
    def _():
        a_ref[...] = jnp.zeros(a_ref.shape, a_ref.dtype)


def _ffn_b_kernel(te_ref, nt_ref, a_ref, w_ref, b_ref, gate_ref, o_ref):
    @pl.when(pl.program_id(1) < nt_ref[0])
    def _():
        acc = jnp.dot(a_ref[...], w_ref[...], preferred_element_type=F32)
        o_ref[...] = (acc + b_ref[...]) * gate_ref[...]

    @pl.when(pl.program_id(1) >= nt_ref[0])
    def _():
        o_ref[...] = jnp.zeros(o_ref.shape, o_ref.dtype)


def _moe_ffn(xs, tile_expert, n_valid, gate_buf, w1g, w1l, b1g, b1l, w2, b2):
    rows, d = xs.shape
    n_tiles = rows // TM_MOE
    d_ff = w1g.shape[2]
    fc = min(FF_CHUNK, d_ff)
    tile = lambda t, nt: jnp.minimum(t, nt[0] - 1)
    a = pl.pallas_call(
        _ffn_a_kernel,
        grid_spec=pltpu.PrefetchScalarGridSpec(
            num_scalar_prefetch=2,
            grid=(d_ff // fc, n_tiles),
            in_specs=[pl.BlockSpec((TM_MOE, d), lambda f, t, te, nt: (tile(t, nt), 0)),
                      pl.BlockSpec((None, d, fc), lambda f, t, te, nt: (te[t], 0, f)),
                      pl.BlockSpec((None, d, fc), lambda f, t, te, nt: (te[t], 0, f)),
                      pl.BlockSpec((None, 1, fc), lambda f, t, te, nt: (te[t], 0, f)),
                      pl.BlockSpec((None, 1, fc), lambda f, t, te, nt: (te[t], 0, f))],
            out_specs=pl.BlockSpec((TM_MOE, fc), lambda f, t, te, nt: (t, f))),
        out_shape=jax.ShapeDtypeStruct((rows, d_ff), BF16),
        compiler_params=_cparams(2),
        name="moe_ffn_up",
    )(tile_expert, n_valid, xs, w1g, w1l, b1g, b1l)
    nc = min(FF_CHUNK, d)
    return pl.pallas_call(
        _ffn_b_kernel,
        grid_spec=pltpu.PrefetchScalarGridSpec(
            num_scalar_prefetch=2,
            grid=(d // nc, n_tiles),
            in_specs=[pl.BlockSpec((TM_MOE, d_ff), lambda c, t, te, nt: (tile(t, nt), 0)),
                      pl.BlockSpec((None, d_ff, nc), lambda c, t, te, nt: (te[t], 0, c)),
                      pl.BlockSpec((None, 1, nc), lambda c, t, te, nt: (te[t], 0, c)),
                      pl.BlockSpec((TM_MOE, 1), lambda c, t, te, nt: (tile(t, nt), 0))],
            out_specs=pl.BlockSpec((TM_MOE, nc), lambda c, t, te, nt: (t, c))),
        out_shape=jax.ShapeDtypeStruct((rows, d), F32),
        compiler_params=_cparams(2),
        name="moe_ffn_down",
    )(tile_expert, n_valid, a, w2, b2, gate_buf)


def _combine_kernel(pos_ref, y_hbm, x_ref, g_ref, b_ref, o_ref, buf, sem, *, alpha):
    t = pl.program_id(0)
    rows = x_ref.shape[0]

    def issue(r, carry):
        for k in range(TOP_K):
            p = pos_ref[(t * rows + r) * TOP_K + k]
            pltpu.make_async_copy(y_hbm.at[pl.ds(p, 1)], buf.at[pl.ds(k * rows + r, 1)], sem).start()
        return carry

    lax.fori_loop(0, rows, issue, 0)
    pltpu.make_async_copy(y_hbm.at[pl.ds(0, TOP_K * rows)], buf, sem).wait()
    ffn = buf[0:rows, :]
    for k in range(1, TOP_K):
        ffn = ffn + buf[k * rows:(k + 1) * rows, :]
    o_ref[...] = _layer_norm(alpha * x_ref[...] + ffn, g_ref[...], b_ref[...])


def _moe_combine(y_sorted, pos, x1, ln_g, ln_b, alpha):
    n, d = x1.shape
    tm = min(TM_COMBINE, n)
    return pl.pallas_call(
        functools.partial(_combine_kernel, alpha=alpha),
        grid_spec=pltpu.PrefetchScalarGridSpec(
            num_scalar_prefetch=1,
            grid=(n // tm,),
            in_specs=[pl.BlockSpec(memory_space=pl.ANY),
                      pl.BlockSpec((tm, d), lambda t, pos: (t, 0)),
                      pl.BlockSpec((1, d), lambda t, pos: (0, 0)),
                      pl.BlockSpec((1, d), lambda t, pos: (0, 0))],
            out_specs=pl.BlockSpec((tm, d), lambda t, pos: (t, 0)),
            scratch_shapes=[pltpu.VMEM((TOP_K * tm, d), F32), pltpu.SemaphoreType.DMA]),
        out_shape=jax.ShapeDtypeStruct((n, d), F32),
        compiler_params=_cparams(1),
        name="moe_combine_ln2",
    )(pos, y_sorted, x1, ln_g, ln_b)


def _route(top_idx, top_gate):
    n = top_idx.shape[0]
    nk = n * TOP_K
    e_flat = top_idx.reshape(-1)
    order = jnp.argsort(e_flat, stable=True).astype(jnp.int32)
    e_sorted = e_flat[order]
    counts = jnp.bincount(e_flat, length=N_EXPERTS).astype(jnp.int32)
    padded = (counts + TM_MOE - 1) // TM_MOE * TM_MOE
    pad_end = jnp.cumsum(padded)
    pad_start = pad_end - padded
    start = jnp.cumsum(counts) - counts
    dest = pad_start[e_sorted] + jnp.arange(nk, dtype=jnp.int32) - start[e_sorted]
    n_tiles = -(-nk // TM_MOE) + N_EXPERTS
    cap = n_tiles * TM_MOE
    tok_buf = jnp.zeros((cap,), jnp.int32).at[dest].set(order // TOP_K)
    gate_buf = jnp.zeros((cap,), F32).at[dest].set(top_gate.reshape(-1)[order])
    pos = jnp.zeros((nk,), jnp.int32).at[order].set(dest)
    n_valid = (pad_end[-1] // TM_MOE).astype(jnp.int32)
    tile_start = jnp.minimum(jnp.arange(n_tiles, dtype=jnp.int32), n_valid - 1) * TM_MOE
    tile_expert = jnp.minimum(jnp.searchsorted(pad_end, tile_start, side="right"),
                              N_EXPERTS - 1).astype(jnp.int32)
    return tok_buf, gate_buf.reshape(cap, 1), pos, tile_expert, n_valid.reshape(1), n_tiles


def _rope_tables(seq):
    half = MLA_ROPE_DIM // 2
    inv = ROPE_THETA ** (-jnp.arange(half, dtype=F32) / half)
    ang = jnp.arange(seq).astype(F32)[:, None] * inv[None, :]
    cos, sin = jnp.cos(ang), jnp.sin(ang)
    z = jnp.zeros((seq, LANES - MLA_ROPE_DIM), F32)
    zh = jnp.zeros((seq, half), F32)
    return (jnp.concatenate([cos, cos, z], axis=1),
            jnp.concatenate([-sin, zh, z], axis=1),
            jnp.concatenate([zh, sin, z], axis=1))


def _pad_cols(w, width):
    return jnp.pad(w, ((0, 0), (0, width - w.shape[1])))


def kernel(x, w_in, q_norm_g, kv_norm_g, w_uq, w_ukv, w_o_mla, w_o_dil, w_out, ln1_g, ln1_b,
           w_router, b_router, w1, b1, w2, b2, ln2_g, ln2_b):
    batch, seq, d_model = x.shape
    depth = w_in.shape[0]
    n = batch * seq
    alpha = (2.0 * depth) ** 0.25
    tabs = _rope_tables(seq)
    o_q, o_kv, o_pe = 0, MLA_Q_RANK, MLA_Q_RANK + MLA_KV_RANK
    o_dil = o_pe + MLA_ROPE_DIM
    o_gate = o_dil + 3 * DIL_WIDTH
    xt = x.reshape(n, d_model)
    for l in range(depth):
        wi = w_in[l]
        w_a = jnp.concatenate([wi[:, o_kv:o_pe], _pad_cols(wi[:, o_pe:o_dil], 2 * LANES),
                               wi[:, o_q:o_kv]], axis=1).astype(BF16)
        w_b = wi[:, o_dil:o_gate].astype(BF16)
        w_c = wi[:, o_gate:].astype(BF16)
        wq = w_uq[l].reshape(MLA_Q_RANK, MLA_HEADS, MLA_QK_DIM)
        wq = jnp.pad(wq, ((0, 0), (0, 0), (0, MLA_QK_PAD - MLA_QK_DIM)))
        wq = wq.reshape(MLA_Q_RANK, MLA_HEADS * MLA_QK_PAD).astype(BF16)

        proj_a = _proj(xt, w_a, F32, w_a.shape[1], False, "in_proj_latent")
        qkv_d = _proj(xt, w_b, BF16, DIL_WIDTH, False, "in_proj_dilated")
        gates = _proj(xt, w_c, F32, 1024, True, "in_proj_gates")

        q, k, v = _mla_up(proj_a, q_norm_g[l][None], kv_norm_g[l][None], wq,
                          w_ukv[l].astype(BF16), tabs, seq)
        o_mla = _mla_attention(q, k, v, batch, seq)
        dil = [_dilated_group(qkv_d, g, batch, seq) for g in range(len(DIL_GROUPS))]
        merged = _merge(o_mla, [o for o, _ in dil], [s for _, s in dil], gates,
                        w_o_mla[l].astype(BF16), w_o_dil[l].astype(BF16), d_model)

        w_r = _pad_cols(w_router[l], LANES)
        b_r = jnp.concatenate([b_router[l], jnp.full((LANES - N_EXPERTS,), NEG_BIG, F32)])[None]
        x1, top_idx, top_gate = _ln1_router(merged, xt, w_out[l].astype(BF16), ln1_g[l][None],
                                            ln1_b[l][None], w_r, b_r, alpha)

        tok_buf, gate_buf, pos, tile_expert, n_valid, n_tiles = _route(
            top_idx[:, :TOP_K], top_gate[:, :TOP_K])
        xs = _moe_gather(x1, tok_buf, n_valid, n_tiles)
        d_ff = w2.shape[2]
        w1p = w1[l].reshape(N_EXPERTS, d_model, d_ff, 2)
        b1p = b1[l].reshape(N_EXPERTS, 1, d_ff, 2)
        y_sorted = _moe_ffn(xs, tile_expert, n_valid, gate_buf,
                            w1p[..., 0].astype(BF16), w1p[..., 1].astype(BF16),
                            b1p[..., 0], b1p[..., 1], w2[l].astype(BF16), b2[l][:, None, :])
        xt = _moe_combine(y_sorted, pos, x1, ln2_g[l][None], ln2_b[l][None], alpha)
    return xt.reshape(batch, seq, d_model)
```

```python
import functools

import numpy as np
import jax
import jax.numpy as jnp
from jax import lax
from jax.experimental import pallas as pl
from jax.experimental.pallas import tpu as pltpu

F32 = jnp.float32
BF16 = jnp.bfloat16

MLA_HEADS = 16
MLA_Q_RANK = 768
MLA_KV_RANK = 512
MLA_NOPE_DIM = 128
MLA_ROPE_DIM = 64
MLA_V_DIM = 128
MLA_QK_DIM = MLA_NOPE_DIM + MLA_ROPE_DIM
MLA_QK_PAD = 256
ROPE_THETA = 10000.0
DIL_GROUPS = ((128, 1), (512, 4), (2048, 16))
DIL_HEADS_PER_GROUP = 4
DIL_HEADS = DIL_HEADS_PER_GROUP * len(DIL_GROUPS)
DIL_HEAD_DIM = 128
DIL_GROUP_WIDTH = DIL_HEADS_PER_GROUP * DIL_HEAD_DIM
DIL_WIDTH = DIL_HEADS * DIL_HEAD_DIM
DIL_BLOCK = 128
N_EXPERTS = 32
TOP_K = 4
SWIGLU_LIMIT = 7.0
SWIGLU_ALPHA = 1.702
LN_EPS = 1e-5
RMS_EPS = 1e-6
LANES = 128
NEG_BIG = -1e30

VMEM_LIMIT = 56 * 1024 * 1024
TM_PROJ = 1024
TM_UP = 512
TM_MERGE = 512
TM_LN = 512
TM_MOE = 512
TM_SCATTER = 512
TM_COMBINE = 256
W1_COLS = 1024
W2_COLS = 1024
PAIR = LANES // 2


def _cparams(n_axes):
    return pltpu.CompilerParams(dimension_semantics=("arbitrary",) * n_axes,
                                vmem_limit_bytes=VMEM_LIMIT)


def _sigmoid(x):
    return 1.0 / (1.0 + jnp.exp(-x))


def _proj_kernel(x_ref, w_ref, o_ref, xb_ref, *, act):
    @pl.when(pl.program_id(1) == 0)
    def _():
        xb_ref[...] = x_ref[...].astype(BF16)

    acc = jnp.dot(xb_ref[...], w_ref[...], preferred_element_type=F32)
    if act:
        acc = _sigmoid(acc)
    o_ref[...] = acc.astype(o_ref.dtype)


def _proj(x, w, out_dtype, tn, act, name):
    m, k = x.shape
    nc = w.shape[1]
    tm = min(TM_PROJ, m)
    return pl.pallas_call(
        functools.partial(_proj_kernel, act=act),
        grid=(m // tm, nc // tn),
        in_specs=[pl.BlockSpec((tm, k), lambda i, j: (i, 0)),
                  pl.BlockSpec((k, tn), lambda i, j: (0, j))],
        out_specs=pl.BlockSpec((tm, tn), lambda i, j: (i, j)),
        out_shape=jax.ShapeDtypeStruct((m, nc), out_dtype),
        scratch_shapes=[pltpu.VMEM((tm, k), BF16)],
        compiler_params=_cparams(2),
        name=name,
    )(x, w)


def _rope_tile(r, cos, sin_lo, sin_hi):
    half = MLA_ROPE_DIM // 2
    return (r * cos + pltpu.roll(r, LANES - half, 1) * sin_lo + pltpu.roll(r, half, 1) * sin_hi)


def _rms(x, g):
    ms = jnp.mean(x * x, axis=-1, keepdims=True)
    return x * lax.rsqrt(ms + RMS_EPS) * g


def _q_up_kernel(cq_ref, g_ref, w_ref, cos_ref, slo_ref, shi_ref, o_ref):
    cqn = _rms(cq_ref[...], g_ref[...]).astype(BF16)
    cos, slo, shi = cos_ref[...], slo_ref[...], shi_ref[...]
    for h in range(MLA_HEADS):
        c0 = h * MLA_QK_PAD
        qh = jnp.dot(cqn, w_ref[:, c0:c0 + MLA_QK_PAD], preferred_element_type=F32)
        o_ref[:, c0:c0 + LANES] = qh[:, :LANES].astype(BF16)
        o_ref[:, c0 + LANES:c0 + MLA_QK_PAD] = _rope_tile(qh[:, LANES:], cos, slo, shi).astype(BF16)


def _kv_up_kernel(ckv_ref, kpe_ref, g_ref, w_ref, cos_ref, slo_ref, shi_ref, k_ref, v_ref):
    ckvn = _rms(ckv_ref[...], g_ref[...]).astype(BF16)
    k_rot = _rope_tile(kpe_ref[...], cos_ref[...], slo_ref[...], shi_ref[...]).astype(BF16)
    for h in range(MLA_HEADS):
        c0 = h * (MLA_NOPE_DIM + MLA_V_DIM)
        kv = jnp.dot(ckvn, w_ref[:, c0:c0 + MLA_NOPE_DIM + MLA_V_DIM], preferred_element_type=F32)
        k0 = h * MLA_QK_PAD
        k_ref[:, k0:k0 + LANES] = kv[:, :MLA_NOPE_DIM].astype(BF16)
        k_ref[:, k0 + LANES:k0 + MLA_QK_PAD] = k_rot
        v_ref[:, h * MLA_V_DIM:(h + 1) * MLA_V_DIM] = kv[:, MLA_NOPE_DIM:].astype(BF16)


def _mla_up(proj_a, q_g, kv_g, w_uq, w_ukv, tabs, seq):
    n = proj_a.shape[0]
    tm = min(TM_UP, seq)
    sb = seq // tm
    tab_specs = [pl.BlockSpec((tm, LANES), lambda i: (i % sb, 0))] * 3
    qw = MLA_HEADS * MLA_QK_PAD
    q = pl.pallas_call(
        _q_up_kernel,
        grid=(n // tm,),
        in_specs=[pl.BlockSpec((tm, MLA_Q_RANK), lambda i: (i, 1)),
                  pl.BlockSpec((1, MLA_Q_RANK), lambda i: (0, 0)),
                  pl.BlockSpec((MLA_Q_RANK, qw), lambda i: (0, 0))] + tab_specs,
        out_specs=pl.BlockSpec((tm, qw), lambda i: (i, 0)),
        out_shape=jax.ShapeDtypeStruct((n, qw), BF16),
        compiler_params=_cparams(1),
        name="mla_q_up",
    )(proj_a, q_g, w_uq, *tabs)
    kvw = MLA_HEADS * (MLA_NOPE_DIM + MLA_V_DIM)
    k, v = pl.pallas_call(
        _kv_up_kernel,
        grid=(n // tm,),
        in_specs=[pl.BlockSpec((tm, MLA_KV_RANK), lambda i: (i, 0)),
                  pl.BlockSpec((tm, LANES), lambda i: (i, MLA_KV_RANK // LANES)),
                  pl.BlockSpec((1, MLA_KV_RANK), lambda i: (0, 0)),
                  pl.BlockSpec((MLA_KV_RANK, kvw), lambda i: (0, 0))] + tab_specs,
        out_specs=[pl.BlockSpec((tm, qw), lambda i: (i, 0)),
                   pl.BlockSpec((tm, MLA_HEADS * MLA_V_DIM), lambda i: (i, 0))],
        out_shape=[jax.ShapeDtypeStruct((n, qw), BF16),
                   jax.ShapeDtypeStruct((n, MLA_HEADS * MLA_V_DIM), BF16)],
        compiler_params=_cparams(1),
        name="mla_kv_up",
    )(proj_a, proj_a, kv_g, w_ukv, *tabs)
    return q, k, v


def _nt_dot(a, b):
    return lax.dot_general(a, b, (((1,), (1,)), ((), ())), preferred_element_type=F32)


def _mla_attn_kernel(q_ref, k_ref, v_ref, o_ref, *, tq, scale):
    seq = q_ref.shape[0]
    row = lax.broadcasted_iota(jnp.int32, (tq, tq), 0)
    col = lax.broadcasted_iota(jnp.int32, (tq, tq), 1)
    causal = col <= row
    for qi in range(seq // tq):
        r0 = qi * tq
        q = q_ref[r0:r0 + tq, :]
        s_d = jnp.where(causal, _nt_dot(q, k_ref[r0:r0 + tq, :]) * scale, -jnp.inf)
        m = jnp.max(s_d, axis=-1, keepdims=True)
        if qi > 0:
            s_p = _nt_dot(q, k_ref[0:r0, :]) * scale
            m = jnp.maximum(m, jnp.max(s_p, axis=-1, keepdims=True))
        p_d = jnp.exp(s_d - m)
        l = jnp.sum(p_d, axis=-1, keepdims=True)
        acc = jnp.dot(p_d.astype(BF16), v_ref[r0:r0 + tq, :], preferred_element_type=F32)
        if qi > 0:
            p_p = jnp.exp(s_p - m)
            l = l + jnp.sum(p_p, axis=-1, keepdims=True)
            acc = acc + jnp.dot(p_p.astype(BF16), v_ref[0:r0, :], preferred_element_type=F32)
        o_ref[r0:r0 + tq, :] = (acc / l).astype(o_ref.dtype)


def _mla_attention(q, k, v, batch, seq):
    n = q.shape[0]
    return pl.pallas_call(
        functools.partial(_mla_attn_kernel, tq=min(256, seq), scale=MLA_QK_DIM ** -0.5),
        grid=(batch, MLA_HEADS),
        in_specs=[pl.BlockSpec((seq, MLA_QK_PAD), lambda b, h: (b, h)),
                  pl.BlockSpec((seq, MLA_QK_PAD), lambda b, h: (b, h)),
                  pl.BlockSpec((seq, MLA_V_DIM), lambda b, h: (b, h))],
        out_specs=pl.BlockSpec((seq, MLA_V_DIM), lambda b, h: (b, h)),
        out_shape=jax.ShapeDtypeStruct((n, MLA_HEADS * MLA_V_DIM), BF16),
        compiler_params=_cparams(2),
        name="mla_attention",
    )(q, k, v)


def _dil_kernel(q_ref, k_ref, v_ref, o_ref, lse_ref, *, dilation, slopes, scale):
    n = q_ref.shape[0]
    blk = DIL_BLOCK
    ri = lax.broadcasted_iota(jnp.int32, (blk, 2 * blk), 0)
    cj = lax.broadcasted_iota(jnp.int32, (blk, 2 * blk), 1) - blk
    steps = ri - cj
    valid2 = (steps >= 0) & (steps <= blk)
    dist2 = (steps * dilation).astype(F32)
    valid1, dist1 = valid2[:, blk:], dist2[:, blk:]
    lane = lax.broadcasted_iota(jnp.int32, (blk, LANES), 1)

    def block(q0, k0, klen, valid, dist):
        lse_tile = jnp.zeros((blk, LANES), F32)
        for j in range(DIL_HEADS_PER_GROUP):
            cs = slice(j * DIL_HEAD_DIM, (j + 1) * DIL_HEAD_DIM)
            q = q_ref[pl.ds(q0, blk), cs]
            kk = k_ref[pl.ds(k0, klen), cs]
            vv = v_ref[pl.ds(k0, klen), cs]
            s = _nt_dot(q, kk) * scale - slopes[j] * dist
            s = jnp.where(valid, s, -jnp.inf)
            m = jnp.max(s, axis=-1, keepdims=True)
            p = jnp.exp(s - m)
            l = jnp.sum(p, axis=-1, keepdims=True)
            o = jnp.dot(p.astype(BF16), vv, preferred_element_type=F32) / l
            o_ref[pl.ds(q0, blk), cs] = o
            lse_tile = jnp.where(lane == j, m + jnp.log(l), lse_tile)
        lse_ref[pl.ds(q0, blk), :] = lse_tile

    block(0, 0, blk, valid1, dist1)

    def body(i, carry):
        q0 = pl.multiple_of(i * blk, blk)
        k0 = pl.multiple_of(i * blk - blk, blk)
        block(q0, k0, 2 * blk, valid2, dist2)
        return carry

    lax.fori_loop(1, n // blk, body, 0)


def _dilated_group(qkv, g, batch, seq):
    window, dilation = DIL_GROUPS[g]
    assert window // dilation == DIL_BLOCK and seq % (dilation * DIL_BLOCK) == 0
    n = seq // dilation
    w = DIL_GROUP_WIDTH
    per_row = 3 * DIL_WIDTH // w
    qkv_v = qkv.reshape(batch, n, dilation * 3 * DIL_WIDTH)
    slopes = np.float32(2.0) ** (np.float32(-8.0) * np.arange(1, DIL_HEADS + 1, dtype=np.float32)
                                 / np.float32(DIL_HEADS))
    slopes = tuple(float(s) for s in slopes[g * DIL_HEADS_PER_GROUP:(g + 1) * DIL_HEADS_PER_GROUP])

    def in_spec(which):
        off = which * (DIL_WIDTH // w) + g
        return pl.BlockSpec((None, n, w), lambda b, r: (b, 0, r * per_row + off))

    o, lse = pl.pallas_call(
        functools.partial(_dil_kernel, dilation=dilation, slopes=slopes, scale=DIL_HEAD_DIM ** -0.5),
        grid=(batch, dilation),
        in_specs=[in_spec(0), in_spec(1), in_spec(2)],
        out_specs=[pl.BlockSpec((None, n, w), lambda b, r: (b, 0, r)),
                   pl.BlockSpec((None, n, LANES), lambda b, r: (b, 0, r))],
        out_shape=[jax.ShapeDtypeStruct((batch, n, dilation * w), F32),
                   jax.ShapeDtypeStruct((batch, n, dilation * LANES), F32)],
        compiler_params=_cparams(2),
        name=f"dilated_attn_g{g}",
    )(qkv_v, qkv_v, qkv_v)
    return o.reshape(batch * seq, w), lse.reshape(batch * seq, LANES)


def _merge_kernel(omla_ref, o0_ref, o1_ref, o2_ref, l0_ref, l1_ref, l2_ref, sgm_ref, sgd_ref,
                  wm_ref, wd_ref, out_ref, od_ref):
    @pl.when(pl.program_id(1) == 0)
    def _():
        lses = [l0_ref[...], l1_ref[...], l2_ref[...]]
        m = jnp.maximum(jnp.maximum(lses[0], lses[1]), lses[2])
        es = [jnp.exp(l - m) for l in lses]
        den = es[0] + es[1] + es[2]
        for g, o_ref in enumerate((o0_ref, o1_ref, o2_ref)):
            wg = es[g] / den
            for j in range(DIL_HEADS_PER_GROUP):
                c0 = (g * DIL_HEADS_PER_GROUP + j) * DIL_HEAD_DIM
                od_ref[:, c0:c0 + DIL_HEAD_DIM] = (
                    o_ref[:, j * DIL_HEAD_DIM:(j + 1) * DIL_HEAD_DIM] * wg[:, j:j + 1]).astype(BF16)

    a = jnp.dot(omla_ref[...], wm_ref[...], preferred_element_type=F32)
    b = jnp.dot(od_ref[...], wd_ref[...], preferred_element_type=F32)
    out_ref[...] = (sgm_ref[...] * a + sgd_ref[...] * b).astype(out_ref.dtype)


def _merge(o_mla, o_dil, lse_dil, gates, w_o_mla, w_o_dil, d_model):
    n = o_mla.shape[0]
    tm = min(TM_MERGE, n)
    tn = min(1024, d_model)
    nj = d_model // tn
    row = lambda width: pl.BlockSpec((tm, width), lambda i, j: (i, 0))
    return pl.pallas_call(
        _merge_kernel,
        grid=(n // tm, nj),
        in_specs=[row(o_mla.shape[1])] + [row(DIL_GROUP_WIDTH)] * 3 + [row(LANES)] * 3 + [
            pl.BlockSpec((tm, tn), lambda i, j: (i, j)),
            pl.BlockSpec((tm, tn), lambda i, j: (i, j + nj)),
            pl.BlockSpec((w_o_mla.shape[0], tn), lambda i, j: (0, j)),
            pl.BlockSpec((w_o_dil.shape[0], tn), lambda i, j: (0, j))],
        out_specs=pl.BlockSpec((tm, tn), lambda i, j: (i, j)),
        out_shape=jax.ShapeDtypeStruct((n, d_model), BF16),
        scratch_shapes=[pltpu.VMEM((tm, DIL_WIDTH), BF16)],
        compiler_params=_cparams(2),
        name="branch_merge",
    )(o_mla, *o_dil, *lse_dil, gates, gates, w_o_mla, w_o_dil)


def _layer_norm(y, g, b):
    mu = jnp.mean(y, axis=-1, keepdims=True)
    yc = y - mu
    var = jnp.mean(yc * yc, axis=-1, keepdims=True)
    return yc * lax.rsqrt(var + LN_EPS) * g + b


def _pack_bf16_pairs(x):
    half = x.shape[1] // 2
    xb = x.astype(BF16).astype(F32)
    lo = lax.bitcast_convert_type(xb[:, :half], jnp.uint32) >> 16
    hi = lax.bitcast_convert_type(xb[:, half:], jnp.uint32) & jnp.uint32(0xFFFF0000)
    return lo | hi


def _unpack_bf16_pairs(w):
    lo = lax.bitcast_convert_type(w << 16, F32).astype(BF16)
    hi = lax.bitcast_convert_type(w & jnp.uint32(0xFFFF0000), F32).astype(BF16)
    return lo, hi


def _ln1_router_kernel(mg_ref, x_ref, wo_ref, g_ref, b_ref, wrh_ref, wrl_ref, br_ref,
                       x1_ref, xpk_ref, idx_ref, gate_ref, rank_ref, cnt_ref, tri_ref, run_ref, *, alpha):
    tm = x_ref.shape[0]

    @pl.when(pl.program_id(0) == 0)
    def _():
        r = lax.broadcasted_iota(jnp.int32, (tm, tm), 0)
        c = lax.broadcasted_iota(jnp.int32, (tm, tm), 1)
        tri_ref[...] = jnp.where(c < r, 1.0, 0.0).astype(BF16)
        run_ref[...] = jnp.zeros(run_ref.shape, F32)

    out = jnp.dot(mg_ref[...], wo_ref[...], preferred_element_type=F32)
    x1 = _layer_norm(alpha * x_ref[...] + out, g_ref[...], b_ref[...])
    x1_ref[...] = x1
    xpk_ref[...] = _pack_bf16_pairs(x1)
    xh = x1.astype(BF16)
    xl = (x1 - xh.astype(F32)).astype(BF16)
    logits = (jnp.dot(xh, wrh_ref[...], preferred_element_type=F32)
              + jnp.dot(xl, wrh_ref[...], preferred_element_type=F32)
              + jnp.dot(xh, wrl_ref[...], preferred_element_type=F32)) + br_ref[...]
    lane = lax.broadcasted_iota(jnp.int32, logits.shape, 1).astype(F32)
    vals, idxs = [], []
    for _ in range(TOP_K):
        m = jnp.max(logits, axis=-1, keepdims=True)
        idx = jnp.min(jnp.where(logits == m, lane, float(LANES)), axis=-1, keepdims=True)
        vals.append(m)
        idxs.append(idx)
        logits = jnp.where(lane == idx, -jnp.inf, logits)
    es = [jnp.exp(v - vals[0]) for v in vals]
    den = es[0] + es[1] + es[2] + es[3]
    onehot = jnp.zeros(lane.shape, F32)
    for k in range(TOP_K):
        onehot = onehot + jnp.where(lane == idxs[k], 1.0, 0.0)
    before = run_ref[...] + jnp.dot(tri_ref[...], onehot.astype(BF16), preferred_element_type=F32)
    run_ref[...] = run_ref[...] + jnp.sum(onehot, axis=0, keepdims=True)
    cnt_ref[...] = run_ref[...].astype(jnp.int32)
    idx_tile = jnp.zeros(lane.shape, F32)
    gate_tile = jnp.zeros(lane.shape, F32)
    rank_tile = jnp.zeros(lane.shape, F32)
    for k in range(TOP_K):
        rank = jnp.sum(jnp.where(lane == idxs[k], before, 0.0), axis=-1, keepdims=True)
        idx_tile = jnp.where(lane == float(k), idxs[k], idx_tile)
        gate_tile = jnp.where(lane == float(k), es[k] / den, gate_tile)
        rank_tile = jnp.where(lane == float(k), rank, rank_tile)
    idx_ref[...] = idx_tile.astype(jnp.int32)
    gate_ref[...] = gate_tile
    rank_ref[...] = rank_tile.astype(jnp.int32)


def _ln1_router(merged, x, w_out, ln_g, ln_b, w_rh, w_rl, b_r, alpha):
    n, d = x.shape
    tm = min(TM_LN, n)
    full = lambda a: pl.BlockSpec(a.shape, lambda i: (0,) * a.ndim)
    row = lambda width: pl.BlockSpec((tm, width), lambda i: (i, 0))
    return pl.pallas_call(
        functools.partial(_ln1_router_kernel, alpha=alpha),
        grid=(n // tm,),
        in_specs=[row(d), row(d), full(w_out), full(ln_g), full(ln_b), full(w_rh), full(w_rl), full(b_r)],
        out_specs=[row(d), row(d // 2), row(LANES), row(LANES), row(LANES),
                   pl.BlockSpec((1, LANES), lambda i: (0, 0))],
        out_shape=[jax.ShapeDtypeStruct((n, d), F32),
                   jax.ShapeDtypeStruct((n, d // 2), jnp.uint32),
                   jax.ShapeDtypeStruct((n, LANES), jnp.int32),
                   jax.ShapeDtypeStruct((n, LANES), F32),
                   jax.ShapeDtypeStruct((n, LANES), jnp.int32),
                   jax.ShapeDtypeStruct((1, LANES), jnp.int32)],
        scratch_shapes=[pltpu.VMEM((tm, tm), BF16), pltpu.VMEM((1, LANES), F32)],
        compiler_params=_cparams(1),
        name="out_proj_ln1_router",
    )(merged, x, w_out, ln_g, ln_b, w_rh, w_rl, b_r)


def _scatter_kernel(dest_ref, zrow_ref, xpk_hbm, xs_hbm, zbuf, sem, zsem):
    t = pl.program_id(0)
    rows = TM_SCATTER

    @pl.when(t == 0)
    def _():
        zbuf[...] = jnp.zeros(zbuf.shape, zbuf.dtype)

        def zero_copy(i):
            row = pl.multiple_of(jnp.maximum(zrow_ref[i], 0), TM_MOE)
            return pltpu.make_async_copy(zbuf, xs_hbm.at[pl.ds(row, TM_MOE)], zsem)

        def zstart(i, carry):
            @pl.when(zrow_ref[i] >= 0)
            def _():
                zero_copy(i).start()
            return carry

        def zwait(i, carry):
            @pl.when(zrow_ref[i] >= 0)
            def _():
                zero_copy(i).wait()
            return carry

        lax.fori_loop(0, zrow_ref.shape[0], zstart, 0)
        lax.fori_loop(0, zrow_ref.shape[0], zwait, 0)

    def issue(r, carry):
        tok = t * rows + r
        for k in range(TOP_K):
            dst = dest_ref[tok * TOP_K + k]
            pltpu.make_async_copy(xpk_hbm.at[pl.ds(tok, 1)], xs_hbm.at[pl.ds(dst, 1)], sem).start()
        return carry

    lax.fori_loop(0, rows, issue, 0)
    pltpu.make_async_copy(xpk_hbm.at[pl.ds(0, TOP_K * rows)], xs_hbm.at[pl.ds(0, TOP_K * rows)], sem).wait()


def _moe_scatter(xpk, dest, zrows, n_tiles):
    n, w = xpk.shape
    return pl.pallas_call(
        _scatter_kernel,
        grid_spec=pltpu.PrefetchScalarGridSpec(
            num_scalar_prefetch=2,
            grid=(n // TM_SCATTER,),
            in_specs=[pl.BlockSpec(memory_space=pl.ANY)],
            out_specs=pl.BlockSpec(memory_space=pl.ANY),
            scratch_shapes=[pltpu.VMEM((TM_MOE, w), xpk.dtype), pltpu.SemaphoreType.DMA,
                            pltpu.SemaphoreType.DMA]),
        out_shape=jax.ShapeDtypeStruct((n_tiles * TM_MOE, w), xpk.dtype),
        compiler_params=_cparams(1),
        name="moe_scatter",
    )(dest, zrows, xpk)


def _new_expert(te_ref, t):
    return (t == 0) | (te_ref[t] != te_ref[jnp.maximum(t - 1, 0)])


def _ffn_a_kernel(te_ref, nt_ref, xs_ref, w_ref, b_ref, a_ref, wb_ref):
    t = pl.program_id(1)

    @pl.when(_new_expert(te_ref, t))
    def _():
        wb_ref[...] = w_ref[...].astype(BF16)

    @pl.when(t < nt_ref[0])
    def _():
        lo, hi = _unpack_bf16_pairs(xs_ref[...])
        half = lo.shape[1]
        h = (jnp.dot(lo, wb_ref[0:half, :], preferred_element_type=F32)
             + jnp.dot(hi, wb_ref[half:, :], preferred_element_type=F32) + b_ref[...])
        even = lax.broadcasted_iota(jnp.int32, (h.shape[0], LANES), 1) % 2 == 0
        for g in range(h.shape[1] // (2 * LANES)):
            h0 = h[:, 2 * g * LANES:(2 * g + 1) * LANES]
            h1 = h[:, (2 * g + 1) * LANES:(2 * g + 2) * LANES]
            glu = jnp.where(even, h0, pltpu.roll(h1, 1, 1))
            lin = jnp.where(even, pltpu.roll(h0, LANES - 1, 1), h1)
            x_glu = jnp.minimum(glu, SWIGLU_LIMIT)
            x_lin = jnp.clip(lin, -SWIGLU_LIMIT, SWIGLU_LIMIT)
            a_ref[:, g * LANES:(g + 1) * LANES] = (
                x_glu * _sigmoid(SWIGLU_ALPHA * x_glu) * (x_lin + 1.0)).astype(a_ref.dtype)

    @pl.when(t >= nt_ref[0])
    def _():
        a_ref[...] = jnp.zeros(a_ref.shape, a_ref.dtype)


def _ffn_b_kernel(te_ref, nt_ref, a_ref, w_ref, b_ref, o_ref):
    @pl.when(pl.program_id(1) < nt_ref[0])
    def _():
        o_ref[...] = jnp.dot(a_ref[...], w_ref[...], preferred_element_type=F32) + b_ref[...]

    @pl.when(pl.program_id(1) >= nt_ref[0])
    def _():
        o_ref[...] = jnp.zeros(o_ref.shape, o_ref.dtype)


def _moe_ffn(xs, tile_expert, n_valid, w1, b1, w2p, b2):
    rows = xs.shape[0]
    n_tiles = rows // TM_MOE
    _, d, cols = w1.shape
    d_ff = cols // 2
    wc = min(W1_COLS, cols)
    tile = lambda t, nt: jnp.minimum(t, nt[0] - 1)
    a = pl.pallas_call(
        _ffn_a_kernel,
        grid_spec=pltpu.PrefetchScalarGridSpec(
            num_scalar_prefetch=2,
            grid=(cols // wc, n_tiles),
            in_specs=[pl.BlockSpec((TM_MOE, d // 2), lambda f, t, te, nt: (tile(t, nt), 0)),
                      pl.BlockSpec((None, d, wc), lambda f, t, te, nt: (te[t], 0, f)),
                      pl.BlockSpec((None, 1, wc), lambda f, t, te, nt: (te[t], 0, f))],
            out_specs=pl.BlockSpec((TM_MOE, wc // 2), lambda f, t, te, nt: (t, f)),
            scratch_shapes=[pltpu.VMEM((d, wc), BF16)]),
        out_shape=jax.ShapeDtypeStruct((rows, d_ff), BF16),
        compiler_params=_cparams(2),
        name="moe_ffn_up",
    )(tile_expert, n_valid, xs, w1, b1)
    nc = min(W2_COLS, d)
    return pl.pallas_call(
        _ffn_b_kernel,
        grid_spec=pltpu.PrefetchScalarGridSpec(
            num_scalar_prefetch=2,
            grid=(d // nc, n_tiles),
            in_specs=[pl.BlockSpec((TM_MOE, d_ff), lambda c, t, te, nt: (tile(t, nt), 0)),
                      pl.BlockSpec((None, d_ff, nc), lambda c, t, te, nt: (te[t], 0, c)),
                      pl.BlockSpec((None, 1, nc), lambda c, t, te, nt: (te[t], 0, c))],
            out_specs=pl.BlockSpec((TM_MOE, nc), lambda c, t, te, nt: (t, c))),
        out_shape=jax.ShapeDtypeStruct((rows, d), F32),
        compiler_params=_cparams(2),
        name="moe_ffn_down",
    )(tile_expert, n_valid, a, w2p, b2)


def _combine_kernel(pos_ref, y_hbm, x_ref, gate_ref, g_ref, b_ref, o_ref, buf, sem, *, alpha):
    t = pl.program_id(0)
    rows = x_ref.shape[0]

    def issue(r, carry):
        for k in range(TOP_K):
            p = pos_ref[(t * rows + r) * TOP_K + k]
            pltpu.make_async_copy(y_hbm.at[pl.ds(p, 1)], buf.at[pl.ds(k * rows + r, 1)], sem).start()
        return carry

    lax.fori_loop(0, rows, issue, 0)
    pltpu.make_async_copy(y_hbm.at[pl.ds(0, TOP_K * rows)], buf, sem).wait()
    gate = gate_ref[...]
    ffn = buf[0:rows, :] * gate[:, 0:1]
    for k in range(1, TOP_K):
        ffn = ffn + buf[k * rows:(k + 1) * rows, :] * gate[:, k:k + 1]
    o_ref[...] = _layer_norm(alpha * x_ref[...] + ffn, g_ref[...], b_ref[...])


def _moe_combine(y_sorted, pos, x1, gate, ln_g, ln_b, alpha):
    n, d = x1.shape
    tm = min(TM_COMBINE, n)
    return pl.pallas_call(
        functools.partial(_combine_kernel, alpha=alpha),
        grid_spec=pltpu.PrefetchScalarGridSpec(
            num_scalar_prefetch=1,
            grid=(n // tm,),
            in_specs=[pl.BlockSpec(memory_space=pl.ANY),
                      pl.BlockSpec((tm, d), lambda t, pos: (t, 0)),
                      pl.BlockSpec((tm, LANES), lambda t, pos: (t, 0)),
                      pl.BlockSpec((1, d), lambda t, pos: (0, 0)),
                      pl.BlockSpec((1, d), lambda t, pos: (0, 0))],
            out_specs=pl.BlockSpec((tm, d), lambda t, pos: (t, 0)),
            scratch_shapes=[pltpu.VMEM((TOP_K * tm, d), F32), pltpu.SemaphoreType.DMA]),
        out_shape=jax.ShapeDtypeStruct((n, d), F32),
        compiler_params=_cparams(1),
        name="moe_combine_ln2",
    )(pos, y_sorted, x1, gate, ln_g, ln_b)


def _route(top_idx, rank, counts, n):
    nk = n * TOP_K
    n_tiles = -(-nk // TM_MOE) + N_EXPERTS
    padded = (counts + TM_MOE - 1) // TM_MOE * TM_MOE
    pad_end = jnp.cumsum(padded)
    pad_start = pad_end - padded
    onehot = top_idx[:, :, None] == jnp.arange(N_EXPERTS, dtype=jnp.int32)
    dest = (rank + jnp.sum(jnp.where(onehot, pad_start, 0), axis=-1)).reshape(nk)
    n_valid = (pad_end[-1] // TM_MOE).astype(jnp.int32)
    tiles = jnp.arange(n_tiles, dtype=jnp.int32)
    tile_start = jnp.minimum(tiles, n_valid - 1) * TM_MOE
    tile_expert = jnp.minimum(jnp.sum(pad_end[None, :] <= tile_start[:, None], axis=1),
                              N_EXPERTS - 1).astype(jnp.int32)
    zrows = jnp.concatenate([jnp.where(padded > 0, pad_end - TM_MOE, -1),
                             jnp.where(tiles >= n_valid, tiles * TM_MOE, -1)]).astype(jnp.int32)
    return dest.astype(jnp.int32), zrows, tile_expert, n_valid.reshape(1), n_tiles


def _rope_tables(seq):
    half = MLA_ROPE_DIM // 2
    inv = ROPE_THETA ** (-jnp.arange(half, dtype=F32) / half)
    ang = jnp.arange(seq).astype(F32)[:, None] * inv[None, :]
    cos, sin = jnp.cos(ang), jnp.sin(ang)
    z = jnp.zeros((seq, LANES - MLA_ROPE_DIM), F32)
    zh = jnp.zeros((seq, half), F32)
    return (jnp.concatenate([cos, cos, z], axis=1),
            jnp.concatenate([-sin, zh, z], axis=1),
            jnp.concatenate([zh, sin, z], axis=1))


def _pad_cols(w, width):
    return jnp.pad(w, ((0, 0), (0, width - w.shape[1])))


def kernel(x, w_in, q_norm_g, kv_norm_g, w_uq, w_ukv, w_o_mla, w_o_dil, w_out, ln1_g, ln1_b,
           w_router, b_router, w1, b1, w2, b2, ln2_g, ln2_b):
    batch, seq, d_model = x.shape
    depth = w_in.shape[0]
    n = batch * seq
    alpha = (2.0 * depth) ** 0.25
    tabs = _rope_tables(seq)
    o_q, o_kv, o_pe = 0, MLA_Q_RANK, MLA_Q_RANK + MLA_KV_RANK
    o_dil = o_pe + MLA_ROPE_DIM
    o_gate = o_dil + 3 * DIL_WIDTH
    xt = x.reshape(n, d_model)
    for l in range(depth):
        wi = w_in[l]
        w_a = jnp.concatenate([wi[:, o_kv:o_pe], _pad_cols(wi[:, o_pe:o_dil], 2 * LANES),
                               wi[:, o_q:o_kv]], axis=1).astype(BF16)
        w_b = wi[:, o_dil:o_gate].astype(BF16)
        w_c = wi[:, o_gate:].astype(BF16)
        wq = w_uq[l].reshape(MLA_Q_RANK, MLA_HEADS, MLA_QK_DIM)
        wq = jnp.pad(wq, ((0, 0), (0, 0), (0, MLA_QK_PAD - MLA_QK_DIM)))
        wq = wq.reshape(MLA_Q_RANK, MLA_HEADS * MLA_QK_PAD).astype(BF16)

        proj_a = _proj(xt, w_a, F32, w_a.shape[1], False, "in_proj_latent")
        qkv_d = _proj(xt, w_b, BF16, DIL_WIDTH, False, "in_proj_dilated")
        gates = _proj(xt, w_c, F32, 1024, True, "in_proj_gates")

        q, k, v = _mla_up(proj_a, q_norm_g[l][None], kv_norm_g[l][None], wq,
                          w_ukv[l].astype(BF16), tabs, seq)
        o_mla = _mla_attention(q, k, v, batch, seq)
        dil = [_dilated_group(qkv_d, g, batch, seq) for g in range(len(DIL_GROUPS))]
        merged = _merge(o_mla, [o for o, _ in dil], [s for _, s in dil], gates,
                        w_o_mla[l].astype(BF16), w_o_dil[l].astype(BF16), d_model)

        w_r = _pad_cols(w_router[l], LANES)
        w_rh = w_r.astype(BF16)
        w_rl = (w_r - w_rh.astype(F32)).astype(BF16)
        b_r = jnp.concatenate([b_router[l], jnp.full((LANES - N_EXPERTS,), NEG_BIG, F32)])[None]
        x1, xpk, top_idx, top_gate, rank, counts = _ln1_router(
            merged, xt, w_out[l].astype(BF16), ln1_g[l][None], ln1_b[l][None], w_rh, w_rl, b_r, alpha)

        dest, zrows, tile_expert, n_valid, n_tiles = _route(
            top_idx[:, :TOP_K], rank[:, :TOP_K], counts[0, :N_EXPERTS], n)
        xs = _moe_scatter(xpk, dest, zrows, n_tiles)
        d_ff = w2.shape[2]
        w2p = w2[l].reshape(N_EXPERTS, d_ff // LANES, 2, PAIR, d_model).transpose(0, 1, 3, 2, 4)
        w2p = w2p.reshape(N_EXPERTS, d_ff, d_model).astype(BF16)
        y_sorted = _moe_ffn(xs, tile_expert, n_valid, w1[l], b1[l][:, None, :], w2p, b2[l][:, None, :])
        xt = _moe_combine(y_sorted, dest, x1, top_gate, ln2_g[l][None], ln2_b[l][None], alpha)
    return xt.reshape(batch, seq, d_model)
```

```python
import functools

import numpy as np
import jax
import jax.numpy as jnp
from jax import lax
from jax.experimental import pallas as pl
from jax.experimental.pallas import tpu as pltpu

F32 = jnp.float32
BF16 = jnp.bfloat16

MLA_HEADS = 16
MLA_Q_RANK = 768
MLA_KV_RANK = 512
MLA_NOPE_DIM = 128
MLA_ROPE_DIM = 64
MLA_V_DIM = 128
MLA_QK_DIM = MLA_NOPE_DIM + MLA_ROPE_DIM
MLA_QK_PAD = 256
ROPE_THETA = 10000.0
DIL_GROUPS = ((128, 1), (512, 4), (2048, 16))
DIL_HEADS_PER_GROUP = 4
DIL_HEADS = DIL_HEADS_PER_GROUP * len(DIL_GROUPS)
DIL_HEAD_DIM = 128
DIL_GROUP_WIDTH = DIL_HEADS_PER_GROUP * DIL_HEAD_DIM
DIL_WIDTH = DIL_HEADS * DIL_HEAD_DIM
DIL_BLOCK = 128
N_EXPERTS = 32
TOP_K = 4
SWIGLU_LIMIT = 7.0
SWIGLU_ALPHA = 1.702
LN_EPS = 1e-5
RMS_EPS = 1e-6
LANES = 128
NEG_BIG = -1e30

VMEM_LIMIT = 56 * 1024 * 1024
TM_PROJ = 1024
TM_PROJ_DIL = 512
TM_UP = 512
TM_MERGE = 512
TM_LN = 512
TM_MOE = 512
TM_SCATTER = 512
TM_COMBINE = 256
W1_COLS = 1024
W2_COLS = 1024
PAIR = LANES // 2


def _cparams(n_axes):
    return pltpu.CompilerParams(dimension_semantics=("arbitrary",) * n_axes,
                                vmem_limit_bytes=VMEM_LIMIT)


def _sigmoid(x):
    return 1.0 / (1.0 + jnp.exp(-x))


def _proj_kernel(x_ref, w_ref, o_ref, xb_ref, *, act):
    @pl.when(pl.program_id(1) == 0)
    def _():
        xb_ref[...] = x_ref[...].astype(BF16)

    acc = jnp.dot(xb_ref[...], w_ref[...], preferred_element_type=F32)
    if act:
        acc = _sigmoid(acc)
    o_ref[...] = acc.astype(o_ref.dtype)


def _proj(x, w, out_dtype, tn, act, name):
    m, k = x.shape
    nc = w.shape[1]
    tm = min(TM_PROJ, m)
    return pl.pallas_call(
        functools.partial(_proj_kernel, act=act),
        grid=(m // tm, nc // tn),
        in_specs=[pl.BlockSpec((tm, k), lambda i, j: (i, 0)),
                  pl.BlockSpec((k, tn), lambda i, j: (0, j))],
        out_specs=pl.BlockSpec((tm, tn), lambda i, j: (i, j)),
        out_shape=jax.ShapeDtypeStruct((m, nc), out_dtype),
        scratch_shapes=[pltpu.VMEM((tm, k), BF16)],
        compiler_params=_cparams(2),
        name=name,
    )(x, w)


def _proj_dil_kernel(x_ref, w_ref, o0_ref, o1_ref, o2_ref, xb_ref, acc_ref):
    j = pl.program_id(1)
    tm = x_ref.shape[0]
    width = w_ref.shape[1]

    @pl.when(j == 0)
    def _():
        xb_ref[...] = x_ref[...].astype(BF16)

    acc = jnp.dot(xb_ref[...], w_ref[...], preferred_element_type=F32)
    n_cb = width // LANES
    for c in range(n_cb):
        acc_ref[c] = acc[:, c * LANES:(c + 1) * LANES]
    for g, o_ref in enumerate((o0_ref, o1_ref, o2_ref)):
        dil = DIL_GROUPS[g][1]

        @pl.when(j == g)
        def _(o_ref=o_ref, dil=dil):
            for r in range(dil):
                for c in range(n_cb):
                    o_ref[:, r * width + c * LANES:r * width + (c + 1) * LANES] = (
                        acc_ref[c, pl.ds(r, tm // dil, stride=dil), :].astype(BF16))


def _proj_dilated(x, w):
    m, k = x.shape
    tm = min(TM_PROJ_DIL, m)
    width = 3 * DIL_GROUP_WIDTH
    n_groups = len(DIL_GROUPS)
    dils = [d for _, d in DIL_GROUPS]
    return pl.pallas_call(
        _proj_dil_kernel,
        grid=(m // tm, n_groups),
        in_specs=[pl.BlockSpec((tm, k), lambda i, j: (i, 0)),
                  pl.BlockSpec((k, width), lambda i, j: (0, j))],
        out_specs=[pl.BlockSpec((tm // d, d * width), lambda i, j: (i, 0)) for d in dils],
        out_shape=[jax.ShapeDtypeStruct((m // d, d * width), BF16) for d in dils],
        scratch_shapes=[pltpu.VMEM((tm, k), BF16), pltpu.VMEM((width // LANES, tm, LANES), F32)],
        compiler_params=_cparams(2),
        name="in_proj_dilated",
    )(x, w)


def _rope_tile(r, cos, sin_lo, sin_hi):
    half = MLA_ROPE_DIM // 2
    return (r * cos + pltpu.roll(r, LANES - half, 1) * sin_lo + pltpu.roll(r, half, 1) * sin_hi)


def _rms(x, g):
    ms = jnp.mean(x * x, axis=-1, keepdims=True)
    return x * lax.rsqrt(ms + RMS_EPS) * g


def _q_up_kernel(cq_ref, g_ref, w_ref, cos_ref, slo_ref, shi_ref, o_ref):
    cqn = _rms(cq_ref[...], g_ref[...]).astype(BF16)
    cos, slo, shi = cos_ref[...], slo_ref[...], shi_ref[...]
    for h in range(MLA_HEADS):
        c0 = h * MLA_QK_PAD
        qh = jnp.dot(cqn, w_ref[:, c0:c0 + MLA_QK_PAD], preferred_element_type=F32)
        o_ref[:, c0:c0 + LANES] = qh[:, :LANES].astype(BF16)
        o_ref[:, c0 + LANES:c0 + MLA_QK_PAD] = _rope_tile(qh[:, LANES:], cos, slo, shi).astype(BF16)


def _kv_up_kernel(ckv_ref, kpe_ref, g_ref, w_ref, cos_ref, slo_ref, shi_ref, k_ref, v_ref):
    ckvn = _rms(ckv_ref[...], g_ref[...]).astype(BF16)
    k_rot = _rope_tile(kpe_ref[...], cos_ref[...], slo_ref[...], shi_ref[...]).astype(BF16)
    for h in range(MLA_HEADS):
        c0 = h * (MLA_NOPE_DIM + MLA_V_DIM)
        kv = jnp.dot(ckvn, w_ref[:, c0:c0 + MLA_NOPE_DIM + MLA_V_DIM], preferred_element_type=F32)
        k0 = h * MLA_QK_PAD
        k_ref[:, k0:k0 + LANES] = kv[:, :MLA_NOPE_DIM].astype(BF16)
        k_ref[:, k0 + LANES:k0 + MLA_QK_PAD] = k_rot
        v_ref[:, h * MLA_V_DIM:(h + 1) * MLA_V_DIM] = kv[:, MLA_NOPE_DIM:].astype(BF16)


def _mla_up(proj_a, q_g, kv_g, w_uq, w_ukv, tabs, seq):
    n = proj_a.shape[0]
    tm = min(TM_UP, seq)
    sb = seq // tm
    tab_specs = [pl.BlockSpec((tm, LANES), lambda i: (i % sb, 0))] * 3
    qw = MLA_HEADS * MLA_QK_PAD
    q = pl.pallas_call(
        _q_up_kernel,
        grid=(n // tm,),
        in_specs=[pl.BlockSpec((tm, MLA_Q_RANK), lambda i: (i, 1)),
                  pl.BlockSpec((1, MLA_Q_RANK), lambda i: (0, 0)),
                  pl.BlockSpec((MLA_Q_RANK, qw), lambda i: (0, 0))] + tab_specs,
        out_specs=pl.BlockSpec((tm, qw), lambda i: (i, 0)),
        out_shape=jax.ShapeDtypeStruct((n, qw), BF16),
        compiler_params=_cparams(1),
        name="mla_q_up",
    )(proj_a, q_g, w_uq, *tabs)
    kvw = MLA_HEADS * (MLA_NOPE_DIM + MLA_V_DIM)
    k, v = pl.pallas_call(
        _kv_up_kernel,
        grid=(n // tm,),
        in_specs=[pl.BlockSpec((tm, MLA_KV_RANK), lambda i: (i, 0)),
                  pl.BlockSpec((tm, LANES), lambda i: (i, MLA_KV_RANK // LANES)),
                  pl.BlockSpec((1, MLA_KV_RANK), lambda i: (0, 0)),
                  pl.BlockSpec((MLA_KV_RANK, kvw), lambda i: (0, 0))] + tab_specs,
        out_specs=[pl.BlockSpec((tm, qw), lambda i: (i, 0)),
                   pl.BlockSpec((tm, MLA_HEADS * MLA_V_DIM), lambda i: (i, 0))],
        out_shape=[jax.ShapeDtypeStruct((n, qw), BF16),
                   jax.ShapeDtypeStruct((n, MLA_HEADS * MLA_V_DIM), BF16)],
        compiler_params=_cparams(1),
        name="mla_kv_up",
    )(proj_a, proj_a, kv_g, w_ukv, *tabs)
    return q, k, v


def _nt_dot(a, b):
    return lax.dot_general(a, b, (((1,), (1,)), ((), ())), preferred_element_type=F32)


def _mla_attn_kernel(q_ref, k_ref, v_ref, o_ref, *, tq, scale):
    seq = q_ref.shape[0]
    row = lax.broadcasted_iota(jnp.int32, (tq, tq), 0)
    col = lax.broadcasted_iota(jnp.int32, (tq, tq), 1)
    causal = col <= row
    for qi in range(seq // tq):
        r0 = qi * tq
        q = q_ref[r0:r0 + tq, :]
        s_d = jnp.where(causal, _nt_dot(q, k_ref[r0:r0 + tq, :]) * scale, -jnp.inf)
        m = jnp.max(s_d, axis=-1, keepdims=True)
        if qi > 0:
            s_p = _nt_dot(q, k_ref[0:r0, :]) * scale
            m = jnp.maximum(m, jnp.max(s_p, axis=-1, keepdims=True))
        p_d = jnp.exp(s_d - m)
        l = jnp.sum(p_d, axis=-1, keepdims=True)
        acc = jnp.dot(p_d.astype(BF16), v_ref[r0:r0 + tq, :], preferred_element_type=F32)
        if qi > 0:
            p_p = jnp.exp(s_p - m)
            l = l + jnp.sum(p_p, axis=-1, keepdims=True)
            acc = acc + jnp.dot(p_p.astype(BF16), v_ref[0:r0, :], preferred_element_type=F32)
        o_ref[r0:r0 + tq, :] = (acc / l).astype(o_ref.dtype)


def _mla_attention(q, k, v, batch, seq):
    n = q.shape[0]
    return pl.pallas_call(
        functools.partial(_mla_attn_kernel, tq=min(256, seq), scale=MLA_QK_DIM ** -0.5),
        grid=(batch, MLA_HEADS),
        in_specs=[pl.BlockSpec((seq, MLA_QK_PAD), lambda b, h: (b, h)),
                  pl.BlockSpec((seq, MLA_QK_PAD), lambda b, h: (b, h)),
                  pl.BlockSpec((seq, MLA_V_DIM), lambda b, h: (b, h))],
        out_specs=pl.BlockSpec((seq, MLA_V_DIM), lambda b, h: (b, h)),
        out_shape=jax.ShapeDtypeStruct((n, MLA_HEADS * MLA_V_DIM), BF16),
        compiler_params=_cparams(2),
        name="mla_attention",
    )(q, k, v)


def _dil_kernel(q_ref, k_ref, v_ref, o_ref, lse_ref, *, dilation, slopes, scale):
    n = q_ref.shape[0]
    blk = DIL_BLOCK
    ri = lax.broadcasted_iota(jnp.int32, (blk, 2 * blk), 0)
    cj = lax.broadcasted_iota(jnp.int32, (blk, 2 * blk), 1) - blk
    steps = ri - cj
    valid2 = (steps >= 0) & (steps <= blk)
    dist2 = (steps * dilation).astype(F32)
    valid1, dist1 = valid2[:, blk:], dist2[:, blk:]
    lane = lax.broadcasted_iota(jnp.int32, (blk, LANES), 1)

    def block(q0, k0, klen, valid, dist):
        lse_tile = jnp.zeros((blk, LANES), F32)
        for j in range(DIL_HEADS_PER_GROUP):
            cs = slice(j * DIL_HEAD_DIM, (j + 1) * DIL_HEAD_DIM)
            q = q_ref[pl.ds(q0, blk), cs]
            kk = k_ref[pl.ds(k0, klen), cs]
            vv = v_ref[pl.ds(k0, klen), cs]
            s = _nt_dot(q, kk) * scale - slopes[j] * dist
            s = jnp.where(valid, s, -jnp.inf)
            m = jnp.max(s, axis=-1, keepdims=True)
            p = jnp.exp(s - m)
            l = jnp.sum(p, axis=-1, keepdims=True)
            o = jnp.dot(p.astype(BF16), vv, preferred_element_type=F32) / l
            o_ref[pl.ds(q0, blk), cs] = o
            lse_tile = jnp.where(lane == j, m + jnp.log(l), lse_tile)
        lse_ref[pl.ds(q0, blk), :] = lse_tile

    block(0, 0, blk, valid1, dist1)

    def body(i, carry):
        q0 = pl.multiple_of(i * blk, blk)
        k0 = pl.multiple_of(i * blk - blk, blk)
        block(q0, k0, 2 * blk, valid2, dist2)
        return carry

    lax.fori_loop(1, n // blk, body, 0)


def _dilated_group(qkv, g, batch, seq):
    window, dilation = DIL_GROUPS[g]
    assert window // dilation == DIL_BLOCK and seq % (dilation * DIL_BLOCK) == 0
    n = seq // dilation
    w = DIL_GROUP_WIDTH
    qkv_v = qkv.reshape(batch, n, dilation * 3 * w)
    slopes = np.float32(2.0) ** (np.float32(-8.0) * np.arange(1, DIL_HEADS + 1, dtype=np.float32)
                                 / np.float32(DIL_HEADS))
    slopes = tuple(float(s) for s in slopes[g * DIL_HEADS_PER_GROUP:(g + 1) * DIL_HEADS_PER_GROUP])

    def in_spec(which):
        return pl.BlockSpec((None, n, w), lambda b, r: (b, 0, r * 3 + which))

    o, lse = pl.pallas_call(
        functools.partial(_dil_kernel, dilation=dilation, slopes=slopes, scale=DIL_HEAD_DIM ** -0.5),
        grid=(batch, dilation),
        in_specs=[in_spec(0), in_spec(1), in_spec(2)],
        out_specs=[pl.BlockSpec((None, n, w), lambda b, r: (b, 0, r)),
                   pl.BlockSpec((None, n, LANES), lambda b, r: (b, 0, r))],
        out_shape=[jax.ShapeDtypeStruct((batch, n, dilation * w), F32),
                   jax.ShapeDtypeStruct((batch, n, dilation * LANES), F32)],
        compiler_params=_cparams(2),
        name=f"dilated_attn_g{g}",
    )(qkv_v, qkv_v, qkv_v)
    return o.reshape(batch * seq, w), lse.reshape(batch * seq, LANES)


def _merge_kernel(omla_ref, o0_ref, o1_ref, o2_ref, l0_ref, l1_ref, l2_ref, sgm_ref, sgd_ref,
                  wm_ref, wd_ref, out_ref, od_ref):
    @pl.when(pl.program_id(1) == 0)
    def _():
        lses = [l0_ref[...], l1_ref[...], l2_ref[...]]
        m = jnp.maximum(jnp.maximum(lses[0], lses[1]), lses[2])
        es = [jnp.exp(l - m) for l in lses]
        den = es[0] + es[1] + es[2]
        for g, o_ref in enumerate((o0_ref, o1_ref, o2_ref)):
            wg = es[g] / den
            for j in range(DIL_HEADS_PER_GROUP):
                c0 = (g * DIL_HEADS_PER_GROUP + j) * DIL_HEAD_DIM
                od_ref[:, c0:c0 + DIL_HEAD_DIM] = (
                    o_ref[:, j * DIL_HEAD_DIM:(j + 1) * DIL_HEAD_DIM] * wg[:, j:j + 1]).astype(BF16)

    a = jnp.dot(omla_ref[...], wm_ref[...], preferred_element_type=F32)
    b = jnp.dot(od_ref[...], wd_ref[...], preferred_element_type=F32)
    out_ref[...] = (sgm_ref[...] * a + sgd_ref[...] * b).astype(out_ref.dtype)


def _merge(o_mla, o_dil, lse_dil, gates, w_o_mla, w_o_dil, d_model):
    n = o_mla.shape[0]
    tm = min(TM_MERGE, n)
    tn = min(1024, d_model)
    nj = d_model // tn
    row = lambda width: pl.BlockSpec((tm, width), lambda i, j: (i, 0))
    return pl.pallas_call(
        _merge_kernel,
        grid=(n // tm, nj),
        in_specs=[row(o_mla.shape[1])] + [row(DIL_GROUP_WIDTH)] * 3 + [row(LANES)] * 3 + [
            pl.BlockSpec((tm, tn), lambda i, j: (i, j)),
            pl.BlockSpec((tm, tn), lambda i, j: (i, j + nj)),
            pl.BlockSpec((w_o_mla.shape[0], tn), lambda i, j: (0, j)),
            pl.BlockSpec((w_o_dil.shape[0], tn), lambda i, j: (0, j))],
        out_specs=pl.BlockSpec((tm, tn), lambda i, j: (i, j)),
        out_shape=jax.ShapeDtypeStruct((n, d_model), BF16),
        scratch_shapes=[pltpu.VMEM((tm, DIL_WIDTH), BF16)],
        compiler_params=_cparams(2),
        name="branch_merge",
    )(o_mla, *o_dil, *lse_dil, gates, gates, w_o_mla, w_o_dil)


def _layer_norm(y, g, b):
    mu = jnp.mean(y, axis=-1, keepdims=True)
    yc = y - mu
    var = jnp.mean(yc * yc, axis=-1, keepdims=True)
    return yc * lax.rsqrt(var + LN_EPS) * g + b


def _pack_bf16_pairs(x):
    half = x.shape[1] // 2
    xb = x.astype(BF16).astype(F32)
    lo = lax.bitcast_convert_type(xb[:, :half], jnp.uint32) >> 16
    hi = lax.bitcast_convert_type(xb[:, half:], jnp.uint32) & jnp.uint32(0xFFFF0000)
    return lo | hi


def _unpack_bf16_pairs(w):
    lo = lax.bitcast_convert_type(w << 16, F32).astype(BF16)
    hi = lax.bitcast_convert_type(w & jnp.uint32(0xFFFF0000), F32).astype(BF16)
    return lo, hi


def _ln1_router_kernel(mg_ref, x_ref, wo_ref, g_ref, b_ref, wrh_ref, wrl_ref, br_ref,
                       x1_ref, xpk_ref, idx_ref, gate_ref, rank_ref, cnt_ref, tri_ref, run_ref, *, alpha):
    tm = x_ref.shape[0]

    @pl.when(pl.program_id(0) == 0)
    def _():
        r = lax.broadcasted_iota(jnp.int32, (tm, tm), 0)
        c = lax.broadcasted_iota(jnp.int32, (tm, tm), 1)
        tri_ref[...] = jnp.where(c < r, 1.0, 0.0).astype(BF16)
        run_ref[...] = jnp.zeros(run_ref.shape, F32)

    out = jnp.dot(mg_ref[...], wo_ref[...], preferred_element_type=F32)
    x1 = _layer_norm(alpha * x_ref[...] + out, g_ref[...], b_ref[...])
    x1_ref[...] = x1
    xpk_ref[...] = _pack_bf16_pairs(x1)
    xh = x1.astype(BF16)
    xl = (x1 - xh.astype(F32)).astype(BF16)
    logits = (jnp.dot(xh, wrh_ref[...], preferred_element_type=F32)
              + jnp.dot(xl, wrh_ref[...], preferred_element_type=F32)
              + jnp.dot(xh, wrl_ref[...], preferred_element_type=F32)) + br_ref[...]
    lane = lax.broadcasted_iota(jnp.int32, logits.shape, 1).astype(F32)
    vals, idxs = [], []
    for _ in range(TOP_K):
        m = jnp.max(logits, axis=-1, keepdims=True)
        idx = jnp.min(jnp.where(logits == m, lane, float(LANES)), axis=-1, keepdims=True)
        vals.append(m)
        idxs.append(idx)
        logits = jnp.where(lane == idx, -jnp.inf, logits)
    es = [jnp.exp(v - vals[0]) for v in vals]
    den = es[0] + es[1] + es[2] + es[3]
    onehot = jnp.zeros(lane.shape, F32)
    for k in range(TOP_K):
        onehot = onehot + jnp.where(lane == idxs[k], 1.0, 0.0)
    before = run_ref[...] + jnp.dot(tri_ref[...], onehot.astype(BF16), preferred_element_type=F32)
    run_ref[...] = run_ref[...] + jnp.sum(onehot, axis=0, keepdims=True)
    cnt_ref[...] = run_ref[...].astype(jnp.int32)
    idx_tile = jnp.zeros(lane.shape, F32)
    gate_tile = jnp.zeros(lane.shape, F32)
    rank_tile = jnp.zeros(lane.shape, F32)
    for k in range(TOP_K):
        rank = jnp.sum(jnp.where(lane == idxs[k], before, 0.0), axis=-1, keepdims=True)
        idx_tile = jnp.where(lane == float(k), idxs[k], idx_tile)
        gate_tile = jnp.where(lane == float(k), es[k] / den, gate_tile)
        rank_tile = jnp.where(lane == float(k), rank, rank_tile)
    idx_ref[...] = idx_tile.astype(jnp.int32)
    gate_ref[...] = gate_tile
    rank_ref[...] = rank_tile.astype(jnp.int32)


def _ln1_router(merged, x, w_out, ln_g, ln_b, w_rh, w_rl, b_r, alpha):
    n, d = x.shape
    tm = min(TM_LN, n)
    full = lambda a: pl.BlockSpec(a.shape, lambda i: (0,) * a.ndim)
    row = lambda width: pl.BlockSpec((tm, width), lambda i: (i, 0))
    return pl.pallas_call(
        functools.partial(_ln1_router_kernel, alpha=alpha),
        grid=(n // tm,),
        in_specs=[row(d), row(d), full(w_out), full(ln_g), full(ln_b), full(w_rh), full(w_rl), full(b_r)],
        out_specs=[row(d), row(d // 2), row(LANES), row(LANES), row(LANES),
                   pl.BlockSpec((1, LANES), lambda i: (0, 0))],
        out_shape=[jax.ShapeDtypeStruct((n, d), F32),
                   jax.ShapeDtypeStruct((n, d // 2), jnp.uint32),
                   jax.ShapeDtypeStruct((n, LANES), jnp.int32),
                   jax.ShapeDtypeStruct((n, LANES), F32),
                   jax.ShapeDtypeStruct((n, LANES), jnp.int32),
                   jax.ShapeDtypeStruct((1, LANES), jnp.int32)],
        scratch_shapes=[pltpu.VMEM((tm, tm), BF16), pltpu.VMEM((1, LANES), F32)],
        compiler_params=_cparams(1),
        name="out_proj_ln1_router",
    )(merged, x, w_out, ln_g, ln_b, w_rh, w_rl, b_r)


def _scatter_kernel(dest_ref, zrow_ref, xpk_ref, xs_hbm, zbuf, sem, zsem):
    t = pl.program_id(0)
    rows = TM_SCATTER

    @pl.when(t == 0)
    def _():
        zbuf[...] = jnp.zeros(zbuf.shape, zbuf.dtype)

        def zero_copy(i):
            row = pl.multiple_of(jnp.maximum(zrow_ref[i], 0), TM_MOE)
            return pltpu.make_async_copy(zbuf, xs_hbm.at[pl.ds(row, TM_MOE)], zsem)

        def zstart(i, carry):
            @pl.when(zrow_ref[i] >= 0)
            def _():
                zero_copy(i).start()
            return carry

        def zwait(i, carry):
            @pl.when(zrow_ref[i] >= 0)
            def _():
                zero_copy(i).wait()
            return carry

        lax.fori_loop(0, zrow_ref.shape[0], zstart, 0)
        lax.fori_loop(0, zrow_ref.shape[0], zwait, 0)

    def issue(r, carry):
        tok = t * rows + r
        for k in range(TOP_K):
            dst = dest_ref[tok * TOP_K + k]
            pltpu.make_async_copy(xpk_ref.at[pl.ds(r, 1)], xs_hbm.at[pl.ds(dst, 1)], sem).start()
        return carry

    lax.fori_loop(0, rows, issue, 0)
    pltpu.make_async_copy(xs_hbm.at[pl.ds(0, TOP_K * rows)], xs_hbm.at[pl.ds(0, TOP_K * rows)], sem).wait()


def _moe_scatter(xpk, dest, zrows, n_tiles):
    n, w = xpk.shape
    return pl.pallas_call(
        _scatter_kernel,
        grid_spec=pltpu.PrefetchScalarGridSpec(
            num_scalar_prefetch=2,
            grid=(n // TM_SCATTER,),
            in_specs=[pl.BlockSpec((TM_SCATTER, w), lambda t, dest, zrows: (t, 0))],
            out_specs=pl.BlockSpec(memory_space=pl.ANY),
            scratch_shapes=[pltpu.VMEM((TM_MOE, w), xpk.dtype), pltpu.SemaphoreType.DMA,
                            pltpu.SemaphoreType.DMA]),
        out_shape=jax.ShapeDtypeStruct((n_tiles * TM_MOE, w), xpk.dtype),
        compiler_params=_cparams(1),
        name="moe_scatter",
    )(dest, zrows, xpk)


def _new_expert(te_ref, t):
    return (t == 0) | (te_ref[t] != te_ref[jnp.maximum(t - 1, 0)])


def _ffn_a_kernel(te_ref, nt_ref, xs_ref, w_ref, b_ref, a_ref, wb_ref):
    t = pl.program_id(1)

    @pl.when(_new_expert(te_ref, t))
    def _():
        wb_ref[...] = w_ref[...].astype(BF16)

    @pl.when(t < nt_ref[0])
    def _():
        lo, hi = _unpack_bf16_pairs(xs_ref[...])
        half = lo.shape[1]
        h = (jnp.dot(lo, wb_ref[0:half, :], preferred_element_type=F32)
             + jnp.dot(hi, wb_ref[half:, :], preferred_element_type=F32) + b_ref[...])
        even = lax.broadcasted_iota(jnp.int32, (h.shape[0], LANES), 1) % 2 == 0
        for g in range(h.shape[1] // (2 * LANES)):
            h0 = h[:, 2 * g * LANES:(2 * g + 1) * LANES]
            h1 = h[:, (2 * g + 1) * LANES:(2 * g + 2) * LANES]
            glu = jnp.where(even, h0, pltpu.roll(h1, 1, 1))
            lin = jnp.where(even, pltpu.roll(h0, LANES - 1, 1), h1)
            x_glu = jnp.minimum(glu, SWIGLU_LIMIT)
            x_lin = jnp.clip(lin, -SWIGLU_LIMIT, SWIGLU_LIMIT)
            a_ref[:, g * LANES:(g + 1) * LANES] = (
                x_glu * _sigmoid(SWIGLU_ALPHA * x_glu) * (x_lin + 1.0)).astype(a_ref.dtype)

    @pl.when(t >= nt_ref[0])
    def _():
        a_ref[...] = jnp.zeros(a_ref.shape, a_ref.dtype)


def _ffn_b_kernel(te_ref, nt_ref, a_ref, w_ref, b_ref, o_ref, perm_ref, wb_ref):
    t = pl.program_id(1)

    @pl.when(_new_expert(te_ref, t))
    def _():
        for c in range(w_ref.shape[1] // LANES):
            cs = slice(c * LANES, (c + 1) * LANES)
            for g in range(w_ref.shape[0] // LANES):
                for p in range(2):
                    src = g * LANES + p * PAIR
                    perm_ref[c, pl.ds(g * LANES + p, PAIR, stride=2), :] = w_ref[src:src + PAIR, cs]
            wb_ref[:, cs] = perm_ref[c].astype(BF16)

    @pl.when(t < nt_ref[0])
    def _():
        o_ref[...] = jnp.dot(a_ref[...], wb_ref[...], preferred_element_type=F32) + b_ref[...]

    @pl.when(t >= nt_ref[0])
    def _():
        o_ref[...] = jnp.zeros(o_ref.shape, o_ref.dtype)


def _moe_ffn(xs, tile_expert, n_valid, w1, b1, w2, b2):
    rows = xs.shape[0]
    n_tiles = rows // TM_MOE
    _, d, cols = w1.shape
    d_ff = cols // 2
    wc = min(W1_COLS, cols)
    tile = lambda t, nt: jnp.minimum(t, nt[0] - 1)
    a = pl.pallas_call(
        _ffn_a_kernel,
        grid_spec=pltpu.PrefetchScalarGridSpec(
            num_scalar_prefetch=2,
            grid=(cols // wc, n_tiles),
            in_specs=[pl.BlockSpec((TM_MOE, d // 2), lambda f, t, te, nt: (tile(t, nt), 0)),
                      pl.BlockSpec((None, d, wc), lambda f, t, te, nt: (te[t], 0, f)),
                      pl.BlockSpec((None, 1, wc), lambda f, t, te, nt: (te[t], 0, f))],
            out_specs=pl.BlockSpec((TM_MOE, wc // 2), lambda f, t, te, nt: (t, f)),
            scratch_shapes=[pltpu.VMEM((d, wc), BF16)]),
        out_shape=jax.ShapeDtypeStruct((rows, d_ff), BF16),
        compiler_params=_cparams(2),
        name="moe_ffn_up",
    )(tile_expert, n_valid, xs, w1, b1)
    nc = min(W2_COLS, d)
    return pl.pallas_call(
        _ffn_b_kernel,
        grid_spec=pltpu.PrefetchScalarGridSpec(
            num_scalar_prefetch=2,
            grid=(d // nc, n_tiles),
            in_specs=[pl.BlockSpec((TM_MOE, d_ff), lambda c, t, te, nt: (tile(t, nt), 0)),
                      pl.BlockSpec((None, d_ff, nc), lambda c, t, te, nt: (te[t], 0, c)),
                      pl.BlockSpec((None, 1, nc), lambda c, t, te, nt: (te[t], 0, c))],
            out_specs=pl.BlockSpec((TM_MOE, nc), lambda c, t, te, nt: (t, c)),
            scratch_shapes=[pltpu.VMEM((nc // LANES, d_ff, LANES), F32), pltpu.VMEM((d_ff, nc), BF16)]),
        out_shape=jax.ShapeDtypeStruct((rows, d), F32),
        compiler_params=_cparams(2),
        name="moe_ffn_down",
    )(tile_expert, n_valid, a, w2, b2)


def _combine_kernel(pos_ref, y_hbm, x_ref, gate_ref, g_ref, b_ref, o_ref, buf, sem, *, alpha):
    t = pl.program_id(0)
    rows = x_ref.shape[0]

    def issue(r, carry):
        for k in range(TOP_K):
            p = pos_ref[(t * rows + r) * TOP_K + k]
            pltpu.make_async_copy(y_hbm.at[pl.ds(p, 1)], buf.at[pl.ds(k * rows + r, 1)], sem).start()
        return carry

    lax.fori_loop(0, rows, issue, 0)
    pltpu.make_async_copy(y_hbm.at[pl.ds(0, TOP_K * rows)], buf, sem).wait()
    gate = gate_ref[...]
    ffn = buf[0:rows, :] * gate[:, 0:1]
    for k in range(1, TOP_K):
        ffn = ffn + buf[k * rows:(k + 1) * rows, :] * gate[:, k:k + 1]
    o_ref[...] = _layer_norm(alpha * x_ref[...] + ffn, g_ref[...], b_ref[...])


def _moe_combine(y_sorted, pos, x1, gate, ln_g, ln_b, alpha):
    n, d = x1.shape
    tm = min(TM_COMBINE, n)
    return pl.pallas_call(
        functools.partial(_combine_kernel, alpha=alpha),
        grid_spec=pltpu.PrefetchScalarGridSpec(
            num_scalar_prefetch=1,
            grid=(n // tm,),
            in_specs=[pl.BlockSpec(memory_space=pl.ANY),
                      pl.BlockSpec((tm, d), lambda t, pos: (t, 0)),
                      pl.BlockSpec((tm, LANES), lambda t, pos: (t, 0)),
                      pl.BlockSpec((1, d), lambda t, pos: (0, 0)),
                      pl.BlockSpec((1, d), lambda t, pos: (0, 0))],
            out_specs=pl.BlockSpec((tm, d), lambda t, pos: (t, 0)),
            scratch_shapes=[pltpu.VMEM((TOP_K * tm, d), F32), pltpu.SemaphoreType.DMA]),
        out_shape=jax.ShapeDtypeStruct((n, d), F32),
        compiler_params=_cparams(1),
        name="moe_combine_ln2",
    )(pos, y_sorted, x1, gate, ln_g, ln_b)


def _route(top_idx, rank, counts, n):
    nk = n * TOP_K
    n_tiles = -(-nk // TM_MOE) + N_EXPERTS
    padded = (counts + TM_MOE - 1) // TM_MOE * TM_MOE
    pad_end = jnp.cumsum(padded)
    pad_start = pad_end - padded
    onehot = top_idx[:, :, None] == jnp.arange(N_EXPERTS, dtype=jnp.int32)
    dest = (rank + jnp.sum(jnp.where(onehot, pad_start, 0), axis=-1)).reshape(nk)
    n_valid = (pad_end[-1] // TM_MOE).astype(jnp.int32)
    tiles = jnp.arange(n_tiles, dtype=jnp.int32)
    tile_start = jnp.minimum(tiles, n_valid - 1) * TM_MOE
    tile_expert = jnp.minimum(jnp.sum(pad_end[None, :] <= tile_start[:, None], axis=1),
                              N_EXPERTS - 1).astype(jnp.int32)
    zrows = jnp.concatenate([jnp.where(padded > 0, pad_end - TM_MOE, -1),
                             jnp.where(tiles >= n_valid, tiles * TM_MOE, -1)]).astype(jnp.int32)
    return dest.astype(jnp.int32), zrows, tile_expert, n_valid.reshape(1), n_tiles


def _rope_tables(seq):
    half = MLA_ROPE_DIM // 2
    inv = ROPE_THETA ** (-jnp.arange(half, dtype=F32) / half)
    ang = jnp.arange(seq).astype(F32)[:, None] * inv[None, :]
    cos, sin = jnp.cos(ang), jnp.sin(ang)
    z = jnp.zeros((seq, LANES - MLA_ROPE_DIM), F32)
    zh = jnp.zeros((seq, half), F32)
    return (jnp.concatenate([cos, cos, z], axis=1),
            jnp.concatenate([-sin, zh, z], axis=1),
            jnp.concatenate([zh, sin, z], axis=1))


def _pad_cols(w, width):
    return jnp.pad(w, ((0, 0), (0, width - w.shape[1])))


def kernel(x, w_in, q_norm_g, kv_norm_g, w_uq, w_ukv, w_o_mla, w_o_dil, w_out, ln1_g, ln1_b,
           w_router, b_router, w1, b1, w2, b2, ln2_g, ln2_b):
    batch, seq, d_model = x.shape
    depth = w_in.shape[0]
    n = batch * seq
    alpha = (2.0 * depth) ** 0.25
    tabs = _rope_tables(seq)
    o_q, o_kv, o_pe = 0, MLA_Q_RANK, MLA_Q_RANK + MLA_KV_RANK
    o_dil = o_pe + MLA_ROPE_DIM
    o_gate = o_dil + 3 * DIL_WIDTH
    xt = x.reshape(n, d_model)
    for l in range(depth):
        wi = w_in[l]
        w_a = jnp.concatenate([wi[:, o_kv:o_pe], _pad_cols(wi[:, o_pe:o_dil], 2 * LANES),
                               wi[:, o_q:o_kv]], axis=1).astype(BF16)
        w_b = jnp.concatenate(
            [wi[:, o_dil + which * DIL_WIDTH + g * DIL_GROUP_WIDTH:][:, :DIL_GROUP_WIDTH]
             for g in range(len(DIL_GROUPS)) for which in range(3)], axis=1).astype(BF16)
        w_c = wi[:, o_gate:].astype(BF16)
        wq = w_uq[l].reshape(MLA_Q_RANK, MLA_HEADS, MLA_QK_DIM)
        wq = jnp.pad(wq, ((0, 0), (0, 0), (0, MLA_QK_PAD - MLA_QK_DIM)))
        wq = wq.reshape(MLA_Q_RANK, MLA_HEADS * MLA_QK_PAD).astype(BF16)

        proj_a = _proj(xt, w_a, F32, w_a.shape[1], False, "in_proj_latent")
        qkv_d = _proj_dilated(xt, w_b)
        gates = _proj(xt, w_c, F32, 1024, True, "in_proj_gates")

        q, k, v = _mla_up(proj_a, q_norm_g[l][None], kv_norm_g[l][None], wq,
                          w_ukv[l].astype(BF16), tabs, seq)
        o_mla = _mla_attention(q, k, v, batch, seq)
        dil = [_dilated_group(qkv_d[g], g, batch, seq) for g in range(len(DIL_GROUPS))]
        merged = _merge(o_mla, [o for o, _ in dil], [s for _, s in dil], gates,
                        w_o_mla[l].astype(BF16), w_o_dil[l].astype(BF16), d_model)

        w_r = _pad_cols(w_router[l], LANES)
        w_rh = w_r.astype(BF16)
        w_rl = (w_r - w_rh.astype(F32)).astype(BF16)
        b_r = jnp.concatenate([b_router[l], jnp.full((LANES - N_EXPERTS,), NEG_BIG, F32)])[None]
        x1, xpk, top_idx, top_gate, rank, counts = _ln1_router(
            merged, xt, w_out[l].astype(BF16), ln1_g[l][None], ln1_b[l][None], w_rh, w_rl, b_r, alpha)

        dest, zrows, tile_expert, n_valid, n_tiles = _route(
            top_idx[:, :TOP_K], rank[:, :TOP_K], counts[0, :N_EXPERTS], n)
        xs = _moe_scatter(xpk, dest, zrows, n_tiles)
        y_sorted = _moe_ffn(xs, tile_expert, n_valid, w1[l], b1[l][:, None, :], w2[l], b2[l][:, None, :])
        xt = _moe_combine(y_sorted, dest, x1, top_gate, ln2_g[l][None], ln2_b[l][None], alpha)
    return xt.reshape(batch, seq, d_model)
```

```python
import functools

import numpy as np
import jax
import jax.numpy as jnp
from jax import lax
from jax.experimental import pallas as pl
from jax.experimental.pallas import tpu as pltpu

F32 = jnp.float32
BF16 = jnp.bfloat16

MLA_HEADS = 16
MLA_Q_RANK = 768
MLA_KV_RANK = 512
MLA_NOPE_DIM = 128
MLA_ROPE_DIM = 64
MLA_V_DIM = 128
MLA_QK_DIM = MLA_NOPE_DIM + MLA_ROPE_DIM
MLA_QK_PAD = 256
MLA_Q_SCALE = MLA_QK_DIM ** -0.5 * 1.4426950408889634
ROPE_THETA = 10000.0
DIL_GROUPS = ((128, 1), (512, 4), (2048, 16))
DIL_HEADS_PER_GROUP = 4
DIL_HEADS = DIL_HEADS_PER_GROUP * len(DIL_GROUPS)
DIL_HEAD_DIM = 128
DIL_GROUP_WIDTH = DIL_HEADS_PER_GROUP * DIL_HEAD_DIM
DIL_WIDTH = DIL_HEADS * DIL_HEAD_DIM
DIL_BLOCK = 128
N_EXPERTS = 32
TOP_K = 4
SWIGLU_LIMIT = 7.0
SWIGLU_ALPHA = 1.702
LN_EPS = 1e-5
RMS_EPS = 1e-6
LANES = 128
SUBLANES = 8
NEG_BIG = -1e30

VMEM_LIMIT = 56 * 1024 * 1024
TM_PROJ = 1024
TM_PROJ_DIL = 512
TM_UP = 512
TM_MERGE = 512
TM_LN = 512
TM_MOE = 1024
MOE_SUB = 256
TM_SCATTER = 512
TM_COMBINE = 256
W1_COLS = 1024
W2_COLS = 1024
PAIR = LANES // 2


def _cparams(n_axes):
    return pltpu.CompilerParams(dimension_semantics=("arbitrary",) * n_axes,
                                vmem_limit_bytes=VMEM_LIMIT)


def _sigmoid(x):
    return 1.0 / (1.0 + jnp.exp(-x))


def _proj_kernel(x_ref, w_ref, o_ref, xb_ref, *, act):
    @pl.when(pl.program_id(1) == 0)
    def _():
        xb_ref[...] = x_ref[...].astype(BF16)

    acc = jnp.dot(xb_ref[...], w_ref[...], preferred_element_type=F32)
    if act:
        acc = _sigmoid(acc)
    o_ref[...] = acc.astype(o_ref.dtype)


def _proj(x, w, out_dtype, tn, act, name):
    m, k = x.shape
    nc = w.shape[1]
    tm = min(TM_PROJ, m)
    return pl.pallas_call(
        functools.partial(_proj_kernel, act=act),
        grid=(m // tm, nc // tn),
        in_specs=[pl.BlockSpec((tm, k), lambda i, j: (i, 0)),
                  pl.BlockSpec((k, tn), lambda i, j: (0, j))],
        out_specs=pl.BlockSpec((tm, tn), lambda i, j: (i, j)),
        out_shape=jax.ShapeDtypeStruct((m, nc), out_dtype),
        scratch_shapes=[pltpu.VMEM((tm, k), BF16)],
        compiler_params=_cparams(2),
        name=name,
    )(x, w)


def _proj_dil_kernel(x_ref, w_ref, o0_ref, o1_ref, o2_ref, xb_ref, acc_ref):
    j = pl.program_id(1)
    tm = x_ref.shape[0]
    width = w_ref.shape[1]

    @pl.when(j == 0)
    def _():
        xb_ref[...] = x_ref[...].astype(BF16)

    acc = jnp.dot(xb_ref[...], w_ref[...], preferred_element_type=F32)
    n_cb = width // LANES
    for c in range(n_cb):
        acc_ref[c] = acc[:, c * LANES:(c + 1) * LANES]
    for g, o_ref in enumerate((o0_ref, o1_ref, o2_ref)):
        dil = DIL_GROUPS[g][1]

        @pl.when(j == g)
        def _(o_ref=o_ref, dil=dil):
            for r in range(dil):
                for c in range(n_cb):
                    o_ref[:, r * width + c * LANES:r * width + (c + 1) * LANES] = (
                        acc_ref[c, pl.ds(r, tm // dil, stride=dil), :].astype(BF16))


def _proj_dilated(x, w):
    m, k = x.shape
    tm = min(TM_PROJ_DIL, m)
    width = 3 * DIL_GROUP_WIDTH
    n_groups = len(DIL_GROUPS)
    dils = [d for _, d in DIL_GROUPS]
    return pl.pallas_call(
        _proj_dil_kernel,
        grid=(m // tm, n_groups),
        in_specs=[pl.BlockSpec((tm, k), lambda i, j: (i, 0)),
                  pl.BlockSpec((k, width), lambda i, j: (0, j))],
        out_specs=[pl.BlockSpec((tm // d, d * width), lambda i, j: (i, 0)) for d in dils],
        out_shape=[jax.ShapeDtypeStruct((m // d, d * width), BF16) for d in dils],
        scratch_shapes=[pltpu.VMEM((tm, k), BF16), pltpu.VMEM((width // LANES, tm, LANES), F32)],
        compiler_params=_cparams(2),
        name="in_proj_dilated",
    )(x, w)


def _rope_tile(r, cos, sin_lo, sin_hi):
    half = MLA_ROPE_DIM // 2
    return (r * cos + pltpu.roll(r, LANES - half, 1) * sin_lo + pltpu.roll(r, half, 1) * sin_hi)


def _rms(x, g):
    ms = jnp.mean(x * x, axis=-1, keepdims=True)
    return x * lax.rsqrt(ms + RMS_EPS) * g


def _q_up_kernel(cq_ref, g_ref, w_ref, cos_ref, slo_ref, shi_ref, o_ref):
    cqn = _rms(cq_ref[...], g_ref[...]).astype(BF16)
    cos, slo, shi = cos_ref[...], slo_ref[...], shi_ref[...]
    for h in range(MLA_HEADS):
        c0 = h * MLA_QK_PAD
        qh = jnp.dot(cqn, w_ref[:, c0:c0 + MLA_QK_PAD], preferred_element_type=F32) * MLA_Q_SCALE
        o_ref[:, c0:c0 + LANES] = qh[:, :LANES].astype(BF16)
        o_ref[:, c0 + LANES:c0 + MLA_QK_PAD] = _rope_tile(qh[:, LANES:], cos, slo, shi).astype(BF16)


def _kv_up_kernel(ckv_ref, kpe_ref, g_ref, w_ref, cos_ref, slo_ref, shi_ref, k_ref, v_ref):
    ckvn = _rms(ckv_ref[...], g_ref[...]).astype(BF16)
    k_rot = _rope_tile(kpe_ref[...], cos_ref[...], slo_ref[...], shi_ref[...]).astype(BF16)
    lane = lax.broadcasted_iota(jnp.int32, kpe_ref.shape, 1)
    ones_col = jnp.where(lane == 0, 1.0, 0.0).astype(BF16)
    for h in range(MLA_HEADS):
        c0 = h * (MLA_NOPE_DIM + MLA_V_DIM)
        kv = jnp.dot(ckvn, w_ref[:, c0:c0 + MLA_NOPE_DIM + MLA_V_DIM], preferred_element_type=F32)
        k0 = h * MLA_QK_PAD
        k_ref[:, k0:k0 + LANES] = kv[:, :MLA_NOPE_DIM].astype(BF16)
        k_ref[:, k0 + LANES:k0 + MLA_QK_PAD] = k_rot
        v_ref[:, k0:k0 + MLA_V_DIM] = kv[:, MLA_NOPE_DIM:].astype(BF16)
        v_ref[:, k0 + MLA_V_DIM:k0 + MLA_QK_PAD] = ones_col


def _mla_up(proj_a, q_g, kv_g, w_uq, w_ukv, tabs, seq):
    n = proj_a.shape[0]
    tm = min(TM_UP, seq)
    sb = seq // tm
    tab_specs = [pl.BlockSpec((tm, LANES), lambda i: (i % sb, 0))] * 3
    qw = MLA_HEADS * MLA_QK_PAD
    q = pl.pallas_call(
        _q_up_kernel,
        grid=(n // tm,),
        in_specs=[pl.BlockSpec((tm, MLA_Q_RANK), lambda i: (i, 1)),
                  pl.BlockSpec((1, MLA_Q_RANK), lambda i: (0, 0)),
                  pl.BlockSpec((MLA_Q_RANK, qw), lambda i: (0, 0))] + tab_specs,
        out_specs=pl.BlockSpec((tm, qw), lambda i: (i, 0)),
        out_shape=jax.ShapeDtypeStruct((n, qw), BF16),
        compiler_params=_cparams(1),
        name="mla_q_up",
    )(proj_a, q_g, w_uq, *tabs)
    kvw = MLA_HEADS * (MLA_NOPE_DIM + MLA_V_DIM)
    k, v = pl.pallas_call(
        _kv_up_kernel,
        grid=(n // tm,),
        in_specs=[pl.BlockSpec((tm, MLA_KV_RANK), lambda i: (i, 0)),
                  pl.BlockSpec((tm, LANES), lambda i: (i, MLA_KV_RANK // LANES)),
                  pl.BlockSpec((1, MLA_KV_RANK), lambda i: (0, 0)),
                  pl.BlockSpec((MLA_KV_RANK, kvw), lambda i: (0, 0))] + tab_specs,
        out_specs=[pl.BlockSpec((tm, qw), lambda i: (i, 0)),
                   pl.BlockSpec((tm, qw), lambda i: (i, 0))],
        out_shape=[jax.ShapeDtypeStruct((n, qw), BF16),
                   jax.ShapeDtypeStruct((n, qw), BF16)],
        compiler_params=_cparams(1),
        name="mla_kv_up",
    )(proj_a, proj_a, kv_g, w_ukv, *tabs)
    return q, k, v


def _nt_dot(a, b):
    return lax.dot_general(a, b, (((1,), (1,)), ((), ())), preferred_element_type=F32)


def _mla_attn_kernel(q_ref, k_ref, v_ref, o_ref, *, tq):
    seq = q_ref.shape[0]
    row = lax.broadcasted_iota(jnp.int32, (tq, tq), 0)
    col = lax.broadcasted_iota(jnp.int32, (tq, tq), 1)
    causal = col <= row
    for qi in range(seq // tq):
        r0 = qi * tq
        q = q_ref[r0:r0 + tq, :]
        s_d = jnp.where(causal, _nt_dot(q, k_ref[r0:r0 + tq, :]), -jnp.inf)
        m = jnp.max(s_d, axis=-1, keepdims=True)
        if qi > 0:
            s_p = _nt_dot(q, k_ref[0:r0, :])
            m = jnp.maximum(m, jnp.max(s_p, axis=-1, keepdims=True))
        acc = jnp.dot(jnp.exp2(s_d - m).astype(BF16), v_ref[r0:r0 + tq, :], preferred_element_type=F32)
        if qi > 0:
            acc = acc + jnp.dot(jnp.exp2(s_p - m).astype(BF16), v_ref[0:r0, :],
                                preferred_element_type=F32)
        o_ref[r0:r0 + tq, :] = (acc[:, :MLA_V_DIM] / acc[:, MLA_V_DIM:MLA_V_DIM + 1]).astype(o_ref.dtype)


def _mla_attention(q, k, v, batch, seq):
    n = q.shape[0]
    return pl.pallas_call(
        functools.partial(_mla_attn_kernel, tq=min(512, seq)),
        grid=(batch, MLA_HEADS),
        in_specs=[pl.BlockSpec((seq, MLA_QK_PAD), lambda b, h: (b, h))] * 3,
        out_specs=pl.BlockSpec((seq, MLA_V_DIM), lambda b, h: (b, h)),
        out_shape=jax.ShapeDtypeStruct((n, MLA_HEADS * MLA_V_DIM), BF16),
        compiler_params=_cparams(2),
        name="mla_attention",
    )(q, k, v)


def _dil_kernel(q_ref, k_ref, v_ref, o_ref, lse_ref, *, dilation, slopes, scale):
    n = q_ref.shape[0]
    blk = DIL_BLOCK
    ri = lax.broadcasted_iota(jnp.int32, (blk, 2 * blk), 0)
    cj = lax.broadcasted_iota(jnp.int32, (blk, 2 * blk), 1) - blk
    steps = ri - cj
    valid2 = (steps >= 0) & (steps <= blk)
    dist2 = (steps * dilation).astype(F32)
    valid1, dist1 = valid2[:, blk:], dist2[:, blk:]
    lane = lax.broadcasted_iota(jnp.int32, (blk, LANES), 1)

    def block(q0, k0, klen, valid, dist):
        lse_tile = jnp.zeros((blk, LANES), F32)
        for j in range(DIL_HEADS_PER_GROUP):
            cs = slice(j * DIL_HEAD_DIM, (j + 1) * DIL_HEAD_DIM)
            q = q_ref[pl.ds(q0, blk), cs]
            kk = k_ref[pl.ds(k0, klen), cs]
            vv = v_ref[pl.ds(k0, klen), cs]
            s = _nt_dot(q, kk) * scale - slopes[j] * dist
            s = jnp.where(valid, s, -jnp.inf)
            m = jnp.max(s, axis=-1, keepdims=True)
            p = jnp.exp(s - m)
            l = jnp.sum(p, axis=-1, keepdims=True)
            o = jnp.dot(p.astype(BF16), vv, preferred_element_type=F32) / l
            o_ref[pl.ds(q0, blk), cs] = o
            lse_tile = jnp.where(lane == j, m + jnp.log(l), lse_tile)
        lse_ref[pl.ds(q0, blk), :] = lse_tile

    block(0, 0, blk, valid1, dist1)

    def body(i, carry):
        q0 = pl.multiple_of(i * blk, blk)
        k0 = pl.multiple_of(i * blk - blk, blk)
        block(q0, k0, 2 * blk, valid2, dist2)
        return carry

    lax.fori_loop(1, n // blk, body, 0)


def _dilated_group(qkv, g, batch, seq):
    window, dilation = DIL_GROUPS[g]
    assert window // dilation == DIL_BLOCK and seq % (dilation * DIL_BLOCK) == 0
    n = seq // dilation
    w = DIL_GROUP_WIDTH
    qkv_v = qkv.reshape(batch, n, dilation * 3 * w)
    slopes = np.float32(2.0) ** (np.float32(-8.0) * np.arange(1, DIL_HEADS + 1, dtype=np.float32)
                                 / np.float32(DIL_HEADS))
    slopes = tuple(float(s) for s in slopes[g * DIL_HEADS_PER_GROUP:(g + 1) * DIL_HEADS_PER_GROUP])

    def in_spec(which):
        return pl.BlockSpec((None, n, w), lambda b, r: (b, 0, r * 3 + which))

    o, lse = pl.pallas_call(
        functools.partial(_dil_kernel, dilation=dilation, slopes=slopes, scale=DIL_HEAD_DIM ** -0.5),
        grid=(batch, dilation),
        in_specs=[in_spec(0), in_spec(1), in_spec(2)],
        out_specs=[pl.BlockSpec((None, n, w), lambda b, r: (b, 0, r)),
                   pl.BlockSpec((None, n, LANES), lambda b, r: (b, 0, r))],
        out_shape=[jax.ShapeDtypeStruct((batch, n, dilation * w), F32),
                   jax.ShapeDtypeStruct((batch, n, dilation * LANES), F32)],
        compiler_params=_cparams(2),
        name=f"dilated_attn_g{g}",
    )(qkv_v, qkv_v, qkv_v)
    return o.reshape(batch * seq, w), lse.reshape(batch * seq, LANES)


def _merge_kernel(omla_ref, o0_ref, o1_ref, o2_ref, l0_ref, l1_ref, l2_ref, sgm_ref, sgd_ref,
                  wm_ref, wd_ref, out_ref, od_ref):
    @pl.when(pl.program_id(1) == 0)
    def _():
        lses = [l0_ref[...], l1_ref[...], l2_ref[...]]
        m = jnp.maximum(jnp.maximum(lses[0], lses[1]), lses[2])
        es = [jnp.exp(l - m) for l in lses]
        den = es[0] + es[1] + es[2]
        for g, o_ref in enumerate((o0_ref, o1_ref, o2_ref)):
            wg = es[g] / den
            for j in range(DIL_HEADS_PER_GROUP):
                c0 = (g * DIL_HEADS_PER_GROUP + j) * DIL_HEAD_DIM
                od_ref[:, c0:c0 + DIL_HEAD_DIM] = (
                    o_ref[:, j * DIL_HEAD_DIM:(j + 1) * DIL_HEAD_DIM] * wg[:, j:j + 1]).astype(BF16)

    a = jnp.dot(omla_ref[...], wm_ref[...], preferred_element_type=F32)
    b = jnp.dot(od_ref[...], wd_ref[...], preferred_element_type=F32)
    out_ref[...] = (sgm_ref[...] * a + sgd_ref[...] * b).astype(out_ref.dtype)


def _merge(o_mla, o_dil, lse_dil, gates, w_o_mla, w_o_dil, d_model):
    n = o_mla.shape[0]
    tm = min(TM_MERGE, n)
    tn = min(1024, d_model)
    nj = d_model // tn
    row = lambda width: pl.BlockSpec((tm, width), lambda i, j: (i, 0))
    return pl.pallas_call(
        _merge_kernel,
        grid=(n // tm, nj),
        in_specs=[row(o_mla.shape[1])] + [row(DIL_GROUP_WIDTH)] * 3 + [row(LANES)] * 3 + [
            pl.BlockSpec((tm, tn), lambda i, j: (i, j)),
            pl.BlockSpec((tm, tn), lambda i, j: (i, j + nj)),
            pl.BlockSpec((w_o_mla.shape[0], tn), lambda i, j: (0, j)),
            pl.BlockSpec((w_o_dil.shape[0], tn), lambda i, j: (0, j))],
        out_specs=pl.BlockSpec((tm, tn), lambda i, j: (i, j)),
        out_shape=jax.ShapeDtypeStruct((n, d_model), BF16),
        scratch_shapes=[pltpu.VMEM((tm, DIL_WIDTH), BF16)],
        compiler_params=_cparams(2),
        name="branch_merge",
    )(o_mla, *o_dil, *lse_dil, gates, gates, w_o_mla, w_o_dil)


def _layer_norm(y, g, b):
    mu = jnp.mean(y, axis=-1, keepdims=True)
    yc = y - mu
    var = jnp.mean(yc * yc, axis=-1, keepdims=True)
    return yc * lax.rsqrt(var + LN_EPS) * g + b


def _pack_bf16_pairs(x):
    half = x.shape[1] // 2
    xb = x.astype(BF16).astype(F32)
    lo = lax.bitcast_convert_type(xb[:, :half], jnp.uint32) >> 16
    hi = lax.bitcast_convert_type(xb[:, half:], jnp.uint32) & jnp.uint32(0xFFFF0000)
    return lo | hi


def _unpack_bf16_pairs(w):
    lo = lax.bitcast_convert_type(w << 16, F32).astype(BF16)
    hi = lax.bitcast_convert_type(w & jnp.uint32(0xFFFF0000), F32).astype(BF16)
    return lo, hi


def _ln1_router_kernel(mg_ref, x_ref, wo_ref, g_ref, b_ref, wrh_ref, wrl_ref, br_ref,
                       x1_ref, xpk_ref, idx_ref, gate_ref, rank_ref, cnt_ref, tri_ref, run_ref, *, alpha):
    tm = x_ref.shape[0]

    @pl.when(pl.program_id(0) == 0)
    def _():
        r = lax.broadcasted_iota(jnp.int32, (tm, tm), 0)
        c = lax.broadcasted_iota(jnp.int32, (tm, tm), 1)
        tri_ref[...] = jnp.where(c < r, 1.0, 0.0).astype(BF16)
        run_ref[...] = jnp.zeros(run_ref.shape, F32)

    out = jnp.dot(mg_ref[...], wo_ref[...], preferred_element_type=F32)
    x1 = _layer_norm(alpha * x_ref[...] + out, g_ref[...], b_ref[...])
    x1_ref[...] = x1
    xpk_ref[...] = _pack_bf16_pairs(x1)
    xh = x1.astype(BF16)
    xl = (x1 - xh.astype(F32)).astype(BF16)
    logits = (jnp.dot(xh, wrh_ref[...], preferred_element_type=F32)
              + jnp.dot(xl, wrh_ref[...], preferred_element_type=F32)
              + jnp.dot(xh, wrl_ref[...], preferred_element_type=F32)) + br_ref[...]
    lane = lax.broadcasted_iota(jnp.int32, logits.shape, 1).astype(F32)
    vals, idxs = [], []
    for _ in range(TOP_K):
        m = jnp.max(logits, axis=-1, keepdims=True)
        idx = jnp.min(jnp.where(logits == m, lane, float(LANES)), axis=-1, keepdims=True)
        vals.append(m)
        idxs.append(idx)
        logits = jnp.where(lane == idx, -jnp.inf, logits)
    es = [jnp.exp(v - vals[0]) for v in vals]
    den = es[0] + es[1] + es[2] + es[3]
    onehot = jnp.zeros(lane.shape, F32)
    for k in range(TOP_K):
        onehot = onehot + jnp.where(lane == idxs[k], 1.0, 0.0)
    before = run_ref[...] + jnp.dot(tri_ref[...], onehot.astype(BF16), preferred_element_type=F32)
    run_ref[...] = run_ref[...] + jnp.sum(onehot, axis=0, keepdims=True)
    cnt_ref[...] = run_ref[...].astype(jnp.int32)
    idx_tile = jnp.zeros(lane.shape, F32)
    gate_tile = jnp.zeros(lane.shape, F32)
    rank_tile = jnp.zeros(lane.shape, F32)
    for k in range(TOP_K):
        rank = jnp.sum(jnp.where(lane == idxs[k], before, 0.0), axis=-1, keepdims=True)
        idx_tile = jnp.where(lane == float(k), idxs[k], idx_tile)
        gate_tile = jnp.where(lane == float(k), es[k] / den, gate_tile)
        rank_tile = jnp.where(lane == float(k), rank, rank_tile)
    idx_ref[...] = idx_tile.astype(jnp.int32)
    gate_ref[...] = gate_tile
    rank_ref[...] = rank_tile.astype(jnp.int32)


def _ln1_router(merged, x, w_out, ln_g, ln_b, w_rh, w_rl, b_r, alpha):
    n, d = x.shape
    tm = min(TM_LN, n)
    full = lambda a: pl.BlockSpec(a.shape, lambda i: (0,) * a.ndim)
    row = lambda width: pl.BlockSpec((tm, width), lambda i: (i, 0))
    return pl.pallas_call(
        functools.partial(_ln1_router_kernel, alpha=alpha),
        grid=(n // tm,),
        in_specs=[row(d), row(d), full(w_out), full(ln_g), full(ln_b), full(w_rh), full(w_rl), full(b_r)],
        out_specs=[row(d), row(d // 2), row(LANES), row(LANES), row(LANES),
                   pl.BlockSpec((1, LANES), lambda i: (0, 0))],
        out_shape=[jax.ShapeDtypeStruct((n, d), F32),
                   jax.ShapeDtypeStruct((n, d // 2), jnp.uint32),
                   jax.ShapeDtypeStruct((n, LANES), jnp.int32),
                   jax.ShapeDtypeStruct((n, LANES), F32),
                   jax.ShapeDtypeStruct((n, LANES), jnp.int32),
                   jax.ShapeDtypeStruct((1, LANES), jnp.int32)],
        scratch_shapes=[pltpu.VMEM((tm, tm), BF16), pltpu.VMEM((1, LANES), F32)],
        compiler_params=_cparams(1),
        name="out_proj_ln1_router",
    )(merged, x, w_out, ln_g, ln_b, w_rh, w_rl, b_r)


def _scatter_kernel(dest_ref, zrow_ref, xpk_ref, xs_hbm, zbuf, sem, zsem):
    t = pl.program_id(0)
    rows = TM_SCATTER

    @pl.when(t == 0)
    def _():
        zbuf[...] = jnp.zeros(zbuf.shape, zbuf.dtype)

        def zero_copy(i):
            row = pl.multiple_of(jnp.maximum(zrow_ref[i], 0), MOE_SUB)
            return pltpu.make_async_copy(zbuf, xs_hbm.at[pl.ds(row, MOE_SUB)], zsem)

        def zstart(i, carry):
            @pl.when(zrow_ref[i] >= 0)
            def _():
                zero_copy(i).start()
            return carry

        def zwait(i, carry):
            @pl.when(zrow_ref[i] >= 0)
            def _():
                zero_copy(i).wait()
            return carry

        lax.fori_loop(0, zrow_ref.shape[0], zstart, 0)
        lax.fori_loop(0, zrow_ref.shape[0], zwait, 0)

    def issue(r8, carry):
        for j in range(SUBLANES):
            for k in range(TOP_K):
                dst = dest_ref[(t * rows + r8 * SUBLANES + j) * TOP_K + k]
                pltpu.make_async_copy(xpk_ref.at[r8, pl.ds(j, 1)], xs_hbm.at[pl.ds(dst, 1)], sem).start()
        return carry

    groups = rows // SUBLANES
    lax.fori_loop(0, groups, issue, 0)
    pltpu.make_async_copy(xs_hbm.at[pl.ds(0, TOP_K * rows)], xs_hbm.at[pl.ds(0, TOP_K * rows)], sem).wait()


def _moe_scatter(xpk, dest, zrows, n_tiles):
    n, w = xpk.shape
    sub = SUBLANES
    xs = pl.pallas_call(
        _scatter_kernel,
        grid_spec=pltpu.PrefetchScalarGridSpec(
            num_scalar_prefetch=2,
            grid=(n // TM_SCATTER,),
            in_specs=[pl.BlockSpec((TM_SCATTER // sub, sub, w), lambda t, dest, zrows: (t, 0, 0))],
            out_specs=pl.BlockSpec(memory_space=pl.ANY),
            scratch_shapes=[pltpu.VMEM((MOE_SUB, w), xpk.dtype), pltpu.SemaphoreType.DMA,
                            pltpu.SemaphoreType.DMA]),
        out_shape=jax.ShapeDtypeStruct((n_tiles * TM_MOE, w), xpk.dtype),
        compiler_params=_cparams(1),
        name="moe_scatter",
    )(dest, zrows, xpk.reshape(n // sub, sub, w))
    return xs


def _new_expert(te_ref, t):
    return (t == 0) | (te_ref[t] != te_ref[jnp.maximum(t - 1, 0)])


def _for_sub_blocks(rows_valid, n_rows, compute, fill):
    pair = 2 * MOE_SUB
    for s in range(n_rows // pair):
        r0 = s * pair
        both, first, second = slice(r0, r0 + pair), slice(r0, r0 + MOE_SUB), slice(r0 + MOE_SUB, r0 + pair)
        pl.when(rows_valid > r0 + MOE_SUB)(functools.partial(compute, both))

        @pl.when((rows_valid > r0) & (rows_valid <= r0 + MOE_SUB))
        def _(first=first, second=second):
            compute(first)
            fill(second)

        pl.when(rows_valid <= r0)(functools.partial(fill, both))


def _ffn_a_kernel(te_ref, nt_ref, tr_ref, xs_ref, w_ref, b_ref, a_ref, wb_ref):
    t = pl.program_id(1)

    @pl.when(_new_expert(te_ref, t))
    def _():
        wb_ref[...] = w_ref[...].astype(BF16)

    def compute(rs):
        lo, hi = _unpack_bf16_pairs(xs_ref[rs, :])
        half = lo.shape[1]
        h = (jnp.dot(lo, wb_ref[0:half, :], preferred_element_type=F32)
             + jnp.dot(hi, wb_ref[half:, :], preferred_element_type=F32) + b_ref[...])
        even = lax.broadcasted_iota(jnp.int32, (h.shape[0], LANES), 1) % 2 == 0
        for g in range(h.shape[1] // (2 * LANES)):
            h0 = h[:, 2 * g * LANES:(2 * g + 1) * LANES]
            h1 = h[:, (2 * g + 1) * LANES:(2 * g + 2) * LANES]
            glu = jnp.where(even, h0, pltpu.roll(h1, 1, 1))
            lin = jnp.where(even, pltpu.roll(h0, LANES - 1, 1), h1)
            x_glu = jnp.minimum(glu, SWIGLU_LIMIT)
            x_lin = jnp.clip(lin, -SWIGLU_LIMIT, SWIGLU_LIMIT)
            a_ref[rs, g * LANES:(g + 1) * LANES] = (
                x_glu * _sigmoid(SWIGLU_ALPHA * x_glu) * (x_lin + 1.0)).astype(a_ref.dtype)

    def fill(rs):
        a_ref[rs, :] = jnp.zeros((rs.stop - rs.start, a_ref.shape[1]), a_ref.dtype)

    _for_sub_blocks(tr_ref[t], a_ref.shape[0], compute, fill)


def _ffn_b_kernel(te_ref, nt_ref, tr_ref, a_ref, w_ref, b_ref, o_ref, perm_ref, wb_ref):
    t = pl.program_id(1)

    @pl.when(_new_expert(te_ref, t))
    def _():
        for c in range(w_ref.shape[1] // LANES):
            cs = slice(c * LANES, (c + 1) * LANES)
            for g in range(w_ref.shape[0] // LANES):
                for p in range(2):
                    src = g * LANES + p * PAIR
                    perm_ref[c, pl.ds(g * LANES + p, PAIR, stride=2), :] = w_ref[src:src + PAIR, cs]
            wb_ref[:, cs] = perm_ref[c].astype(BF16)

    def compute(rs):
        o_ref[rs, :] = jnp.dot(a_ref[rs, :], wb_ref[...], preferred_element_type=F32) + b_ref[...]

    def fill(rs):
        o_ref[rs, :] = jnp.zeros((rs.stop - rs.start, o_ref.shape[1]), o_ref.dtype)

    _for_sub_blocks(tr_ref[t], o_ref.shape[0], compute, fill)


def _moe_ffn(xs, tile_expert, n_valid, tile_rows, w1, b1, w2, b2):
    rows = xs.shape[0]
    n_tiles = rows // TM_MOE
    _, d, cols = w1.shape
    d_ff = cols // 2
    wc = min(W1_COLS, cols)
    tile = lambda t, nt: jnp.minimum(t, nt[0] - 1)
    a = pl.pallas_call(
        _ffn_a_kernel,
        grid_spec=pltpu.PrefetchScalarGridSpec(
            num_scalar_prefetch=3,
            grid=(cols // wc, n_tiles),
            in_specs=[pl.BlockSpec((TM_MOE, d // 2), lambda f, t, te, nt, tr: (tile(t, nt), 0)),
                      pl.BlockSpec((None, d, wc), lambda f, t, te, nt, tr: (te[t], 0, f)),
                      pl.BlockSpec((None, 1, wc), lambda f, t, te, nt, tr: (te[t], 0, f))],
            out_specs=pl.BlockSpec((TM_MOE, wc // 2), lambda f, t, te, nt, tr: (t, f)),
            scratch_shapes=[pltpu.VMEM((d, wc), BF16)]),
        out_shape=jax.ShapeDtypeStruct((rows, d_ff), BF16),
        compiler_params=_cparams(2),
        name="moe_ffn_up",
    )(tile_expert, n_valid, tile_rows, xs, w1, b1)
    nc = min(W2_COLS, d)
    return pl.pallas_call(
        _ffn_b_kernel,
        grid_spec=pltpu.PrefetchScalarGridSpec(
            num_scalar_prefetch=3,
            grid=(d // nc, n_tiles),
            in_specs=[pl.BlockSpec((TM_MOE, d_ff), lambda c, t, te, nt, tr: (tile(t, nt), 0)),
                      pl.BlockSpec((None, d_ff, nc), lambda c, t, te, nt, tr: (te[t], 0, c)),
                      pl.BlockSpec((None, 1, nc), lambda c, t, te, nt, tr: (te[t], 0, c))],
            out_specs=pl.BlockSpec((TM_MOE, nc), lambda c, t, te, nt, tr: (t, c)),
            scratch_shapes=[pltpu.VMEM((nc // LANES, d_ff, LANES), F32), pltpu.VMEM((d_ff, nc), BF16)]),
        out_shape=jax.ShapeDtypeStruct((rows, d), F32),
        compiler_params=_cparams(2),
        name="moe_ffn_down",
    )(tile_expert, n_valid, tile_rows, a, w2, b2)


def _combine_kernel(pos_ref, y_hbm, x_ref, gate_ref, g_ref, b_ref, o_ref, buf, sem, *, alpha):
    t = pl.program_id(0)
    rows = x_ref.shape[0]
    groups = rows // SUBLANES
    slot = t % 2

    def issue(step, dst_slot):
        def body(r8, carry):
            for j in range(SUBLANES):
                for k in range(TOP_K):
                    p = pos_ref[(step * rows + r8 * SUBLANES + j) * TOP_K + k]
                    pltpu.make_async_copy(y_hbm.at[pl.ds(p, 1)],
                                          buf.at[dst_slot, k * groups + r8, pl.ds(j, 1)],
                                          sem.at[dst_slot]).start()
            return carry

        lax.fori_loop(0, groups, body, 0)

    @pl.when(t == 0)
    def _():
        issue(0, 0)

    @pl.when(t + 1 < pl.num_programs(0))
    def _():
        issue(t + 1, 1 - slot)

    pltpu.make_async_copy(y_hbm.at[pl.ds(0, TOP_K * rows)], y_hbm.at[pl.ds(0, TOP_K * rows)],
                          sem.at[slot]).wait()
    gate = gate_ref[...]
    ffn = None
    for k in range(TOP_K):
        yk = buf[slot, k * groups:(k + 1) * groups].reshape(rows, buf.shape[-1]) * gate[:, k:k + 1]
        ffn = yk if ffn is None else ffn + yk
    o_ref[...] = _layer_norm(alpha * x_ref[...] + ffn, g_ref[...], b_ref[...])


def _moe_combine(y_sorted, pos, x1, gate, ln_g, ln_b, alpha):
    n, d = x1.shape
    tm = min(TM_COMBINE, n)
    return pl.pallas_call(
        functools.partial(_combine_kernel, alpha=alpha),
        grid_spec=pltpu.PrefetchScalarGridSpec(
            num_scalar_prefetch=1,
            grid=(n // tm,),
            in_specs=[pl.BlockSpec(memory_space=pl.ANY),
                      pl.BlockSpec((tm, d), lambda t, pos: (t, 0)),
                      pl.BlockSpec((tm, LANES), lambda t, pos: (t, 0)),
                      pl.BlockSpec((1, d), lambda t, pos: (0, 0)),
                      pl.BlockSpec((1, d), lambda t, pos: (0, 0))],
            out_specs=pl.BlockSpec((tm, d), lambda t, pos: (t, 0)),
            scratch_shapes=[pltpu.VMEM((2, TOP_K * tm // SUBLANES, SUBLANES, d), F32),
                            pltpu.SemaphoreType.DMA((2,))]),
        out_shape=jax.ShapeDtypeStruct((n, d), F32),
        compiler_params=_cparams(1),
        name="moe_combine_ln2",
    )(pos, y_sorted, x1, gate, ln_g, ln_b)


def _route(top_idx, rank, counts, n):
    nk = n * TOP_K
    n_tiles = -(-nk // TM_MOE) + N_EXPERTS
    padded = (counts + TM_MOE - 1) // TM_MOE * TM_MOE
    pad_end = jnp.cumsum(padded)
    pad_start = pad_end - padded
    onehot = top_idx[:, :, None] == jnp.arange(N_EXPERTS, dtype=jnp.int32)
    dest = (rank + jnp.sum(jnp.where(onehot, pad_start, 0), axis=-1)).reshape(nk)
    n_valid = (pad_end[-1] // TM_MOE).astype(jnp.int32)
    tiles = jnp.arange(n_tiles, dtype=jnp.int32)
    tile_start = jnp.minimum(tiles, n_valid - 1) * TM_MOE
    tile_expert = jnp.minimum(jnp.sum(pad_end[None, :] <= tile_start[:, None], axis=1),
                              N_EXPERTS - 1).astype(jnp.int32)
    rows_in = counts[tile_expert] - (tiles * TM_MOE - pad_start[tile_expert])
    tile_rows = jnp.where(tiles < n_valid, jnp.clip(rows_in, 0, TM_MOE), 0).astype(jnp.int32)
    per_tile = TM_MOE // MOE_SUB
    subs = jnp.arange(n_tiles * per_tile, dtype=jnp.int32)
    covered = (subs % per_tile + 1) * MOE_SUB <= tile_rows[subs // per_tile]
    zrows = jnp.where(covered, -1, subs * MOE_SUB).astype(jnp.int32)
    return dest.astype(jnp.int32), zrows, tile_expert, n_valid.reshape(1), tile_rows, n_tiles


def _rope_tables(seq):
    half = MLA_ROPE_DIM // 2
    inv = ROPE_THETA ** (-jnp.arange(half, dtype=F32) / half)
    ang = jnp.arange(seq).astype(F32)[:, None] * inv[None, :]
    cos, sin = jnp.cos(ang), jnp.sin(ang)
    z = jnp.zeros((seq, LANES - MLA_ROPE_DIM), F32)
    zh = jnp.zeros((seq, half), F32)
    return (jnp.concatenate([cos, cos, z], axis=1),
            jnp.concatenate([-sin, zh, z], axis=1),
            jnp.concatenate([zh, sin, z], axis=1))


def _pad_cols(w, width):
    return jnp.pad(w, ((0, 0), (0, width - w.shape[1])))


def kernel(x, w_in, q_norm_g, kv_norm_g, w_uq, w_ukv, w_o_mla, w_o_dil, w_out, ln1_g, ln1_b,
           w_router, b_router, w1, b1, w2, b2, ln2_g, ln2_b):
    batch, seq, d_model = x.shape
    depth = w_in.shape[0]
    n = batch * seq
    alpha = (2.0 * depth) ** 0.25
    tabs = _rope_tables(seq)
    o_q, o_kv, o_pe = 0, MLA_Q_RANK, MLA_Q_RANK + MLA_KV_RANK
    o_dil = o_pe + MLA_ROPE_DIM
    o_gate = o_dil + 3 * DIL_WIDTH
    xt = x.reshape(n, d_model)
    for l in range(depth):
        wi = w_in[l]
        w_a = jnp.concatenate([wi[:, o_kv:o_pe], _pad_cols(wi[:, o_pe:o_dil], 2 * LANES),
                               wi[:, o_q:o_kv]], axis=1).astype(BF16)
        w_b = jnp.concatenate(
            [wi[:, o_dil + which * DIL_WIDTH + g * DIL_GROUP_WIDTH:][:, :DIL_GROUP_WIDTH]
             for g in range(len(DIL_GROUPS)) for which in range(3)], axis=1).astype(BF16)
        w_c = wi[:, o_gate:].astype(BF16)
        wq = w_uq[l].reshape(MLA_Q_RANK, MLA_HEADS, MLA_QK_DIM)
        wq = jnp.pad(wq, ((0, 0), (0, 0), (0, MLA_QK_PAD - MLA_QK_DIM)))
        wq = wq.reshape(MLA_Q_RANK, MLA_HEADS * MLA_QK_PAD).astype(BF16)

        proj_a = _proj(xt, w_a, F32, w_a.shape[1], False, "in_proj_latent")
        qkv_d = _proj_dilated(xt, w_b)
        gates = _proj(xt, w_c, F32, 1024, True, "in_proj_gates")

        q, k, v = _mla_up(proj_a, q_norm_g[l][None], kv_norm_g[l][None], wq,
                          w_ukv[l].astype(BF16), tabs, seq)
        o_mla = _mla_attention(q, k, v, batch, seq)
        dil = [_dilated_group(qkv_d[g], g, batch, seq) for g in range(len(DIL_GROUPS))]
        merged = _merge(o_mla, [o for o, _ in dil], [s for _, s in dil], gates,
                        w_o_mla[l].astype(BF16), w_o_dil[l].astype(BF16), d_model)

        w_r = _pad_cols(w_router[l], LANES)
        w_rh = w_r.astype(BF16)
        w_rl = (w_r - w_rh.astype(F32)).astype(BF16)
        b_r = jnp.concatenate([b_router[l], jnp.full((LANES - N_EXPERTS,), NEG_BIG, F32)])[None]
        x1, xpk, top_idx, top_gate, rank, counts = _ln1_router(
            merged, xt, w_out[l].astype(BF16), ln1_g[l][None], ln1_b[l][None], w_rh, w_rl, b_r, alpha)

        dest, zrows, tile_expert, n_valid, tile_rows, n_tiles = _route(
            top_idx[:, :TOP_K], rank[:, :TOP_K], counts[0, :N_EXPERTS], n)
        xs = _moe_scatter(xpk, dest, zrows, n_tiles)
        y_sorted = _moe_ffn(xs, tile_expert, n_valid, tile_rows, w1[l], b1[l][:, None, :], w2[l],
                            b2[l][:, None, :])
        xt = _moe_combine(y_sorted, dest, x1, top_gate, ln2_g[l][None], ln2_b[l][None], alpha)
    return xt.reshape(batch, seq, d_model)
```

```python
import functools

import numpy as np
import jax
import jax.numpy as jnp
from jax import lax
from jax.experimental import pallas as pl
from jax.experimental.pallas import tpu as pltpu

F32 = jnp.float32
BF16 = jnp.bfloat16

MLA_HEADS = 16
MLA_Q_RANK = 768
MLA_KV_RANK = 512
MLA_NOPE_DIM = 128
MLA_ROPE_DIM = 64
MLA_V_DIM = 128
MLA_QK_DIM = MLA_NOPE_DIM + MLA_ROPE_DIM
MLA_QK_PAD = 256
MLA_Q_SCALE = MLA_QK_DIM ** -0.5 * 1.4426950408889634
ROPE_THETA = 10000.0
DIL_GROUPS = ((128, 1), (512, 4), (2048, 16))
DIL_HEADS_PER_GROUP = 4
DIL_HEADS = DIL_HEADS_PER_GROUP * len(DIL_GROUPS)
DIL_HEAD_DIM = 128
DIL_GROUP_WIDTH = DIL_HEADS_PER_GROUP * DIL_HEAD_DIM
DIL_WIDTH = DIL_HEADS * DIL_HEAD_DIM
DIL_BLOCK = 128
DIL_RES_PER_STEP = 4
N_EXPERTS = 32
TOP_K = 4
SWIGLU_LIMIT = 7.0
SWIGLU_ALPHA = 1.702
LN_EPS = 1e-5
RMS_EPS = 1e-6
LANES = 128
SUBLANES = 8
NEG_BIG = -1e30

VMEM_LIMIT = 56 * 1024 * 1024
TM_PROJ = 1024
TM_PROJ_DIL = 512
TM_UP = 512
TM_MERGE = 512
TM_LN = 512
TM_MOE = 1024
MOE_SUB = 256
TM_SCATTER = 512
TM_COMBINE = 256
W1_COLS = 1024
W2_COLS = 1024
PAIR = LANES // 2


def _cparams(n_axes):
    return pltpu.CompilerParams(dimension_semantics=("arbitrary",) * n_axes,
                                vmem_limit_bytes=VMEM_LIMIT)


def _sigmoid(x):
    return 1.0 / (1.0 + jnp.exp(-x))


def _proj_kernel(x_ref, w_ref, o_ref, xb_ref, *, act):
    @pl.when(pl.program_id(1) == 0)
    def _():
        xb_ref[...] = x_ref[...].astype(BF16)

    acc = jnp.dot(xb_ref[...], w_ref[...], preferred_element_type=F32)
    if act:
        acc = _sigmoid(acc)
    o_ref[...] = acc.astype(o_ref.dtype)


def _proj(x, w, out_dtype, tn, act, name):
    m, k = x.shape
    nc = w.shape[1]
    tm = min(TM_PROJ, m)
    return pl.pallas_call(
        functools.partial(_proj_kernel, act=act),
        grid=(m // tm, nc // tn),
        in_specs=[pl.BlockSpec((tm, k), lambda i, j: (i, 0)),
                  pl.BlockSpec((k, tn), lambda i, j: (0, j))],
        out_specs=pl.BlockSpec((tm, tn), lambda i, j: (i, j)),
        out_shape=jax.ShapeDtypeStruct((m, nc), out_dtype),
        scratch_shapes=[pltpu.VMEM((tm, k), BF16)],
        compiler_params=_cparams(2),
        name=name,
    )(x, w)


def _proj_dil_kernel(x_ref, w_ref, o0_ref, o1_ref, o2_ref, xb_ref, acc_ref):
    j = pl.program_id(1)
    tm = x_ref.shape[0]
    width = w_ref.shape[1]

    @pl.when(j == 0)
    def _():
        xb_ref[...] = x_ref[...].astype(BF16)

    acc = jnp.dot(xb_ref[...], w_ref[...], preferred_element_type=F32)
    n_cb = width // LANES
    for c in range(n_cb):
        acc_ref[c] = acc[:, c * LANES:(c + 1) * LANES]
    for g, o_ref in enumerate((o0_ref, o1_ref, o2_ref)):
        dil = DIL_GROUPS[g][1]

        @pl.when(j == g)
        def _(o_ref=o_ref, dil=dil):
            for r in range(dil):
                for c in range(n_cb):
                    o_ref[:, r * width + c * LANES:r * width + (c + 1) * LANES] = (
                        acc_ref[c, pl.ds(r, tm // dil, stride=dil), :].astype(BF16))


def _proj_dilated(x, w):
    m, k = x.shape
    tm = min(TM_PROJ_DIL, m)
    width = 3 * DIL_GROUP_WIDTH
    n_groups = len(DIL_GROUPS)
    dils = [d for _, d in DIL_GROUPS]
    return pl.pallas_call(
        _proj_dil_kernel,
        grid=(m // tm, n_groups),
        in_specs=[pl.BlockSpec((tm, k), lambda i, j: (i, 0)),
                  pl.BlockSpec((k, width), lambda i, j: (0, j))],
        out_specs=[pl.BlockSpec((tm // d, d * width), lambda i, j: (i, 0)) for d in dils],
        out_shape=[jax.ShapeDtypeStruct((m // d, d * width), BF16) for d in dils],
        scratch_shapes=[pltpu.VMEM((tm, k), BF16), pltpu.VMEM((width // LANES, tm, LANES), F32)],
        compiler_params=_cparams(2),
        name="in_proj_dilated",
    )(x, w)


def _rope_tile(r, cos, sin_lo, sin_hi):
    half = MLA_ROPE_DIM // 2
    return (r * cos + pltpu.roll(r, LANES - half, 1) * sin_lo + pltpu.roll(r, half, 1) * sin_hi)


def _rms(x, g):
    ms = jnp.mean(x * x, axis=-1, keepdims=True)
    return x * lax.rsqrt(ms + RMS_EPS) * g


def _q_up_kernel(cq_ref, g_ref, w_ref, cos_ref, slo_ref, shi_ref, o_ref):
    cqn = _rms(cq_ref[...], g_ref[...]).astype(BF16)
    cos, slo, shi = cos_ref[...], slo_ref[...], shi_ref[...]
    for h in range(MLA_HEADS):
        c0 = h * MLA_QK_PAD
        qh = jnp.dot(cqn, w_ref[:, c0:c0 + MLA_QK_PAD], preferred_element_type=F32) * MLA_Q_SCALE
        o_ref[:, c0:c0 + LANES] = qh[:, :LANES].astype(BF16)
        o_ref[:, c0 + LANES:c0 + MLA_QK_PAD] = _rope_tile(qh[:, LANES:], cos, slo, shi).astype(BF16)


def _kv_up_kernel(ckv_ref, kpe_ref, g_ref, w_ref, cos_ref, slo_ref, shi_ref, k_ref, v_ref):
    ckvn = _rms(ckv_ref[...], g_ref[...]).astype(BF16)
    k_rot = _rope_tile(kpe_ref[...], cos_ref[...], slo_ref[...], shi_ref[...]).astype(BF16)
    lane = lax.broadcasted_iota(jnp.int32, kpe_ref.shape, 1)
    ones_col = jnp.where(lane == 0, 1.0, 0.0).astype(BF16)
    for h in range(MLA_HEADS):
        c0 = h * (MLA_NOPE_DIM + MLA_V_DIM)
        kv = jnp.dot(ckvn, w_ref[:, c0:c0 + MLA_NOPE_DIM + MLA_V_DIM], preferred_element_type=F32)
        k0 = h * MLA_QK_PAD
        k_ref[:, k0:k0 + LANES] = kv[:, :MLA_NOPE_DIM].astype(BF16)
        k_ref[:, k0 + LANES:k0 + MLA_QK_PAD] = k_rot
        v_ref[:, k0:k0 + MLA_V_DIM] = kv[:, MLA_NOPE_DIM:].astype(BF16)
        v_ref[:, k0 + MLA_V_DIM:k0 + MLA_QK_PAD] = ones_col


def _mla_up(proj_a, q_g, kv_g, w_uq, w_ukv, tabs, seq):
    n = proj_a.shape[0]
    tm = min(TM_UP, seq)
    sb = seq // tm
    tab_specs = [pl.BlockSpec((tm, LANES), lambda i: (i % sb, 0))] * 3
    qw = MLA_HEADS * MLA_QK_PAD
    q = pl.pallas_call(
        _q_up_kernel,
        grid=(n // tm,),
        in_specs=[pl.BlockSpec((tm, MLA_Q_RANK), lambda i: (i, 1)),
                  pl.BlockSpec((1, MLA_Q_RANK), lambda i: (0, 0)),
                  pl.BlockSpec((MLA_Q_RANK, qw), lambda i: (0, 0))] + tab_specs,
        out_specs=pl.BlockSpec((tm, qw), lambda i: (i, 0)),
        out_shape=jax.ShapeDtypeStruct((n, qw), BF16),
        compiler_params=_cparams(1),
        name="mla_q_up",
    )(proj_a, q_g, w_uq, *tabs)
    kvw = MLA_HEADS * (MLA_NOPE_DIM + MLA_V_DIM)
    k, v = pl.pallas_call(
        _kv_up_kernel,
        grid=(n // tm,),
        in_specs=[pl.BlockSpec((tm, MLA_KV_RANK), lambda i: (i, 0)),
                  pl.BlockSpec((tm, LANES), lambda i: (i, MLA_KV_RANK // LANES)),
                  pl.BlockSpec((1, MLA_KV_RANK), lambda i: (0, 0)),
                  pl.BlockSpec((MLA_KV_RANK, kvw), lambda i: (0, 0))] + tab_specs,
        out_specs=[pl.BlockSpec((tm, qw), lambda i: (i, 0)),
                   pl.BlockSpec((tm, qw), lambda i: (i, 0))],
        out_shape=[jax.ShapeDtypeStruct((n, qw), BF16),
                   jax.ShapeDtypeStruct((n, qw), BF16)],
        compiler_params=_cparams(1),
        name="mla_kv_up",
    )(proj_a, proj_a, kv_g, w_ukv, *tabs)
    return q, k, v


def _nt_dot(a, b):
    return lax.dot_general(a, b, (((1,), (1,)), ((), ())), preferred_element_type=F32)


def _mla_attn_kernel(q_ref, k_ref, v_ref, o_ref, *, tq):
    seq = q_ref.shape[0]
    row = lax.broadcasted_iota(jnp.int32, (tq, tq), 0)
    col = lax.broadcasted_iota(jnp.int32, (tq, tq), 1)
    causal = col <= row
    for qi in range(seq // tq):
        r0 = qi * tq
        q = q_ref[r0:r0 + tq, :]
        s_d = jnp.where(causal, _nt_dot(q, k_ref[r0:r0 + tq, :]), -jnp.inf)
        m = jnp.max(s_d, axis=-1, keepdims=True)
        if qi > 0:
            s_p = _nt_dot(q, k_ref[0:r0, :])
            m = jnp.maximum(m, jnp.max(s_p, axis=-1, keepdims=True))
        acc = jnp.dot(jnp.exp2(s_d - m).astype(BF16), v_ref[r0:r0 + tq, :], preferred_element_type=F32)
        if qi > 0:
            acc = acc + jnp.dot(jnp.exp2(s_p - m).astype(BF16), v_ref[0:r0, :],
                                preferred_element_type=F32)
        o_ref[r0:r0 + tq, :] = (acc[:, :MLA_V_DIM] / acc[:, MLA_V_DIM:MLA_V_DIM + 1]).astype(o_ref.dtype)


def _mla_attention(q, k, v, batch, seq):
    n = q.shape[0]
    return pl.pallas_call(
        functools.partial(_mla_attn_kernel, tq=min(512, seq)),
        grid=(batch, MLA_HEADS),
        in_specs=[pl.BlockSpec((seq, MLA_QK_PAD), lambda b, h: (b, h))] * 3,
        out_specs=pl.BlockSpec((seq, MLA_V_DIM), lambda b, h: (b, h)),
        out_shape=jax.ShapeDtypeStruct((n, MLA_HEADS * MLA_V_DIM), BF16),
        compiler_params=_cparams(2),
        name="mla_attention",
    )(q, k, v)


def _dil_kernel(qkv_ref, o_ref, lse_ref, *, dilation, slopes, scale):
    n = qkv_ref.shape[0]
    w = DIL_GROUP_WIDTH
    res = qkv_ref.shape[1] // (3 * w)
    blk = DIL_BLOCK
    ri = lax.broadcasted_iota(jnp.int32, (blk, 2 * blk), 0)
    cj = lax.broadcasted_iota(jnp.int32, (blk, 2 * blk), 1) - blk
    steps = ri - cj
    valid2 = (steps >= 0) & (steps <= blk)
    dist2 = (steps * dilation).astype(F32)
    valid1, dist1 = valid2[:, blk:], dist2[:, blk:]
    lane = lax.broadcasted_iota(jnp.int32, (blk, LANES), 1)

    def block(rr, q0, k0, klen, valid, dist):
        lse_tile = jnp.zeros((blk, LANES), F32)
        for j in range(DIL_HEADS_PER_GROUP):
            c0 = rr * 3 * w + j * DIL_HEAD_DIM
            q = qkv_ref[pl.ds(q0, blk), c0:c0 + DIL_HEAD_DIM]
            kk = qkv_ref[pl.ds(k0, klen), c0 + w:c0 + w + DIL_HEAD_DIM]
            vv = qkv_ref[pl.ds(k0, klen), c0 + 2 * w:c0 + 2 * w + DIL_HEAD_DIM]
            s = _nt_dot(q, kk) * scale - slopes[j] * dist
            s = jnp.where(valid, s, -jnp.inf)
            m = jnp.max(s, axis=-1, keepdims=True)
            p = jnp.exp(s - m)
            l = jnp.sum(p, axis=-1, keepdims=True)
            o = jnp.dot(p.astype(BF16), vv, preferred_element_type=F32) / l
            o0 = rr * w + j * DIL_HEAD_DIM
            o_ref[pl.ds(q0, blk), o0:o0 + DIL_HEAD_DIM] = o
            lse_tile = jnp.where(lane == j, m + jnp.log(l), lse_tile)
        lse_ref[pl.ds(q0, blk), rr * LANES:(rr + 1) * LANES] = lse_tile

    for rr in range(res):
        block(rr, 0, 0, blk, valid1, dist1)

        def body(i, carry, rr=rr):
            q0 = pl.multiple_of(i * blk, blk)
            k0 = pl.multiple_of(i * blk - blk, blk)
            block(rr, q0, k0, 2 * blk, valid2, dist2)
            return carry

        lax.fori_loop(1, n // blk, body, 0)


def _dilated_group(qkv, g, batch, seq):
    window, dilation = DIL_GROUPS[g]
    assert window // dilation == DIL_BLOCK and seq % (dilation * DIL_BLOCK) == 0
    n = seq // dilation
    w = DIL_GROUP_WIDTH
    qkv_v = qkv.reshape(batch, n, dilation * 3 * w)
    slopes = np.float32(2.0) ** (np.float32(-8.0) * np.arange(1, DIL_HEADS + 1, dtype=np.float32)
                                 / np.float32(DIL_HEADS))
    slopes = tuple(float(s) for s in slopes[g * DIL_HEADS_PER_GROUP:(g + 1) * DIL_HEADS_PER_GROUP])

    res = min(dilation, DIL_RES_PER_STEP)
    o, lse = pl.pallas_call(
        functools.partial(_dil_kernel, dilation=dilation, slopes=slopes, scale=DIL_HEAD_DIM ** -0.5),
        grid=(batch, dilation // res),
        in_specs=[pl.BlockSpec((None, n, res * 3 * w), lambda b, r: (b, 0, r))],
        out_specs=[pl.BlockSpec((None, n, res * w), lambda b, r: (b, 0, r)),
                   pl.BlockSpec((None, n, res * LANES), lambda b, r: (b, 0, r))],
        out_shape=[jax.ShapeDtypeStruct((batch, n, dilation * w), F32),
                   jax.ShapeDtypeStruct((batch, n, dilation * LANES), F32)],
        compiler_params=_cparams(2),
        name=f"dilated_attn_g{g}",
    )(qkv_v)
    return o.reshape(batch * n, dilation * w), lse.reshape(batch * n, dilation * LANES)


def _merge_kernel(omla_ref, o0_ref, o1_ref, o2_ref, l0_ref, l1_ref, l2_ref, sgm_ref, sgd_ref,
                  wm_ref, wd_ref, out_ref, od_ref, otok_ref, ltok_ref):
    tm = omla_ref.shape[0]
    hpg = DIL_HEADS_PER_GROUP

    @pl.when(pl.program_id(1) == 0)
    def _():
        for g, (o_ref, l_ref) in enumerate(((o0_ref, l0_ref), (o1_ref, l1_ref), (o2_ref, l2_ref))):
            dil = DIL_GROUPS[g][1]
            for r in range(dil):
                rows = pl.ds(r, tm // dil, stride=dil)
                ltok_ref[g, rows, :] = l_ref[:, r * LANES:(r + 1) * LANES]
                for j in range(hpg):
                    c0 = (r * hpg + j) * DIL_HEAD_DIM
                    otok_ref[g * hpg + j, rows, :] = o_ref[:, c0:c0 + DIL_HEAD_DIM]
        lses = [ltok_ref[g] for g in range(len(DIL_GROUPS))]
        m = jnp.maximum(jnp.maximum(lses[0], lses[1]), lses[2])
        es = [jnp.exp(l - m) for l in lses]
        den = es[0] + es[1] + es[2]
        for g in range(len(DIL_GROUPS)):
            wg = es[g] / den
            for j in range(hpg):
                c0 = (g * hpg + j) * DIL_HEAD_DIM
                od_ref[:, c0:c0 + DIL_HEAD_DIM] = (otok_ref[g * hpg + j] * wg[:, j:j + 1]).astype(BF16)

    a = jnp.dot(omla_ref[...], wm_ref[...], preferred_element_type=F32)
    b = jnp.dot(od_ref[...], wd_ref[...], preferred_element_type=F32)
    out_ref[...] = (sgm_ref[...] * a + sgd_ref[...] * b).astype(out_ref.dtype)


def _merge(o_mla, o_dil, lse_dil, gates, w_o_mla, w_o_dil, d_model):
    n = o_mla.shape[0]
    tm = min(TM_MERGE, n)
    tn = min(1024, d_model)
    nj = d_model // tn
    row = lambda width: pl.BlockSpec((tm, width), lambda i, j: (i, 0))
    dils = [d for _, d in DIL_GROUPS]
    dil_row = lambda width: [pl.BlockSpec((tm // d, d * width), lambda i, j: (i, 0)) for d in dils]
    return pl.pallas_call(
        _merge_kernel,
        grid=(n // tm, nj),
        in_specs=[row(o_mla.shape[1])] + dil_row(DIL_GROUP_WIDTH) + dil_row(LANES) + [
            pl.BlockSpec((tm, tn), lambda i, j: (i, j)),
            pl.BlockSpec((tm, tn), lambda i, j: (i, j + nj)),
            pl.BlockSpec((w_o_mla.shape[0], tn), lambda i, j: (0, j)),
            pl.BlockSpec((w_o_dil.shape[0], tn), lambda i, j: (0, j))],
        out_specs=pl.BlockSpec((tm, tn), lambda i, j: (i, j)),
        out_shape=jax.ShapeDtypeStruct((n, d_model), BF16),
        scratch_shapes=[pltpu.VMEM((tm, DIL_WIDTH), BF16),
                        pltpu.VMEM((DIL_HEADS, tm, DIL_HEAD_DIM), F32),
                        pltpu.VMEM((len(DIL_GROUPS), tm, LANES), F32)],
        compiler_params=_cparams(2),
        name="branch_merge",
    )(o_mla, *o_dil, *lse_dil, gates, gates, w_o_mla, w_o_dil)


def _layer_norm(y, g, b):
    mu = jnp.mean(y, axis=-1, keepdims=True)
    yc = y - mu
    var = jnp.mean(yc * yc, axis=-1, keepdims=True)
    return yc * lax.rsqrt(var + LN_EPS) * g + b


def _pack_bf16_pairs(x):
    half = x.shape[1] // 2
    xb = x.astype(BF16).astype(F32)
    lo = lax.bitcast_convert_type(xb[:, :half], jnp.uint32) >> 16
    hi = lax.bitcast_convert_type(xb[:, half:], jnp.uint32) & jnp.uint32(0xFFFF0000)
    return lo | hi


def _unpack_bf16_pairs(w):
    lo = lax.bitcast_convert_type(w << 16, F32).astype(BF16)
    hi = lax.bitcast_convert_type(w & jnp.uint32(0xFFFF0000), F32).astype(BF16)
    return lo, hi


def _ln1_router_kernel(mg_ref, x_ref, wo_ref, g_ref, b_ref, wrh_ref, wrl_ref, br_ref,
                       x1_ref, xpk_ref, idx_ref, gate_ref, rank_ref, cnt_ref, tri_ref, run_ref, *, alpha):
    tm = x_ref.shape[0]

    @pl.when(pl.program_id(0) == 0)
    def _():
        r = lax.broadcasted_iota(jnp.int32, (tm, tm), 0)
        c = lax.broadcasted_iota(jnp.int32, (tm, tm), 1)
        tri_ref[...] = jnp.where(c < r, 1.0, 0.0).astype(BF16)
        run_ref[...] = jnp.zeros(run_ref.shape, F32)

    out = jnp.dot(mg_ref[...], wo_ref[...], preferred_element_type=F32)
    x1 = _layer_norm(alpha * x_ref[...] + out, g_ref[...], b_ref[...])
    x1_ref[...] = x1
    xpk_ref[...] = _pack_bf16_pairs(x1)
    xh = x1.astype(BF16)
    xl = (x1 - xh.astype(F32)).astype(BF16)
    logits = (jnp.dot(xh, wrh_ref[...], preferred_element_type=F32)
              + jnp.dot(xl, wrh_ref[...], preferred_element_type=F32)
              + jnp.dot(xh, wrl_ref[...], preferred_element_type=F32)) + br_ref[...]
    lane = lax.broadcasted_iota(jnp.int32, logits.shape, 1).astype(F32)
    vals, idxs = [], []
    for _ in range(TOP_K):
        m = jnp.max(logits, axis=-1, keepdims=True)
        idx = jnp.min(jnp.where(logits == m, lane, float(LANES)), axis=-1, keepdims=True)
        vals.append(m)
        idxs.append(idx)
        logits = jnp.where(lane == idx, -jnp.inf, logits)
    es = [jnp.exp(v - vals[0]) for v in vals]
    den = es[0] + es[1] + es[2] + es[3]
    onehot = jnp.zeros(lane.shape, F32)
    for k in range(TOP_K):
        onehot = onehot + jnp.where(lane == idxs[k], 1.0, 0.0)
    before = run_ref[...] + jnp.dot(tri_ref[...], onehot.astype(BF16), preferred_element_type=F32)
    run_ref[...] = run_ref[...] + jnp.sum(onehot, axis=0, keepdims=True)
    cnt_ref[...] = run_ref[...].astype(jnp.int32)
    idx_tile = jnp.zeros(lane.shape, F32)
    gate_tile = jnp.zeros(lane.shape, F32)
    rank_tile = jnp.zeros(lane.shape, F32)
    for k in range(TOP_K):
        rank = jnp.sum(jnp.where(lane == idxs[k], before, 0.0), axis=-1, keepdims=True)
        idx_tile = jnp.where(lane == float(k), idxs[k], idx_tile)
        gate_tile = jnp.where(lane == float(k), es[k] / den, gate_tile)
        rank_tile = jnp.where(lane == float(k), rank, rank_tile)
    idx_ref[...] = idx_tile.astype(jnp.int32)
    gate_ref[...] = gate_tile
    rank_ref[...] = rank_tile.astype(jnp.int32)


def _ln1_router(merged, x, w_out, ln_g, ln_b, w_rh, w_rl, b_r, alpha):
    n, d = x.shape
    tm = min(TM_LN, n)
    full = lambda a: pl.BlockSpec(a.shape, lambda i: (0,) * a.ndim)
    row = lambda width: pl.BlockSpec((tm, width), lambda i: (i, 0))
    return pl.pallas_call(
        functools.partial(_ln1_router_kernel, alpha=alpha),
        grid=(n // tm,),
        in_specs=[row(d), row(d), full(w_out), full(ln_g), full(ln_b), full(w_rh), full(w_rl), full(b_r)],
        out_specs=[row(d), row(d // 2), row(LANES), row(LANES), row(LANES),
                   pl.BlockSpec((1, LANES), lambda i: (0, 0))],
        out_shape=[jax.ShapeDtypeStruct((n, d), F32),
                   jax.ShapeDtypeStruct((n, d // 2), jnp.uint32),
                   jax.ShapeDtypeStruct((n, LANES), jnp.int32),
                   jax.ShapeDtypeStruct((n, LANES), F32),
                   jax.ShapeDtypeStruct((n, LANES), jnp.int32),
                   jax.ShapeDtypeStruct((1, LANES), jnp.int32)],
        scratch_shapes=[pltpu.VMEM((tm, tm), BF16), pltpu.VMEM((1, LANES), F32)],
        compiler_params=_cparams(1),
        name="out_proj_ln1_router",
    )(merged, x, w_out, ln_g, ln_b, w_rh, w_rl, b_r)


def _scatter_kernel(dest_ref, zrow_ref, xpk_ref, xs_hbm, zbuf, sem, zsem):
    t = pl.program_id(0)
    rows = TM_SCATTER

    @pl.when(t == 0)
    def _():
        zbuf[...] = jnp.zeros(zbuf.shape, zbuf.dtype)

        def zero_copy(i):
            row = pl.multiple_of(jnp.maximum(zrow_ref[i], 0), MOE_SUB)
            return pltpu.make_async_copy(zbuf, xs_hbm.at[pl.ds(row, MOE_SUB)], zsem)

        def zstart(i, carry):
            @pl.when(zrow_ref[i] >= 0)
            def _():
                zero_copy(i).start()
            return carry

        def zwait(i, carry):
            @pl.when(zrow_ref[i] >= 0)
            def _():
                zero_copy(i).wait()
            return carry

        lax.fori_loop(0, zrow_ref.shape[0], zstart, 0)
        lax.fori_loop(0, zrow_ref.shape[0], zwait, 0)

    def issue(r8, carry):
        for j in range(SUBLANES):
            for k in range(TOP_K):
                dst = dest_ref[(t * rows + r8 * SUBLANES + j) * TOP_K + k]
                pltpu.make_async_copy(xpk_ref.at[r8, pl.ds(j, 1)], xs_hbm.at[pl.ds(dst, 1)], sem).start()
        return carry

    groups = rows // SUBLANES
    lax.fori_loop(0, groups, issue, 0)
    pltpu.make_async_copy(xs_hbm.at[pl.ds(0, TOP_K * rows)], xs_hbm.at[pl.ds(0, TOP_K * rows)], sem).wait()


def _moe_scatter(xpk, dest, zrows, n_tiles):
    n, w = xpk.shape
    sub = SUBLANES
    xs = pl.pallas_call(
        _scatter_kernel,
        grid_spec=pltpu.PrefetchScalarGridSpec(
            num_scalar_prefetch=2,
            grid=(n // TM_SCATTER,),
            in_specs=[pl.BlockSpec((TM_SCATTER // sub, sub, w), lambda t, dest, zrows: (t, 0, 0))],
            out_specs=pl.BlockSpec(memory_space=pl.ANY),
            scratch_shapes=[pltpu.VMEM((MOE_SUB, w), xpk.dtype), pltpu.SemaphoreType.DMA,
                            pltpu.SemaphoreType.DMA]),
        out_shape=jax.ShapeDtypeStruct((n_tiles * TM_MOE, w), xpk.dtype),
        compiler_params=_cparams(1),
        name="moe_scatter",
    )(dest, zrows, xpk.reshape(n // sub, sub, w))
    return xs


def _new_expert(te_ref, t):
    return (t == 0) | (te_ref[t] != te_ref[jnp.maximum(t - 1, 0)])


def _for_sub_blocks(skip, n_rows, compute, fill):
    pair = 2 * MOE_SUB
    for s in range(n_rows // pair):
        r0 = s * pair
        both, first, second = slice(r0, r0 + pair), slice(r0, r0 + MOE_SUB), slice(r0 + MOE_SUB, r0 + pair)
        pl.when(skip < r0 + MOE_SUB)(functools.partial(compute, both))

        @pl.when((skip >= r0 + MOE_SUB) & (skip < r0 + pair))
        def _(first=first, second=second):
            fill(first)
            compute(second)

        pl.when(skip >= r0 + pair)(functools.partial(fill, both))


def _ffn_a_kernel(te_ref, nt_ref, skip_ref, xs_ref, w_ref, b_ref, a_ref, wb_ref):
    t = pl.program_id(1)

    @pl.when(_new_expert(te_ref, t))
    def _():
        wb_ref[...] = w_ref[...].astype(BF16)

    def compute(rs):
        lo, hi = _unpack_bf16_pairs(xs_ref[rs, :])
        half = lo.shape[1]
        h = (jnp.dot(lo, wb_ref[0:half, :], preferred_element_type=F32)
             + jnp.dot(hi, wb_ref[half:, :], preferred_element_type=F32) + b_ref[...])
        even = lax.broadcasted_iota(jnp.int32, (h.shape[0], LANES), 1) % 2 == 0
        for g in range(h.shape[1] // (2 * LANES)):
            h0 = h[:, 2 * g * LANES:(2 * g + 1) * LANES]
            h1 = h[:, (2 * g + 1) * LANES:(2 * g + 2) * LANES]
            glu = jnp.where(even, h0, pltpu.roll(h1, 1, 1))
            lin = jnp.where(even, pltpu.roll(h0, LANES - 1, 1), h1)
            x_glu = jnp.minimum(glu, SWIGLU_LIMIT)
            x_lin = jnp.clip(lin, -SWIGLU_LIMIT, SWIGLU_LIMIT)
            a_ref[rs, g * LANES:(g + 1) * LANES] = (
                x_glu * _sigmoid(SWIGLU_ALPHA * x_glu) * (x_lin + 1.0)).astype(a_ref.dtype)

    def fill(rs):
        a_ref[rs, :] = jnp.zeros((rs.stop - rs.start, a_ref.shape[1]), a_ref.dtype)

    _for_sub_blocks(skip_ref[t], a_ref.shape[0], compute, fill)


def _ffn_b_kernel(te_ref, nt_ref, skip_ref, a_ref, w_ref, b_ref, o_ref, perm_ref, wb_ref):
    t = pl.program_id(1)

    @pl.when(_new_expert(te_ref, t))
    def _():
        for c in range(w_ref.shape[1] // LANES):
            cs = slice(c * LANES, (c + 1) * LANES)
            for g in range(w_ref.shape[0] // LANES):
                for p in range(2):
                    src = g * LANES + p * PAIR
                    perm_ref[c, pl.ds(g * LANES + p, PAIR, stride=2), :] = w_ref[src:src + PAIR, cs]
            wb_ref[:, cs] = perm_ref[c].astype(BF16)

    def compute(rs):
        o_ref[rs, :] = jnp.dot(a_ref[rs, :], wb_ref[...], preferred_element_type=F32) + b_ref[...]

    def fill(rs):
        o_ref[rs, :] = jnp.zeros((rs.stop - rs.start, o_ref.shape[1]), o_ref.dtype)

    _for_sub_blocks(skip_ref[t], o_ref.shape[0], compute, fill)


def _moe_ffn(xs, tile_expert, n_valid, tile_skip, w1, b1, w2, b2):
    rows = xs.shape[0]
    n_tiles = rows // TM_MOE
    _, d, cols = w1.shape
    d_ff = cols // 2
    wc = min(W1_COLS, cols)
    tile = lambda t, nt: jnp.minimum(t, nt[0] - 1)
    a = pl.pallas_call(
        _ffn_a_kernel,
        grid_spec=pltpu.PrefetchScalarGridSpec(
            num_scalar_prefetch=3,
            grid=(cols // wc, n_tiles),
            in_specs=[pl.BlockSpec((TM_MOE, d // 2), lambda f, t, te, nt, tr: (tile(t, nt), 0)),
                      pl.BlockSpec((None, d, wc), lambda f, t, te, nt, tr: (te[t], 0, f)),
                      pl.BlockSpec((None, 1, wc), lambda f, t, te, nt, tr: (te[t], 0, f))],
            out_specs=pl.BlockSpec((TM_MOE, wc // 2), lambda f, t, te, nt, tr: (t, f)),
            scratch_shapes=[pltpu.VMEM((d, wc), BF16)]),
        out_shape=jax.ShapeDtypeStruct((rows, d_ff), BF16),
        compiler_params=_cparams(2),
        name="moe_ffn_up",
    )(tile_expert, n_valid, tile_skip, xs, w1, b1)
    nc = min(W2_COLS, d)
    return pl.pallas_call(
        _ffn_b_kernel,
        grid_spec=pltpu.PrefetchScalarGridSpec(
            num_scalar_prefetch=3,
            grid=(d // nc, n_tiles),
            in_specs=[pl.BlockSpec((TM_MOE, d_ff), lambda c, t, te, nt, tr: (tile(t, nt), 0)),
                      pl.BlockSpec((None, d_ff, nc), lambda c, t, te, nt, tr: (te[t], 0, c)),
                      pl.BlockSpec((None, 1, nc), lambda c, t, te, nt, tr: (te[t], 0, c))],
            out_specs=pl.BlockSpec((TM_MOE, nc), lambda c, t, te, nt, tr: (t, c)),
            scratch_shapes=[pltpu.VMEM((nc // LANES, d_ff, LANES), F32), pltpu.VMEM((d_ff, nc), BF16)]),
        out_shape=jax.ShapeDtypeStruct((rows, d), F32),
        compiler_params=_cparams(2),
        name="moe_ffn_down",
    )(tile_expert, n_valid, tile_skip, a, w2, b2)


def _combine_kernel(pos_ref, y_hbm, x_ref, gate_ref, g_ref, b_ref, o_ref, buf, sem, *, alpha):
    t = pl.program_id(0)
    rows = x_ref.shape[0]
    groups = rows // SUBLANES
    slot = t % 2

    def issue(step, dst_slot):
        def body(r8, carry):
            for j in range(SUBLANES):
                for k in range(TOP_K):
                    p = pos_ref[(step * rows + r8 * SUBLANES + j) * TOP_K + k]
                    pltpu.make_async_copy(y_hbm.at[pl.ds(p, 1)],
                                          buf.at[dst_slot, k * groups + r8, pl.ds(j, 1)],
                                          sem.at[dst_slot]).start()
            return carry

        lax.fori_loop(0, groups, body, 0)

    @pl.when(t == 0)
    def _():
        issue(0, 0)

    @pl.when(t + 1 < pl.num_programs(0))
    def _():
        issue(t + 1, 1 - slot)

    pltpu.make_async_copy(y_hbm.at[pl.ds(0, TOP_K * rows)], y_hbm.at[pl.ds(0, TOP_K * rows)],
                          sem.at[slot]).wait()
    gate = gate_ref[...]
    ffn = None
    for k in range(TOP_K):
        yk = buf[slot, k * groups:(k + 1) * groups].reshape(rows, buf.shape[-1]) * gate[:, k:k + 1]
        ffn = yk if ffn is None else ffn + yk
    o_ref[...] = _layer_norm(alpha * x_ref[...] + ffn, g_ref[...], b_ref[...])


def _moe_combine(y_sorted, pos, x1, gate, ln_g, ln_b, alpha):
    n, d = x1.shape
    tm = min(TM_COMBINE, n)
    return pl.pallas_call(
        functools.partial(_combine_kernel, alpha=alpha),
        grid_spec=pltpu.PrefetchScalarGridSpec(
            num_scalar_prefetch=1,
            grid=(n // tm,),
            in_specs=[pl.BlockSpec(memory_space=pl.ANY),
                      pl.BlockSpec((tm, d), lambda t, pos: (t, 0)),
                      pl.BlockSpec((tm, LANES), lambda t, pos: (t, 0)),
                      pl.BlockSpec((1, d), lambda t, pos: (0, 0)),
                      pl.BlockSpec((1, d), lambda t, pos: (0, 0))],
            out_specs=pl.BlockSpec((tm, d), lambda t, pos: (t, 0)),
            scratch_shapes=[pltpu.VMEM((2, TOP_K * tm // SUBLANES, SUBLANES, d), F32),
                            pltpu.SemaphoreType.DMA((2,))]),
        out_shape=jax.ShapeDtypeStruct((n, d), F32),
        compiler_params=_cparams(1),
        name="moe_combine_ln2",
    )(pos, y_sorted, x1, gate, ln_g, ln_b)


def _route(top_idx, rank, counts, n):
    nk = n * TOP_K
    n_tiles = -(-nk // TM_MOE) + N_EXPERTS
    padded = (counts + TM_MOE - 1) // TM_MOE * TM_MOE
    pad_end = jnp.cumsum(padded)
    pad_start = pad_end - padded
    lead = padded - counts
    onehot = top_idx[:, :, None] == jnp.arange(N_EXPERTS, dtype=jnp.int32)
    dest = (rank + jnp.sum(jnp.where(onehot, pad_start + lead, 0), axis=-1)).reshape(nk)
    n_valid = (pad_end[-1] // TM_MOE).astype(jnp.int32)
    tiles = jnp.arange(n_tiles, dtype=jnp.int32)
    tile_start = jnp.minimum(tiles, n_valid - 1) * TM_MOE
    tile_expert = jnp.minimum(jnp.sum(pad_end[None, :] <= tile_start[:, None], axis=1),
                              N_EXPERTS - 1).astype(jnp.int32)
    skip_in = lead[tile_expert] - (tiles * TM_MOE - pad_start[tile_expert])
    tile_skip = jnp.where(tiles < n_valid, jnp.clip(skip_in, 0, TM_MOE), TM_MOE).astype(jnp.int32)
    per_tile = TM_MOE // MOE_SUB
    subs = jnp.arange(n_tiles * per_tile, dtype=jnp.int32)
    covered = (subs % per_tile) * MOE_SUB >= tile_skip[subs // per_tile]
    zrows = jnp.where(covered, -1, subs * MOE_SUB).astype(jnp.int32)
    return dest.astype(jnp.int32), zrows, tile_expert, n_valid.reshape(1), tile_skip, n_tiles


def _rope_tables(seq):
    half = MLA_ROPE_DIM // 2
    inv = ROPE_THETA ** (-jnp.arange(half, dtype=F32) / half)
    ang = jnp.arange(seq).astype(F32)[:, None] * inv[None, :]
    cos, sin = jnp.cos(ang), jnp.sin(ang)
    z = jnp.zeros((seq, LANES - MLA_ROPE_DIM), F32)
    zh = jnp.zeros((seq, half), F32)
    return (jnp.concatenate([cos, cos, z], axis=1),
            jnp.concatenate([-sin, zh, z], axis=1),
            jnp.concatenate([zh, sin, z], axis=1))


def _pad_cols(w, width):
    return jnp.pad(w, ((0, 0), (0, width - w.shape[1])))


def kernel(x, w_in, q_norm_g, kv_norm_g, w_uq, w_ukv, w_o_mla, w_o_dil, w_out, ln1_g, ln1_b,
           w_router, b_router, w1, b1, w2, b2, ln2_g, ln2_b):
    batch, seq, d_model = x.shape
    depth = w_in.shape[0]
    n = batch * seq
    alpha = (2.0 * depth) ** 0.25
    tabs = _rope_tables(seq)
    o_q, o_kv, o_pe = 0, MLA_Q_RANK, MLA_Q_RANK + MLA_KV_RANK
    o_dil = o_pe + MLA_ROPE_DIM
    o_gate = o_dil + 3 * DIL_WIDTH
    xt = x.reshape(n, d_model)
    for l in range(depth):
        wi = w_in[l]
        w_a = jnp.concatenate([wi[:, o_kv:o_pe], _pad_cols(wi[:, o_pe:o_dil], 2 * LANES),
                               wi[:, o_q:o_kv]], axis=1).astype(BF16)
        w_b = jnp.concatenate(
            [wi[:, o_dil + which * DIL_WIDTH + g * DIL_GROUP_WIDTH:][:, :DIL_GROUP_WIDTH]
             for g in range(len(DIL_GROUPS)) for which in range(3)], axis=1).astype(BF16)
        w_c = wi[:, o_gate:].astype(BF16)
        wq = w_uq[l].reshape(MLA_Q_RANK, MLA_HEADS, MLA_QK_DIM)
        wq = jnp.pad(wq, ((0, 0), (0, 0), (0, MLA_QK_PAD - MLA_QK_DIM)))
        wq = wq.reshape(MLA_Q_RANK, MLA_HEADS * MLA_QK_PAD).astype(BF16)

        proj_a = _proj(xt, w_a, F32, w_a.shape[1], False, "in_proj_latent")
        qkv_d = _proj_dilated(xt, w_b)
        gates = _proj(xt, w_c, F32, 1024, True, "in_proj_gates")

        q, k, v = _mla_up(proj_a, q_norm_g[l][None], kv_norm_g[l][None], wq,
                          w_ukv[l].astype(BF16), tabs, seq)
        o_mla = _mla_attention(q, k, v, batch, seq)
        dil = [_dilated_group(qkv_d[g], g, batch, seq) for g in range(len(DIL_GROUPS))]
        merged = _merge(o_mla, [o for o, _ in dil], [s for _, s in dil], gates,
                        w_o_mla[l].astype(BF16), w_o_dil[l].astype(BF16), d_model)

        w_r = _pad_cols(w_router[l], LANES)
        w_rh = w_r.astype(BF16)
        w_rl = (w_r - w_rh.astype(F32)).astype(BF16)
        b_r = jnp.concatenate([b_router[l], jnp.full((LANES - N_EXPERTS,), NEG_BIG, F32)])[None]
        x1, xpk, top_idx, top_gate, rank, counts = _ln1_router(
            merged, xt, w_out[l].astype(BF16), ln1_g[l][None], ln1_b[l][None], w_rh, w_rl, b_r, alpha)

        dest, zrows, tile_expert, n_valid, tile_skip, n_tiles = _route(
            top_idx[:, :TOP_K], rank[:, :TOP_K], counts[0, :N_EXPERTS], n)
        xs = _moe_scatter(xpk, dest, zrows, n_tiles)
        y_sorted = _moe_ffn(xs, tile_expert, n_valid, tile_skip, w1[l], b1[l][:, None, :], w2[l],
                            b2[l][:, None, :])
        xt = _moe_combine(y_sorted, dest, x1, top_gate, ln2_g[l][None], ln2_b[l][None], alpha)
    return xt.reshape(batch, seq, d_model)
```

```python
import functools

import numpy as np
import jax
import jax.numpy as jnp
from jax import lax
from jax.experimental import pallas as pl
from jax.experimental.pallas import tpu as pltpu

F32 = jnp.float32
BF16 = jnp.bfloat16

MLA_HEADS = 16
MLA_Q_RANK = 768
MLA_KV_RANK = 512
MLA_NOPE_DIM = 128
MLA_ROPE_DIM = 64
MLA_V_DIM = 128
MLA_QK_DIM = MLA_NOPE_DIM + MLA_ROPE_DIM
MLA_QK_PAD = 256
MLA_HEADS_PER_STEP = 4
MLA_Q_SCALE =MLA_QK_DIM ** -0.5 * 1.4426950408889634
ROPE_THETA = 10000.0
DIL_GROUPS = ((128, 1), (512, 4), (2048, 16))
DIL_HEADS_PER_GROUP = 4
DIL_HEADS = DIL_HEADS_PER_GROUP * len(DIL_GROUPS)
DIL_HEAD_DIM = 128
DIL_GROUP_WIDTH = DIL_HEADS_PER_GROUP * DIL_HEAD_DIM
DIL_WIDTH = DIL_HEADS * DIL_HEAD_DIM
DIL_BLOCK = 128
DIL_RES_PER_STEP = 4
N_EXPERTS = 32
TOP_K = 4
SWIGLU_LIMIT = 7.0
SWIGLU_ALPHA = 1.702
LN_EPS = 1e-5
RMS_EPS = 1e-6
LANES = 128
SUBLANES = 8
NEG_BIG = -1e30

VMEM_LIMIT = 56 * 1024 * 1024
TM_PROJ = 1024
TM_PROJ_DIL = 512
TM_UP = 512
TM_MERGE = 512
TM_LN = 512
LN_SUB = 512
TM_MOE = 1024
MOE_SUB = 256
TM_SCATTER = 512
TM_COMBINE = 256
W1_COLS = 1024
W2_COLS = 1024
PAIR = LANES // 2


def _cparams(n_axes):
    return pltpu.CompilerParams(dimension_semantics=("arbitrary",) * n_axes,
                                vmem_limit_bytes=VMEM_LIMIT)


def _sigmoid(x):
    return 1.0 / (1.0 + jnp.exp(-x))


def _proj_kernel(x_ref, w_ref, o_ref, xb_ref, *, act):
    @pl.when(pl.program_id(1) == 0)
    def _():
        xb_ref[...] = x_ref[...].astype(BF16)

    acc = jnp.dot(xb_ref[...], w_ref[...], preferred_element_type=F32)
    if act:
        acc = _sigmoid(acc)
    o_ref[...] = acc.astype(o_ref.dtype)


def _proj(x, w, out_dtype, tn, act, name):
    m, k = x.shape
    nc = w.shape[1]
    tm = min(TM_PROJ, m)
    return pl.pallas_call(
        functools.partial(_proj_kernel, act=act),
        grid=(m // tm, nc // tn),
        in_specs=[pl.BlockSpec((tm, k), lambda i, j: (i, 0)),
                  pl.BlockSpec((k, tn), lambda i, j: (0, j))],
        out_specs=pl.BlockSpec((tm, tn), lambda i, j: (i, j)),
        out_shape=jax.ShapeDtypeStruct((m, nc), out_dtype),
        scratch_shapes=[pltpu.VMEM((tm, k), BF16)],
        compiler_params=_cparams(2),
        name=name,
    )(x, w)


def _proj_dil_kernel(x_ref, w_ref, o0_ref, o1_ref, o2_ref, xb_ref, acc_ref):
    j = pl.program_id(1)
    tm = x_ref.shape[0]
    width = w_ref.shape[1]

    @pl.when(j == 0)
    def _():
        xb_ref[...] = x_ref[...].astype(BF16)

    acc = jnp.dot(xb_ref[...], w_ref[...], preferred_element_type=F32)
    n_cb = width // LANES
    for c in range(n_cb):
        acc_ref[c] = acc[:, c * LANES:(c + 1) * LANES]
    for g, o_ref in enumerate((o0_ref, o1_ref, o2_ref)):
        dil = DIL_GROUPS[g][1]

        @pl.when(j == g)
        def _(o_ref=o_ref, dil=dil):
            for r in range(dil):
                for c in range(n_cb):
                    o_ref[:, r * width + c * LANES:r * width + (c + 1) * LANES] = (
                        acc_ref[c, pl.ds(r, tm // dil, stride=dil), :].astype(BF16))


def _proj_dilated(x, w):
    m, k = x.shape
    tm = min(TM_PROJ_DIL, m)
    width = 3 * DIL_GROUP_WIDTH
    n_groups = len(DIL_GROUPS)
    dils = [d for _, d in DIL_GROUPS]
    return pl.pallas_call(
        _proj_dil_kernel,
        grid=(m // tm, n_groups),
        in_specs=[pl.BlockSpec((tm, k), lambda i, j: (i, 0)),
                  pl.BlockSpec((k, width), lambda i, j: (0, j))],
        out_specs=[pl.BlockSpec((tm // d, d * width), lambda i, j: (i, 0)) for d in dils],
        out_shape=[jax.ShapeDtypeStruct((m // d, d * width), BF16) for d in dils],
        scratch_shapes=[pltpu.VMEM((tm, k), BF16), pltpu.VMEM((width // LANES, tm, LANES), F32)],
        compiler_params=_cparams(2),
        name="in_proj_dilated",
    )(x, w)


def _rope_tile(r, cos, sin_lo, sin_hi):
    half = MLA_ROPE_DIM // 2
    return (r * cos + pltpu.roll(r, LANES - half, 1) * sin_lo + pltpu.roll(r, half, 1) * sin_hi)


def _rms(x, g):
    ms = jnp.mean(x * x, axis=-1, keepdims=True)
    return x * lax.rsqrt(ms + RMS_EPS) * g


def _q_up_kernel(cq_ref, g_ref, w_ref, cos_ref, slo_ref, shi_ref, o_ref):
    cqn = _rms(cq_ref[...], g_ref[...]).astype(BF16)
    cos, slo, shi = cos_ref[...], slo_ref[...], shi_ref[...]
    for h in range(MLA_HEADS):
        c0 = h * MLA_QK_PAD
        qh = jnp.dot(cqn, w_ref[:, c0:c0 + MLA_QK_PAD], preferred_element_type=F32) * MLA_Q_SCALE
        o_ref[:, c0:c0 + LANES] = qh[:, :LANES].astype(BF16)
        o_ref[:, c0 + LANES:c0 + MLA_QK_PAD] = _rope_tile(qh[:, LANES:], cos, slo, shi).astype(BF16)


def _kv_up_kernel(ckv_ref, kpe_ref, g_ref, w_ref, cos_ref, slo_ref, shi_ref, k_ref, v_ref):
    ckvn = _rms(ckv_ref[...], g_ref[...]).astype(BF16)
    k_rot = _rope_tile(kpe_ref[...], cos_ref[...], slo_ref[...], shi_ref[...]).astype(BF16)
    lane = lax.broadcasted_iota(jnp.int32, kpe_ref.shape, 1)
    ones_col = jnp.where(lane == 0, 1.0, 0.0).astype(BF16)
    for h in range(MLA_HEADS):
        c0 = h * (MLA_NOPE_DIM + MLA_V_DIM)
        kv = jnp.dot(ckvn, w_ref[:, c0:c0 + MLA_NOPE_DIM + MLA_V_DIM], preferred_element_type=F32)
        k0 = h * MLA_QK_PAD
        k_ref[:, k0:k0 + LANES] = kv[:, :MLA_NOPE_DIM].astype(BF16)
        k_ref[:, k0 + LANES:k0 + MLA_QK_PAD] = k_rot
        v_ref[:, k0:k0 + MLA_V_DIM] = kv[:, MLA_NOPE_DIM:].astype(BF16)
        v_ref[:, k0 + MLA_V_DIM:k0 + MLA_QK_PAD] = ones_col


def _mla_up(proj_a, q_g, kv_g, w_uq, w_ukv, tabs, seq):
    n = proj_a.shape[0]
    tm = min(TM_UP, seq)
    sb = seq // tm
    tab_specs = [pl.BlockSpec((tm, LANES), lambda i: (i % sb, 0))] * 3
    qw = MLA_HEADS * MLA_QK_PAD
    q = pl.pallas_call(
        _q_up_kernel,
        grid=(n // tm,),
        in_specs=[pl.BlockSpec((tm, MLA_Q_RANK), lambda i: (i, 1)),
                  pl.BlockSpec((1, MLA_Q_RANK), lambda i: (0, 0)),
                  pl.BlockSpec((MLA_Q_RANK, qw), lambda i: (0, 0))] + tab_specs,
        out_specs=pl.BlockSpec((tm, qw), lambda i: (i, 0)),
        out_shape=jax.ShapeDtypeStruct((n, qw), BF16),
        compiler_params=_cparams(1),
        name="mla_q_up",
    )(proj_a, q_g, w_uq, *tabs)
    kvw = MLA_HEADS * (MLA_NOPE_DIM + MLA_V_DIM)
    k, v = pl.pallas_call(
        _kv_up_kernel,
        grid=(n // tm,),
        in_specs=[pl.BlockSpec((tm, MLA_KV_RANK), lambda i: (i, 0)),
                  pl.BlockSpec((tm, LANES), lambda i: (i, MLA_KV_RANK // LANES)),
                  pl.BlockSpec((1, MLA_KV_RANK), lambda i: (0, 0)),
                  pl.BlockSpec((MLA_KV_RANK, kvw), lambda i: (0, 0))] + tab_specs,
        out_specs=[pl.BlockSpec((tm, qw), lambda i: (i, 0)),
                   pl.BlockSpec((tm, qw), lambda i: (i, 0))],
        out_shape=[jax.ShapeDtypeStruct((n, qw), BF16),
                   jax.ShapeDtypeStruct((n, qw), BF16)],
        compiler_params=_cparams(1),
        name="mla_kv_up",
    )(proj_a, proj_a, kv_g, w_ukv, *tabs)
    return q, k, v


def _nt_dot(a, b):
    return lax.dot_general(a, b, (((1,), (1,)), ((), ())), preferred_element_type=F32)


def _mla_attn_kernel(q_ref, k_ref, v_ref, o_ref, *, tq):
    seq = q_ref.shape[0]
    row = lax.broadcasted_iota(jnp.int32, (tq, tq), 0)
    col = lax.broadcasted_iota(jnp.int32, (tq, tq), 1)
    causal = col <= row
    for hh in range(q_ref.shape[1] // MLA_QK_PAD):
        cq = slice(hh * MLA_QK_PAD, (hh + 1) * MLA_QK_PAD)
        for qi in range(seq // tq):
            r0 = qi * tq
            q = q_ref[r0:r0 + tq, cq]
            s_d = jnp.where(causal, _nt_dot(q, k_ref[r0:r0 + tq, cq]), -jnp.inf)
            m = jnp.max(s_d, axis=-1, keepdims=True)
            if qi > 0:
                s_p = _nt_dot(q, k_ref[0:r0, cq])
                m = jnp.maximum(m, jnp.max(s_p, axis=-1, keepdims=True))
            acc = jnp.dot(jnp.exp2(s_d - m).astype(BF16), v_ref[r0:r0 + tq, cq],
                          preferred_element_type=F32)
            if qi > 0:
                acc = acc + jnp.dot(jnp.exp2(s_p - m).astype(BF16), v_ref[0:r0, cq],
                                    preferred_element_type=F32)
            o_ref[r0:r0 + tq, hh * MLA_V_DIM:(hh + 1) * MLA_V_DIM] = (
                acc[:, :MLA_V_DIM] / acc[:, MLA_V_DIM:MLA_V_DIM + 1]).astype(o_ref.dtype)


def _mla_attention(q, k, v, batch, seq):
    n = q.shape[0]
    return pl.pallas_call(
        functools.partial(_mla_attn_kernel, tq=min(512, seq)),
        grid=(batch, MLA_HEADS // MLA_HEADS_PER_STEP),
        in_specs=[pl.BlockSpec((seq, MLA_HEADS_PER_STEP * MLA_QK_PAD), lambda b, h: (b, h))] * 3,
        out_specs=pl.BlockSpec((seq, MLA_HEADS_PER_STEP * MLA_V_DIM), lambda b, h: (b, h)),
        out_shape=jax.ShapeDtypeStruct((n, MLA_HEADS * MLA_V_DIM), BF16),
        compiler_params=_cparams(2),
        name="mla_attention",
    )(q, k, v)


def _dil_kernel(qkv_ref, o_ref, lse_ref, *, dilation, slopes, scale):
    n = qkv_ref.shape[0]
    w = DIL_GROUP_WIDTH
    res = qkv_ref.shape[1] // (3 * w)
    blk = DIL_BLOCK
    ri = lax.broadcasted_iota(jnp.int32, (blk, 2 * blk), 0)
    cj = lax.broadcasted_iota(jnp.int32, (blk, 2 * blk), 1) - blk
    steps = ri - cj
    valid2 = (steps >= 0) & (steps <= blk)
    dist2 = (steps * dilation).astype(F32)
    valid1, dist1 = valid2[:, blk:], dist2[:, blk:]
    lane = lax.broadcasted_iota(jnp.int32, (blk, LANES), 1)

    def block(rr, q0, k0, klen, valid, dist):
        lse_tile = jnp.zeros((blk, LANES), F32)
        for j in range(DIL_HEADS_PER_GROUP):
            c0 = rr * 3 * w + j * DIL_HEAD_DIM
            q = qkv_ref[pl.ds(q0, blk), c0:c0 + DIL_HEAD_DIM]
            kk = qkv_ref[pl.ds(k0, klen), c0 + w:c0 + w + DIL_HEAD_DIM]
            vv = qkv_ref[pl.ds(k0, klen), c0 + 2 * w:c0 + 2 * w + DIL_HEAD_DIM]
            s = _nt_dot(q, kk) * scale - slopes[j] * dist
            s = jnp.where(valid, s, -jnp.inf)
            m = jnp.max(s, axis=-1, keepdims=True)
            p = jnp.exp(s - m)
            l = jnp.sum(p, axis=-1, keepdims=True)
            o = jnp.dot(p.astype(BF16), vv, preferred_element_type=F32) / l
            o0 = rr * w + j * DIL_HEAD_DIM
            o_ref[pl.ds(q0, blk), o0:o0 + DIL_HEAD_DIM] = o
            lse_tile = jnp.where(lane == j, m + jnp.log(l), lse_tile)
        lse_ref[pl.ds(q0, blk), rr * LANES:(rr + 1) * LANES] = lse_tile

    for rr in range(res):
        block(rr, 0, 0, blk, valid1, dist1)

        def body(i, carry, rr=rr):
            q0 = pl.multiple_of(i * blk, blk)
            k0 = pl.multiple_of(i * blk - blk, blk)
            block(rr, q0, k0, 2 * blk, valid2, dist2)
            return carry

        lax.fori_loop(1, n // blk, body, 0, unroll=3)


def _dilated_group(qkv, g, batch, seq):
    window, dilation = DIL_GROUPS[g]
    assert window // dilation == DIL_BLOCK and seq % (dilation * DIL_BLOCK) == 0
    n = seq // dilation
    w = DIL_GROUP_WIDTH
    qkv_v = qkv.reshape(batch, n, dilation * 3 * w)
    slopes = np.float32(2.0) ** (np.float32(-8.0) * np.arange(1, DIL_HEADS + 1, dtype=np.float32)
                                 / np.float32(DIL_HEADS))
    slopes = tuple(float(s) for s in slopes[g * DIL_HEADS_PER_GROUP:(g + 1) * DIL_HEADS_PER_GROUP])

    res = min(dilation, DIL_RES_PER_STEP)
    o, lse = pl.pallas_call(
        functools.partial(_dil_kernel, dilation=dilation, slopes=slopes, scale=DIL_HEAD_DIM ** -0.5),
        grid=(batch, dilation // res),
        in_specs=[pl.BlockSpec((None, n, res * 3 * w), lambda b, r: (b, 0, r))],
        out_specs=[pl.BlockSpec((None, n, res * w), lambda b, r: (b, 0, r)),
                   pl.BlockSpec((None, n, res * LANES), lambda b, r: (b, 0, r))],
        out_shape=[jax.ShapeDtypeStruct((batch, n, dilation * w), F32),
                   jax.ShapeDtypeStruct((batch, n, dilation * LANES), F32)],
        compiler_params=_cparams(2),
        name=f"dilated_attn_g{g}",
    )(qkv_v)
    return o.reshape(batch * n, dilation * w), lse.reshape(batch * n, dilation * LANES)


def _merge_kernel(omla_ref, o0_ref, o1_ref, o2_ref, l0_ref, l1_ref, l2_ref, sgm_ref, sgd_ref,
                  wm_ref, wd_ref, out_ref, od_ref, otok_ref, ltok_ref):
    tm = omla_ref.shape[0]
    hpg = DIL_HEADS_PER_GROUP

    @pl.when(pl.program_id(1) == 0)
    def _():
        for g, (o_ref, l_ref) in enumerate(((o0_ref, l0_ref), (o1_ref, l1_ref), (o2_ref, l2_ref))):
            dil = DIL_GROUPS[g][1]
            for r in range(dil):
                rows = pl.ds(r, tm // dil, stride=dil)
                ltok_ref[g, rows, :] = l_ref[:, r * LANES:(r + 1) * LANES]
                for j in range(hpg):
                    c0 = (r * hpg + j) * DIL_HEAD_DIM
                    otok_ref[g * hpg + j, rows, :] = o_ref[:, c0:c0 + DIL_HEAD_DIM]
        lses = [ltok_ref[g] for g in range(len(DIL_GROUPS))]
        m = jnp.maximum(jnp.maximum(lses[0], lses[1]), lses[2])
        es = [jnp.exp(l - m) for l in lses]
        den = es[0] + es[1] + es[2]
        for g in range(len(DIL_GROUPS)):
            wg = es[g] / den
            for j in range(hpg):
                c0 = (g * hpg + j) * DIL_HEAD_DIM
                od_ref[:, c0:c0 + DIL_HEAD_DIM] = (otok_ref[g * hpg + j] * wg[:, j:j + 1]).astype(BF16)

    a = jnp.dot(omla_ref[...], wm_ref[...], preferred_element_type=F32)
    b = jnp.dot(od_ref[...], wd_ref[...], preferred_element_type=F32)
    out_ref[...] = (sgm_ref[...] * a + sgd_ref[...] * b).astype(out_ref.dtype)


def _merge(o_mla, o_dil, lse_dil, gates, w_o_mla, w_o_dil, d_model):
    n = o_mla.shape[0]
    tm = min(TM_MERGE, n)
    tn = min(1024, d_model)
    nj = d_model // tn
    row = lambda width: pl.BlockSpec((tm, width), lambda i, j: (i, 0))
    dils = [d for _, d in DIL_GROUPS]
    dil_row = lambda width: [pl.BlockSpec((tm // d, d * width), lambda i, j: (i, 0)) for d in dils]
    return pl.pallas_call(
        _merge_kernel,
        grid=(n // tm, nj),
        in_specs=[row(o_mla.shape[1])] + dil_row(DIL_GROUP_WIDTH) + dil_row(LANES) + [
            pl.BlockSpec((tm, tn), lambda i, j: (i, j)),
            pl.BlockSpec((tm, tn), lambda i, j: (i, j + nj)),
            pl.BlockSpec((w_o_mla.shape[0], tn), lambda i, j: (0, j)),
            pl.BlockSpec((w_o_dil.shape[0], tn), lambda i, j: (0, j))],
        out_specs=pl.BlockSpec((tm, tn), lambda i, j: (i, j)),
        out_shape=jax.ShapeDtypeStruct((n, d_model), BF16),
        scratch_shapes=[pltpu.VMEM((tm, DIL_WIDTH), BF16),
                        pltpu.VMEM((DIL_HEADS, tm, DIL_HEAD_DIM), F32),
                        pltpu.VMEM((len(DIL_GROUPS), tm, LANES), F32)],
        compiler_params=_cparams(2),
        name="branch_merge",
    )(o_mla, *o_dil, *lse_dil, gates, gates, w_o_mla, w_o_dil)


def _layer_norm(y, g, b):
    mu = jnp.mean(y, axis=-1, keepdims=True)
    yc = y - mu
    var = jnp.mean(yc * yc, axis=-1, keepdims=True)
    return yc * lax.rsqrt(var + LN_EPS) * g + b


def _pack_bf16_pairs(x):
    half = x.shape[1] // 2
    xb = x.astype(BF16).astype(F32)
    lo = lax.bitcast_convert_type(xb[:, :half], jnp.uint32) >> 16
    hi = lax.bitcast_convert_type(xb[:, half:], jnp.uint32) & jnp.uint32(0xFFFF0000)
    return lo | hi


def _unpack_bf16_pairs(w):
    lo = lax.bitcast_convert_type(w << 16, F32).astype(BF16)
    hi = lax.bitcast_convert_type(w & jnp.uint32(0xFFFF0000), F32).astype(BF16)
    return lo, hi


def _ln1_router_kernel(mg_ref, x_ref, wo_ref, g_ref, b_ref, wrh_ref, wrl_ref, br_ref,
                       x1_ref, xpk_ref, idx_ref, gate_ref, rank_ref, cnt_ref, tri_ref, run_ref, *, alpha):
    sub = tri_ref.shape[0]

    @pl.when(pl.program_id(0) == 0)
    def _():
        r = lax.broadcasted_iota(jnp.int32, (sub, sub), 0)
        c = lax.broadcasted_iota(jnp.int32, (sub, sub), 1)
        tri_ref[...] = jnp.where(c < r, 1.0, 0.0).astype(BF16)
        run_ref[...] = jnp.zeros(run_ref.shape, F32)

    lane = lax.broadcasted_iota(jnp.int32, (sub, LANES), 1).astype(F32)
    for s in range(x_ref.shape[0] // sub):
        rs = slice(s * sub, (s + 1) * sub)
        out = jnp.dot(mg_ref[rs, :], wo_ref[...], preferred_element_type=F32)
        x1 = _layer_norm(alpha * x_ref[rs, :] + out, g_ref[...], b_ref[...])
        x1_ref[rs, :] = x1
        xpk_ref[rs, :] = _pack_bf16_pairs(x1)
        xh = x1.astype(BF16)
        xl = (x1 - xh.astype(F32)).astype(BF16)
        logits = (jnp.dot(xh, wrh_ref[...], preferred_element_type=F32)
                  + jnp.dot(xl, wrh_ref[...], preferred_element_type=F32)
                  + jnp.dot(xh, wrl_ref[...], preferred_element_type=F32)) + br_ref[...]
        vals, idxs = [], []
        for _ in range(TOP_K):
            m = jnp.max(logits, axis=-1, keepdims=True)
            idx = jnp.min(jnp.where(logits == m, lane, float(LANES)), axis=-1, keepdims=True)
            vals.append(m)
            idxs.append(idx)
            logits = jnp.where(lane == idx, -jnp.inf, logits)
        es = [jnp.exp(v - vals[0]) for v in vals]
        den = es[0] + es[1] + es[2] + es[3]
        onehot = jnp.zeros(lane.shape, F32)
        for k in range(TOP_K):
            onehot = onehot + jnp.where(lane == idxs[k], 1.0, 0.0)
        before = run_ref[...] + jnp.dot(tri_ref[...], onehot.astype(BF16), preferred_element_type=F32)
        run_ref[...] = run_ref[...] + jnp.sum(onehot, axis=0, keepdims=True)
        idx_tile = jnp.zeros(lane.shape, F32)
        gate_tile = jnp.zeros(lane.shape, F32)
        rank_tile = jnp.zeros(lane.shape, F32)
        for k in range(TOP_K):
            rank = jnp.sum(jnp.where(lane == idxs[k], before, 0.0), axis=-1, keepdims=True)
            idx_tile = jnp.where(lane == float(k), idxs[k], idx_tile)
            gate_tile = jnp.where(lane == float(k), es[k] / den, gate_tile)
            rank_tile = jnp.where(lane == float(k), rank, rank_tile)
        idx_ref[rs, :] = idx_tile.astype(jnp.int32)
        gate_ref[rs, :] = gate_tile
        rank_ref[rs, :] = rank_tile.astype(jnp.int32)
    cnt_ref[...] = run_ref[...].astype(jnp.int32)


def _ln1_router(merged, x, w_out, ln_g, ln_b, w_rh, w_rl, b_r, alpha):
    n, d = x.shape
    tm = min(TM_LN, n)
    full = lambda a: pl.BlockSpec(a.shape, lambda i: (0,) * a.ndim)
    row = lambda width: pl.BlockSpec((tm, width), lambda i: (i, 0))
    return pl.pallas_call(
        functools.partial(_ln1_router_kernel, alpha=alpha),
        grid=(n // tm,),
        in_specs=[row(d), row(d), full(w_out), full(ln_g), full(ln_b), full(w_rh), full(w_rl), full(b_r)],
        out_specs=[row(d), row(d // 2), row(LANES), row(LANES), row(LANES),
                   pl.BlockSpec((1, LANES), lambda i: (0, 0))],
        out_shape=[jax.ShapeDtypeStruct((n, d), F32),
                   jax.ShapeDtypeStruct((n, d // 2), jnp.uint32),
                   jax.ShapeDtypeStruct((n, LANES), jnp.int32),
                   jax.ShapeDtypeStruct((n, LANES), F32),
                   jax.ShapeDtypeStruct((n, LANES), jnp.int32),
                   jax.ShapeDtypeStruct((1, LANES), jnp.int32)],
        scratch_shapes=[pltpu.VMEM((min(LN_SUB, tm),) * 2, BF16), pltpu.VMEM((1, LANES), F32)],
        compiler_params=_cparams(1),
        name="out_proj_ln1_router",
    )(merged, x, w_out, ln_g, ln_b, w_rh, w_rl, b_r)


def _scatter_kernel(dest_ref, zrow_ref, xpk_ref, xs_hbm, zbuf, sem, zsem):
    t = pl.program_id(0)
    rows = TM_SCATTER

    @pl.when(t == 0)
    def _():
        zbuf[...] = jnp.zeros(zbuf.shape, zbuf.dtype)

        def zero_copy(i):
            row = pl.multiple_of(jnp.maximum(zrow_ref[i], 0), MOE_SUB)
            return pltpu.make_async_copy(zbuf, xs_hbm.at[pl.ds(row, MOE_SUB)], zsem)

        def zstart(i, carry):
            @pl.when(zrow_ref[i] >= 0)
            def _():
                zero_copy(i).start()
            return carry

        def zwait(i, carry):
            @pl.when(zrow_ref[i] >= 0)
            def _():
                zero_copy(i).wait()
            return carry

        lax.fori_loop(0, zrow_ref.shape[0], zstart, 0)
        lax.fori_loop(0, zrow_ref.shape[0], zwait, 0)

    def issue(r8, carry):
        for j in range(SUBLANES):
            for k in range(TOP_K):
                dst = dest_ref[(t * rows + r8 * SUBLANES + j) * TOP_K + k]
                pltpu.make_async_copy(xpk_ref.at[r8, pl.ds(j, 1)], xs_hbm.at[pl.ds(dst, 1)], sem).start()
        return carry

    groups = rows // SUBLANES
    lax.fori_loop(0, groups, issue, 0)
    pltpu.make_async_copy(xs_hbm.at[pl.ds(0, TOP_K * rows)], xs_hbm.at[pl.ds(0, TOP_K * rows)], sem).wait()


def _moe_scatter(xpk, dest, zrows, n_tiles):
    n, w = xpk.shape
    sub = SUBLANES
    xs = pl.pallas_call(
        _scatter_kernel,
        grid_spec=pltpu.PrefetchScalarGridSpec(
            num_scalar_prefetch=2,
            grid=(n // TM_SCATTER,),
            in_specs=[pl.BlockSpec((TM_SCATTER // sub, sub, w), lambda t, dest, zrows: (t, 0, 0))],
            out_specs=pl.BlockSpec(memory_space=pl.ANY),
            scratch_shapes=[pltpu.VMEM((MOE_SUB, w), xpk.dtype), pltpu.SemaphoreType.DMA,
                            pltpu.SemaphoreType.DMA]),
        out_shape=jax.ShapeDtypeStruct((n_tiles * TM_MOE, w), xpk.dtype),
        compiler_params=_cparams(1),
        name="moe_scatter",
    )(dest, zrows, xpk.reshape(n // sub, sub, w))
    return xs


def _new_expert(te_ref, t):
    return (t == 0) | (te_ref[t] != te_ref[jnp.maximum(t - 1, 0)])


def _for_sub_blocks(skip, n_rows, compute, fill):
    pair = 2 * MOE_SUB
    for s in range(n_rows // pair):
        r0 = s * pair
        both, first, second = slice(r0, r0 + pair), slice(r0, r0 + MOE_SUB), slice(r0 + MOE_SUB, r0 + pair)
        pl.when(skip < r0 + MOE_SUB)(functools.partial(compute, both))

        @pl.when((skip >= r0 + MOE_SUB) & (skip < r0 + pair))
        def _(first=first, second=second):
            fill(first)
            compute(second)

        pl.when(skip >= r0 + pair)(functools.partial(fill, both))


def _ffn_a_kernel(te_ref, nt_ref, skip_ref, xs_ref, w_ref, b_ref, a_ref, wb_ref):
    t = pl.program_id(1)

    @pl.when(_new_expert(te_ref, t))
    def _():
        wb_ref[...] = w_ref[...].astype(BF16)

    def compute(rs):
        x = jnp.concatenate(_unpack_bf16_pairs(xs_ref[rs, :]), axis=1)
        h = jnp.dot(x, wb_ref[...], preferred_element_type=F32) + b_ref[...]
        even = lax.broadcasted_iota(jnp.int32, (h.shape[0], LANES), 1) % 2 == 0
        for g in range(h.shape[1] // (2 * LANES)):
            h0 = h[:, 2 * g * LANES:(2 * g + 1) * LANES]
            h1 = h[:, (2 * g + 1) * LANES:(2 * g + 2) * LANES]
            glu = jnp.where(even, h0, pltpu.roll(h1, 1, 1))
            lin = jnp.where(even, pltpu.roll(h0, LANES - 1, 1), h1)
            x_glu = jnp.minimum(glu, SWIGLU_LIMIT)
            x_lin = jnp.clip(lin, -SWIGLU_LIMIT, SWIGLU_LIMIT)
            a_ref[rs, g * LANES:(g + 1) * LANES] = (
                x_glu * _sigmoid(SWIGLU_ALPHA * x_glu) * (x_lin + 1.0)).astype(a_ref.dtype)

    def fill(rs):
        a_ref[rs, :] = jnp.zeros((rs.stop - rs.start, a_ref.shape[1]), a_ref.dtype)

    _for_sub_blocks(skip_ref[t], a_ref.shape[0], compute, fill)


def _ffn_b_kernel(te_ref, nt_ref, skip_ref, a_ref, w_ref, b_ref, o_ref, perm_ref, wb_ref):
    t = pl.program_id(1)

    @pl.when(_new_expert(te_ref, t))
    def _():
        for c in range(w_ref.shape[1] // LANES):
            cs = slice(c * LANES, (c + 1) * LANES)
            for g in range(w_ref.shape[0] // LANES):
                for p in range(2):
                    src = g * LANES + p * PAIR
                    perm_ref[c, pl.ds(g * LANES + p, PAIR, stride=2), :] = w_ref[src:src + PAIR, cs]
            wb_ref[:, cs] = perm_ref[c].astype(BF16)

    def compute(rs):
        o_ref[rs, :] = jnp.dot(a_ref[rs, :], wb_ref[...], preferred_element_type=F32) + b_ref[...]

    def fill(rs):
        o_ref[rs, :] = jnp.zeros((rs.stop - rs.start, o_ref.shape[1]), o_ref.dtype)

    _for_sub_blocks(skip_ref[t], o_ref.shape[0], compute, fill)


def _moe_ffn(xs, tile_expert, n_valid, tile_skip, w1, b1, w2, b2):
    rows = xs.shape[0]
    n_tiles = rows // TM_MOE
    _, d, cols = w1.shape
    d_ff = cols // 2
    wc = min(W1_COLS, cols)
    tile = lambda t, nt: jnp.minimum(t, nt[0] - 1)
    a = pl.pallas_call(
        _ffn_a_kernel,
        grid_spec=pltpu.PrefetchScalarGridSpec(
            num_scalar_prefetch=3,
            grid=(cols // wc, n_tiles),
            in_specs=[pl.BlockSpec((TM_MOE, d // 2), lambda f, t, te, nt, tr: (tile(t, nt), 0)),
                      pl.BlockSpec((None, d, wc), lambda f, t, te, nt, tr: (te[t], 0, f)),
                      pl.BlockSpec((None, 1, wc), lambda f, t, te, nt, tr: (te[t], 0, f))],
            out_specs=pl.BlockSpec((TM_MOE, wc // 2), lambda f, t, te, nt, tr: (t, f)),
            scratch_shapes=[pltpu.VMEM((d, wc), BF16)]),
        out_shape=jax.ShapeDtypeStruct((rows, d_ff), BF16),
        compiler_params=_cparams(2),
        name="moe_ffn_up",
    )(tile_expert, n_valid, tile_skip, xs, w1, b1)
    nc = min(W2_COLS, d)
    return pl.pallas_call(
        _ffn_b_kernel,
        grid_spec=pltpu.PrefetchScalarGridSpec(
            num_scalar_prefetch=3,
            grid=(d // nc, n_tiles),
            in_specs=[pl.BlockSpec((TM_MOE, d_ff), lambda c, t, te, nt, tr: (tile(t, nt), 0)),
                      pl.BlockSpec((None, d_ff, nc), lambda c, t, te, nt, tr: (te[t], 0, c)),
                      pl.BlockSpec((None, 1, nc), lambda c, t, te, nt, tr: (te[t], 0, c))],
            out_specs=pl.BlockSpec((TM_MOE, nc), lambda c, t, te, nt, tr: (t, c)),
            scratch_shapes=[pltpu.VMEM((nc // LANES, d_ff, LANES), F32), pltpu.VMEM((d_ff, nc), BF16)]),
        out_shape=jax.ShapeDtypeStruct((rows, d), F32),
        compiler_params=_cparams(2),
        name="moe_ffn_down",
    )(tile_expert, n_valid, tile_skip, a, w2, b2)


def _combine_kernel(pos_ref, y_hbm, x_ref, gate_ref, g_ref, b_ref, o_ref, buf, sem, *, alpha):
    t = pl.program_id(0)
    rows = x_ref.shape[0]
    groups = rows // SUBLANES
    slot = t % 2

    def issue(step, dst_slot):
        def body(r8, carry):
            for j in range(SUBLANES):
                for k in range(TOP_K):
                    p = pos_ref[(step * rows + r8 * SUBLANES + j) * TOP_K + k]
                    pltpu.make_async_copy(y_hbm.at[pl.ds(p, 1)],
                                          buf.at[dst_slot, k * groups + r8, pl.ds(j, 1)],
                                          sem.at[dst_slot]).start()
            return carry

        lax.fori_loop(0, groups, body, 0)

    @pl.when(t == 0)
    def _():
        issue(0, 0)

    @pl.when(t + 1 < pl.num_programs(0))
    def _():
        issue(t + 1, 1 - slot)

    pltpu.make_async_copy(y_hbm.at[pl.ds(0, TOP_K * rows)], y_hbm.at[pl.ds(0, TOP_K * rows)],
                          sem.at[slot]).wait()
    gate = gate_ref[...]
    ffn = None
    for k in range(TOP_K):
        yk = buf[slot, k * groups:(k + 1) * groups].reshape(rows, buf.shape[-1]) * gate[:, k:k + 1]
        ffn = yk if ffn is None else ffn + yk
    o_ref[...] = _layer_norm(alpha * x_ref[...] + ffn, g_ref[...], b_ref[...])


def _moe_combine(y_sorted, pos, x1, gate, ln_g, ln_b, alpha):
    n, d = x1.shape
    tm = min(TM_COMBINE, n)
    return pl.pallas_call(
        functools.partial(_combine_kernel, alpha=alpha),
        grid_spec=pltpu.PrefetchScalarGridSpec(
            num_scalar_prefetch=1,
            grid=(n // tm,),
            in_specs=[pl.BlockSpec(memory_space=pl.ANY),
                      pl.BlockSpec((tm, d), lambda t, pos: (t, 0)),
                      pl.BlockSpec((tm, LANES), lambda t, pos: (t, 0)),
                      pl.BlockSpec((1, d), lambda t, pos: (0, 0)),
                      pl.BlockSpec((1, d), lambda t, pos: (0, 0))],
            out_specs=pl.BlockSpec((tm, d), lambda t, pos: (t, 0)),
            scratch_shapes=[pltpu.VMEM((2, TOP_K * tm // SUBLANES, SUBLANES, d), F32),
                            pltpu.SemaphoreType.DMA((2,))]),
        out_shape=jax.ShapeDtypeStruct((n, d), F32),
        compiler_params=_cparams(1),
        name="moe_combine_ln2",
    )(pos, y_sorted, x1, gate, ln_g, ln_b)


def _route(top_idx, rank, counts, n):
    nk = n * TOP_K
    n_tiles = -(-nk // TM_MOE) + N_EXPERTS
    padded = (counts + TM_MOE - 1) // TM_MOE * TM_MOE
    pad_end = jnp.cumsum(padded)
    pad_start = pad_end - padded
    lead = padded - counts
    onehot = top_idx[:, :, None] == jnp.arange(N_EXPERTS, dtype=jnp.int32)
    dest = (rank + jnp.sum(jnp.where(onehot, pad_start + lead, 0), axis=-1)).reshape(nk)
    n_valid = (pad_end[-1] // TM_MOE).astype(jnp.int32)
    tiles = jnp.arange(n_tiles, dtype=jnp.int32)
    tile_start = jnp.minimum(tiles, n_valid - 1) * TM_MOE
    tile_expert = jnp.minimum(jnp.sum(pad_end[None, :] <= tile_start[:, None], axis=1),
                              N_EXPERTS - 1).astype(jnp.int32)
    skip_in = lead[tile_expert] - (tiles * TM_MOE - pad_start[tile_expert])
    tile_skip = jnp.where(tiles < n_valid, jnp.clip(skip_in, 0, TM_MOE), TM_MOE).astype(jnp.int32)
    per_tile = TM_MOE // MOE_SUB
    subs = jnp.arange(n_tiles * per_tile, dtype=jnp.int32)
    covered = (subs % per_tile) * MOE_SUB >= tile_skip[subs // per_tile]
    zrows = jnp.where(covered, -1, subs * MOE_SUB).astype(jnp.int32)
    return dest.astype(jnp.int32), zrows, tile_expert, n_valid.reshape(1), tile_skip, n_tiles


def _rope_tables(seq):
    half = MLA_ROPE_DIM // 2
    inv = ROPE_THETA ** (-jnp.arange(half, dtype=F32) / half)
    ang = jnp.arange(seq).astype(F32)[:, None] * inv[None, :]
    cos, sin = jnp.cos(ang), jnp.sin(ang)
    z = jnp.zeros((seq, LANES - MLA_ROPE_DIM), F32)
    zh = jnp.zeros((seq, half), F32)
    return (jnp.concatenate([cos, cos, z], axis=1),
            jnp.concatenate([-sin, zh, z], axis=1),
            jnp.concatenate([zh, sin, z], axis=1))


def _pad_cols(w, width):
    return jnp.pad(w, ((0, 0), (0, width - w.shape[1])))


def kernel(x, w_in, q_norm_g, kv_norm_g, w_uq, w_ukv, w_o_mla, w_o_dil, w_out, ln1_g, ln1_b,
           w_router, b_router, w1, b1, w2, b2, ln2_g, ln2_b):
    batch, seq, d_model = x.shape
    depth = w_in.shape[0]
    n = batch * seq
    alpha = (2.0 * depth) ** 0.25
    tabs = _rope_tables(seq)
    o_q, o_kv, o_pe = 0, MLA_Q_RANK, MLA_Q_RANK + MLA_KV_RANK
    o_dil = o_pe + MLA_ROPE_DIM
    o_gate = o_dil + 3 * DIL_WIDTH
    xt = x.reshape(n, d_model)
    for l in range(depth):
        wi = w_in[l]
        w_a = jnp.concatenate([wi[:, o_kv:o_pe], _pad_cols(wi[:, o_pe:o_dil], 2 * LANES),
                               wi[:, o_q:o_kv]], axis=1).astype(BF16)
        w_b = jnp.concatenate(
            [wi[:, o_dil + which * DIL_WIDTH + g * DIL_GROUP_WIDTH:][:, :DIL_GROUP_WIDTH]
             for g in range(len(DIL_GROUPS)) for which in range(3)], axis=1).astype(BF16)
        w_c = wi[:, o_gate:].astype(BF16)
        wq = w_uq[l].reshape(MLA_Q_RANK, MLA_HEADS, MLA_QK_DIM)
        wq = jnp.pad(wq, ((0, 0), (0, 0), (0, MLA_QK_PAD - MLA_QK_DIM)))
        wq = wq.reshape(MLA_Q_RANK, MLA_HEADS * MLA_QK_PAD).astype(BF16)

        proj_a = _proj(xt, w_a, F32, w_a.shape[1], False, "in_proj_latent")
        qkv_d = _proj_dilated(xt, w_b)
        gates = _proj(xt, w_c, F32, 1024, True, "in_proj_gates")

        q, k, v = _mla_up(proj_a, q_norm_g[l][None], kv_norm_g[l][None], wq,
                          w_ukv[l].astype(BF16), tabs, seq)
        o_mla = _mla_attention(q, k, v, batch, seq)
        dil = [_dilated_group(qkv_d[g], g, batch, seq) for g in range(len(DIL_GROUPS))]
        merged = _merge(o_mla, [o for o, _ in dil], [s for _, s in dil], gates,
                        w_o_mla[l].astype(BF16), w_o_dil[l].astype(BF16), d_model)

        w_r = _pad_cols(w_router[l], LANES)
        w_rh = w_r.astype(BF16)
        w_rl = (w_r - w_rh.astype(F32)).astype(BF16)
        b_r = jnp.concatenate([b_router[l], jnp.full((LANES - N_EXPERTS,), NEG_BIG, F32)])[None]
        x1, xpk, top_idx, top_gate, rank, counts = _ln1_router(
            merged, xt, w_out[l].astype(BF16), ln1_g[l][None], ln1_b[l][None], w_rh, w_rl, b_r, alpha)

        dest, zrows, tile_expert, n_valid, tile_skip, n_tiles = _route(
            top_idx[:, :TOP_K], rank[:, :TOP_K], counts[0, :N_EXPERTS], n)
        xs = _moe_scatter(xpk, dest, zrows, n_tiles)
        y_sorted = _moe_ffn(xs, tile_expert, n_valid, tile_skip, w1[l], b1[l][:, None, :], w2[l],
                            b2[l][:, None, :])
        xt = _moe_combine(y_sorted, dest, x1, top_gate, ln2_g[l][None], ln2_b[l][None], alpha)
    return xt.reshape(batch, seq, d_model)
```

```python
import functools

import numpy as np
import jax
import jax.numpy as jnp
from jax import lax
from jax.experimental import pallas as pl
from jax.experimental.pallas import tpu as pltpu

F32 = jnp.float32
BF16 = jnp.bfloat16

MLA_HEADS = 16
MLA_Q_RANK = 768
MLA_KV_RANK = 512
MLA_NOPE_DIM = 128
MLA_ROPE_DIM = 64
MLA_V_DIM = 128
MLA_QK_DIM = MLA_NOPE_DIM + MLA_ROPE_DIM
MLA_QK_PAD = 256
MLA_HEADS_PER_STEP = 2
MLA_LOOKAHEAD = 1
MLA_Q_SCALE =MLA_QK_DIM ** -0.5 * 1.4426950408889634
ROPE_THETA = 10000.0
DIL_GROUPS = ((128, 1), (512, 4), (2048, 16))
DIL_HEADS_PER_GROUP = 4
DIL_HEADS = DIL_HEADS_PER_GROUP * len(DIL_GROUPS)
DIL_HEAD_DIM = 128
DIL_GROUP_WIDTH = DIL_HEADS_PER_GROUP * DIL_HEAD_DIM
DIL_WIDTH = DIL_HEADS * DIL_HEAD_DIM
DIL_BLOCK = 128
DIL_RES_PER_STEP = 4
N_EXPERTS = 32
TOP_K = 4
SWIGLU_LIMIT = 7.0
SWIGLU_ALPHA = 1.702
LN_EPS = 1e-5
RMS_EPS = 1e-6
LANES = 128
SUBLANES = 8
NEG_BIG = -1e30

VMEM_LIMIT = 56 * 1024 * 1024
TM_PROJ = 1024
TM_PROJ_DIL = 512
TM_UP = 512
TM_MERGE = 512
TM_LN = 512
LN_SUB = 512
TM_MOE = 1024
MOE_SUB = 256
TM_SCATTER = 512
TM_COMBINE = 256
W1_COLS = 1024
W2_COLS = 1024
PAIR = LANES // 2


def _cparams(n_axes):
    return pltpu.CompilerParams(dimension_semantics=("arbitrary",) * n_axes,
                                vmem_limit_bytes=VMEM_LIMIT)


def _sigmoid(x):
    return 1.0 / (1.0 + jnp.exp(-x))


def _proj_kernel(x_ref, w_ref, o_ref, xb_ref, *, act):
    @pl.when(pl.program_id(1) == 0)
    def _():
        xb_ref[...] = x_ref[...].astype(BF16)

    acc = jnp.dot(xb_ref[...], w_ref[...], preferred_element_type=F32)
    if act:
        acc = _sigmoid(acc)
    o_ref[...] = acc.astype(o_ref.dtype)


def _proj(x, w, out_dtype, tn, act, name):
    m, k = x.shape
    nc = w.shape[1]
    tm = min(TM_PROJ, m)
    return pl.pallas_call(
        functools.partial(_proj_kernel, act=act),
        grid=(m // tm, nc // tn),
        in_specs=[pl.BlockSpec((tm, k), lambda i, j: (i, 0)),
                  pl.BlockSpec((k, tn), lambda i, j: (0, j))],
        out_specs=pl.BlockSpec((tm, tn), lambda i, j: (i, j)),
        out_shape=jax.ShapeDtypeStruct((m, nc), out_dtype),
        scratch_shapes=[pltpu.VMEM((tm, k), BF16)],
        compiler_params=_cparams(2),
        name=name,
    )(x, w)


def _proj_dil_kernel(x_ref, w_ref, o0_ref, o1_ref, o2_ref, xb_ref, acc_ref):
    j = pl.program_id(1)
    tm = x_ref.shape[0]
    width = w_ref.shape[1]

    @pl.when(j == 0)
    def _():
        xb_ref[...] = x_ref[...].astype(BF16)

    acc = jnp.dot(xb_ref[...], w_ref[...], preferred_element_type=F32)
    n_cb = width // LANES
    for c in range(n_cb):
        acc_ref[c] = acc[:, c * LANES:(c + 1) * LANES]
    for g, o_ref in enumerate((o0_ref, o1_ref, o2_ref)):
        dil = DIL_GROUPS[g][1]

        @pl.when(j == g)
        def _(o_ref=o_ref, dil=dil):
            for r in range(dil):
                for c in range(n_cb):
                    o_ref[:, r * width + c * LANES:r * width + (c + 1) * LANES] = (
                        acc_ref[c, pl.ds(r, tm // dil, stride=dil), :].astype(BF16))


def _proj_dilated(x, w):
    m, k = x.shape
    tm = min(TM_PROJ_DIL, m)
    width = 3 * DIL_GROUP_WIDTH
    n_groups = len(DIL_GROUPS)
    dils = [d for _, d in DIL_GROUPS]
    return pl.pallas_call(
        _proj_dil_kernel,
        grid=(m // tm, n_groups),
        in_specs=[pl.BlockSpec((tm, k), lambda i, j: (i, 0)),
                  pl.BlockSpec((k, width), lambda i, j: (0, j))],
        out_specs=[pl.BlockSpec((tm // d, d * width), lambda i, j: (i, 0)) for d in dils],
        out_shape=[jax.ShapeDtypeStruct((m // d, d * width), BF16) for d in dils],
        scratch_shapes=[pltpu.VMEM((tm, k), BF16), pltpu.VMEM((width // LANES, tm, LANES), F32)],
        compiler_params=_cparams(2),
        name="in_proj_dilated",
    )(x, w)


def _rope_tile(r, cos, sin_lo, sin_hi):
    half = MLA_ROPE_DIM // 2
    return (r * cos + pltpu.roll(r, LANES - half, 1) * sin_lo + pltpu.roll(r, half, 1) * sin_hi)


def _rms(x, g):
    ms = jnp.mean(x * x, axis=-1, keepdims=True)
    return x * lax.rsqrt(ms + RMS_EPS) * g


def _q_up_kernel(cq_ref, g_ref, w_ref, cos_ref, slo_ref, shi_ref, o_ref):
    cqn = _rms(cq_ref[...], g_ref[...]).astype(BF16)
    cos, slo, shi = cos_ref[...], slo_ref[...], shi_ref[...]
    for h in range(MLA_HEADS):
        c0 = h * MLA_QK_PAD
        qh = jnp.dot(cqn, w_ref[:, c0:c0 + MLA_QK_PAD], preferred_element_type=F32) * MLA_Q_SCALE
        o_ref[:, c0:c0 + LANES] = qh[:, :LANES].astype(BF16)
        o_ref[:, c0 + LANES:c0 + MLA_QK_PAD] = _rope_tile(qh[:, LANES:], cos, slo, shi).astype(BF16)


def _kv_up_kernel(ckv_ref, kpe_ref, g_ref, w_ref, cos_ref, slo_ref, shi_ref, k_ref, v_ref):
    ckvn = _rms(ckv_ref[...], g_ref[...]).astype(BF16)
    k_rot = _rope_tile(kpe_ref[...], cos_ref[...], slo_ref[...], shi_ref[...]).astype(BF16)
    lane = lax.broadcasted_iota(jnp.int32, kpe_ref.shape, 1)
    ones_col = jnp.where(lane == 0, 1.0, 0.0).astype(BF16)
    for h in range(MLA_HEADS):
        c0 = h * (MLA_NOPE_DIM + MLA_V_DIM)
        kv = jnp.dot(ckvn, w_ref[:, c0:c0 + MLA_NOPE_DIM + MLA_V_DIM], preferred_element_type=F32)
        k0 = h * MLA_QK_PAD
        k_ref[:, k0:k0 + LANES] = kv[:, :MLA_NOPE_DIM].astype(BF16)
        k_ref[:, k0 + LANES:k0 + MLA_QK_PAD] = k_rot
        v_ref[:, k0:k0 + MLA_V_DIM] = kv[:, MLA_NOPE_DIM:].astype(BF16)
        v_ref[:, k0 + MLA_V_DIM:k0 + MLA_QK_PAD] = ones_col


def _mla_up(proj_a, q_g, kv_g, w_uq, w_ukv, tabs, seq):
    n = proj_a.shape[0]
    tm = min(TM_UP, seq)
    sb = seq // tm
    tab_specs = [pl.BlockSpec((tm, LANES), lambda i: (i % sb, 0))] * 3
    qw = MLA_HEADS * MLA_QK_PAD
    q = pl.pallas_call(
        _q_up_kernel,
        grid=(n // tm,),
        in_specs=[pl.BlockSpec((tm, MLA_Q_RANK), lambda i: (i, 1)),
                  pl.BlockSpec((1, MLA_Q_RANK), lambda i: (0, 0)),
                  pl.BlockSpec((MLA_Q_RANK, qw), lambda i: (0, 0))] + tab_specs,
        out_specs=pl.BlockSpec((tm, qw), lambda i: (i, 0)),
        out_shape=jax.ShapeDtypeStruct((n, qw), BF16),
        compiler_params=_cparams(1),
        name="mla_q_up",
    )(proj_a, q_g, w_uq, *tabs)
    kvw = MLA_HEADS * (MLA_NOPE_DIM + MLA_V_DIM)
    k, v = pl.pallas_call(
        _kv_up_kernel,
        grid=(n // tm,),
        in_specs=[pl.BlockSpec((tm, MLA_KV_RANK), lambda i: (i, 0)),
                  pl.BlockSpec((tm, LANES), lambda i: (i, MLA_KV_RANK // LANES)),
                  pl.BlockSpec((1, MLA_KV_RANK), lambda i: (0, 0)),
                  pl.BlockSpec((MLA_KV_RANK, kvw), lambda i: (0, 0))] + tab_specs,
        out_specs=[pl.BlockSpec((tm, qw), lambda i: (i, 0)),
                   pl.BlockSpec((tm, qw), lambda i: (i, 0))],
        out_shape=[jax.ShapeDtypeStruct((n, qw), BF16),
                   jax.ShapeDtypeStruct((n, qw), BF16)],
        compiler_params=_cparams(1),
        name="mla_kv_up",
    )(proj_a, proj_a, kv_g, w_ukv, *tabs)
    return q, k, v


def _nt_dot(a, b):
    return lax.dot_general(a, b, (((1,), (1,)), ((), ())), preferred_element_type=F32)


def _mla_attn_kernel(q_ref, k_ref, v_ref, o_ref, *, tq):
    seq = q_ref.shape[0]
    row = lax.broadcasted_iota(jnp.int32, (tq, tq), 0)
    col = lax.broadcasted_iota(jnp.int32, (tq, tq), 1)
    causal = col <= row
    def scores(hh, qi):
        cq = slice(hh * MLA_QK_PAD, (hh + 1) * MLA_QK_PAD)
        r0 = qi * tq
        q = q_ref[r0:r0 + tq, cq]
        s_d = jnp.where(causal, _nt_dot(q, k_ref[r0:r0 + tq, cq]), -jnp.inf)
        s_p = _nt_dot(q, k_ref[0:r0, cq]) if qi > 0 else None
        return s_d, s_p

    def finish(hh, qi, s_d, s_p):
        cq = slice(hh * MLA_QK_PAD, (hh + 1) * MLA_QK_PAD)
        r0 = qi * tq
        m = jnp.max(s_d, axis=-1, keepdims=True)
        if qi > 0:
            m = jnp.maximum(m, jnp.max(s_p, axis=-1, keepdims=True))
        acc = jnp.dot(jnp.exp2(s_d - m).astype(BF16), v_ref[r0:r0 + tq, cq], preferred_element_type=F32)
        if qi > 0:
            acc = acc + jnp.dot(jnp.exp2(s_p - m).astype(BF16), v_ref[0:r0, cq],
                                preferred_element_type=F32)
        o_ref[r0:r0 + tq, hh * MLA_V_DIM:(hh + 1) * MLA_V_DIM] = (
            acc[:, :MLA_V_DIM] / acc[:, MLA_V_DIM:MLA_V_DIM + 1]).astype(o_ref.dtype)

    items = [(hh, qi) for hh in range(q_ref.shape[1] // MLA_QK_PAD) for qi in range(seq // tq)]
    pending = [scores(*it) for it in items[:MLA_LOOKAHEAD]]
    for k, cur in enumerate(items):
        if k + MLA_LOOKAHEAD < len(items):
            pending.append(scores(*items[k + MLA_LOOKAHEAD]))
        finish(*cur, *pending.pop(0))


def _mla_attention(q, k, v, batch, seq):
    n = q.shape[0]
    return pl.pallas_call(
        functools.partial(_mla_attn_kernel, tq=min(256, seq)),
        grid=(batch, MLA_HEADS // MLA_HEADS_PER_STEP),
        in_specs=[pl.BlockSpec((seq, MLA_HEADS_PER_STEP * MLA_QK_PAD), lambda b, h: (b, h))] * 3,
        out_specs=pl.BlockSpec((seq, MLA_HEADS_PER_STEP * MLA_V_DIM), lambda b, h: (b, h)),
        out_shape=jax.ShapeDtypeStruct((n, MLA_HEADS * MLA_V_DIM), BF16),
        compiler_params=_cparams(2),
        name="mla_attention",
    )(q, k, v)


def _dil_kernel(qkv_ref, o_ref, lse_ref, *, dilation, slopes, scale):
    n = qkv_ref.shape[0]
    w = DIL_GROUP_WIDTH
    res = qkv_ref.shape[1] // (3 * w)
    blk = DIL_BLOCK
    ri = lax.broadcasted_iota(jnp.int32, (blk, 2 * blk), 0)
    cj = lax.broadcasted_iota(jnp.int32, (blk, 2 * blk), 1) - blk
    steps = ri - cj
    valid2 = (steps >= 0) & (steps <= blk)
    dist2 = (steps * dilation).astype(F32)
    valid1, dist1 = valid2[:, blk:], dist2[:, blk:]
    lane = lax.broadcasted_iota(jnp.int32, (blk, LANES), 1)

    def block(rr, q0, k0, klen, valid, dist):
        heads = range(DIL_HEADS_PER_GROUP)
        c0 = [rr * 3 * w + j * DIL_HEAD_DIM for j in heads]
        s = [_nt_dot(qkv_ref[pl.ds(q0, blk), c0[j]:c0[j] + DIL_HEAD_DIM],
                     qkv_ref[pl.ds(k0, klen), c0[j] + w:c0[j] + w + DIL_HEAD_DIM]) for j in heads]
        s = [jnp.where(valid, s[j] * scale - slopes[j] * dist, -jnp.inf) for j in heads]
        m = [jnp.max(s[j], axis=-1, keepdims=True) for j in heads]
        p = [jnp.exp(s[j] - m[j]) for j in heads]
        l = [jnp.sum(p[j], axis=-1, keepdims=True) for j in heads]
        lse_tile = jnp.zeros((blk, LANES), F32)
        for j in heads:
            vv = qkv_ref[pl.ds(k0, klen), c0[j] + 2 * w:c0[j] + 2 * w + DIL_HEAD_DIM]
            o = jnp.dot(p[j].astype(BF16), vv, preferred_element_type=F32) / l[j]
            o0 = rr * w + j * DIL_HEAD_DIM
            o_ref[pl.ds(q0, blk), o0:o0 + DIL_HEAD_DIM] = o
            lse_tile = jnp.where(lane == j, m[j] + jnp.log(l[j]), lse_tile)
        lse_ref[pl.ds(q0, blk), rr * LANES:(rr + 1) * LANES] = lse_tile

    for rr in range(res):
        block(rr, 0, 0, blk, valid1, dist1)

        def body(i, carry, rr=rr):
            q0 = pl.multiple_of(i * blk, blk)
            k0 = pl.multiple_of(i * blk - blk, blk)
            block(rr, q0, k0, 2 * blk, valid2, dist2)
            return carry

        lax.fori_loop(1, n // blk, body, 0, unroll=3)


def _dilated_group(qkv, g, batch, seq):
    window, dilation = DIL_GROUPS[g]
    assert window // dilation == DIL_BLOCK and seq % (dilation * DIL_BLOCK) == 0
    n = seq // dilation
    w = DIL_GROUP_WIDTH
    qkv_v = qkv.reshape(batch, n, dilation * 3 * w)
    slopes = np.float32(2.0) ** (np.float32(-8.0) * np.arange(1, DIL_HEADS + 1, dtype=np.float32)
                                 / np.float32(DIL_HEADS))
    slopes = tuple(float(s) for s in slopes[g * DIL_HEADS_PER_GROUP:(g + 1) * DIL_HEADS_PER_GROUP])

    res = min(dilation, DIL_RES_PER_STEP)
    o, lse = pl.pallas_call(
        functools.partial(_dil_kernel, dilation=dilation, slopes=slopes, scale=DIL_HEAD_DIM ** -0.5),
        grid=(batch, dilation // res),
        in_specs=[pl.BlockSpec((None, n, res * 3 * w), lambda b, r: (b, 0, r))],
        out_specs=[pl.BlockSpec((None, n, res * w), lambda b, r: (b, 0, r)),
                   pl.BlockSpec((None, n, res * LANES), lambda b, r: (b, 0, r))],
        out_shape=[jax.ShapeDtypeStruct((batch, n, dilation * w), F32),
                   jax.ShapeDtypeStruct((batch, n, dilation * LANES), F32)],
        compiler_params=_cparams(2),
        name=f"dilated_attn_g{g}",
    )(qkv_v)
    return o.reshape(batch * n, dilation * w), lse.reshape(batch * n, dilation * LANES)


def _merge_kernel(omla_ref, o0_ref, o1_ref, o2_ref, l0_ref, l1_ref, l2_ref, sgm_ref, sgd_ref,
                  wm_ref, wd_ref, out_ref, od_ref, otok_ref, ltok_ref):
    tm = omla_ref.shape[0]
    hpg = DIL_HEADS_PER_GROUP

    @pl.when(pl.program_id(1) == 0)
    def _():
        for g, (o_ref, l_ref) in enumerate(((o0_ref, l0_ref), (o1_ref, l1_ref), (o2_ref, l2_ref))):
            dil = DIL_GROUPS[g][1]
            for r in range(dil):
                rows = pl.ds(r, tm // dil, stride=dil)
                ltok_ref[g, rows, :] = l_ref[:, r * LANES:(r + 1) * LANES]
                for j in range(hpg):
                    c0 = (r * hpg + j) * DIL_HEAD_DIM
                    otok_ref[g * hpg + j, rows, :] = o_ref[:, c0:c0 + DIL_HEAD_DIM]
        lses = [ltok_ref[g] for g in range(len(DIL_GROUPS))]
        m = jnp.maximum(jnp.maximum(lses[0], lses[1]), lses[2])
        es = [jnp.exp(l - m) for l in lses]
        den = es[0] + es[1] + es[2]
        for g in range(len(DIL_GROUPS)):
            wg = es[g] / den
            for j in range(hpg):
                c0 = (g * hpg + j) * DIL_HEAD_DIM
                od_ref[:, c0:c0 + DIL_HEAD_DIM] = (otok_ref[g * hpg + j] * wg[:, j:j + 1]).astype(BF16)

    a = jnp.dot(omla_ref[...], wm_ref[...], preferred_element_type=F32)
    b = jnp.dot(od_ref[...], wd_ref[...], preferred_element_type=F32)
    out_ref[...] = (sgm_ref[...] * a + sgd_ref[...] * b).astype(out_ref.dtype)


def _merge(o_mla, o_dil, lse_dil, gates, w_o_mla, w_o_dil, d_model):
    n = o_mla.shape[0]
    tm = min(TM_MERGE, n)
    tn = min(1024, d_model)
    nj = d_model // tn
    row = lambda width: pl.BlockSpec((tm, width), lambda i, j: (i, 0))
    dils = [d for _, d in DIL_GROUPS]
    dil_row = lambda width: [pl.BlockSpec((tm // d, d * width), lambda i, j: (i, 0)) for d in dils]
    return pl.pallas_call(
        _merge_kernel,
        grid=(n // tm, nj),
        in_specs=[row(o_mla.shape[1])] + dil_row(DIL_GROUP_WIDTH) + dil_row(LANES) + [
            pl.BlockSpec((tm, tn), lambda i, j: (i, j)),
            pl.BlockSpec((tm, tn), lambda i, j: (i, j + nj)),
            pl.BlockSpec((w_o_mla.shape[0], tn), lambda i, j: (0, j)),
            pl.BlockSpec((w_o_dil.shape[0], tn), lambda i, j: (0, j))],
        out_specs=pl.BlockSpec((tm, tn), lambda i, j: (i, j)),
        out_shape=jax.ShapeDtypeStruct((n, d_model), BF16),
        scratch_shapes=[pltpu.VMEM((tm, DIL_WIDTH), BF16),
                        pltpu.VMEM((DIL_HEADS, tm, DIL_HEAD_DIM), F32),
                        pltpu.VMEM((len(DIL_GROUPS), tm, LANES), F32)],
        compiler_params=_cparams(2),
        name="branch_merge",
    )(o_mla, *o_dil, *lse_dil, gates, gates, w_o_mla, w_o_dil)


def _layer_norm(y, g, b):
    mu = jnp.mean(y, axis=-1, keepdims=True)
    yc = y - mu
    var = jnp.mean(yc * yc, axis=-1, keepdims=True)
    return yc * lax.rsqrt(var + LN_EPS) * g + b


def _pack_bf16_pairs(x):
    half = x.shape[1] // 2
    xb = x.astype(BF16).astype(F32)
    lo = lax.bitcast_convert_type(xb[:, :half], jnp.uint32) >> 16
    hi = lax.bitcast_convert_type(xb[:, half:], jnp.uint32) & jnp.uint32(0xFFFF0000)
    return lo | hi


def _unpack_bf16_pairs(w):
    lo = lax.bitcast_convert_type(w << 16, F32).astype(BF16)
    hi = lax.bitcast_convert_type(w & jnp.uint32(0xFFFF0000), F32).astype(BF16)
    return lo, hi


def _ln1_router_kernel(mg_ref, x_ref, wo_ref, g_ref, b_ref, wrh_ref, wrl_ref, br_ref,
                       x1_ref, xpk_ref, idx_ref, gate_ref, rank_ref, cnt_ref, tri_ref, run_ref, *, alpha):
    sub = tri_ref.shape[0]

    @pl.when(pl.program_id(0) == 0)
    def _():
        r = lax.broadcasted_iota(jnp.int32, (sub, sub), 0)
        c = lax.broadcasted_iota(jnp.int32, (sub, sub), 1)
        tri_ref[...] = jnp.where(c < r, 1.0, 0.0).astype(BF16)
        run_ref[...] = jnp.zeros(run_ref.shape, F32)

    lane = lax.broadcasted_iota(jnp.int32, (sub, LANES), 1).astype(F32)
    blocks = [slice(s * sub, (s + 1) * sub) for s in range(x_ref.shape[0] // sub)]
    outs = [jnp.dot(mg_ref[rs, :], wo_ref[...], preferred_element_type=F32) for rs in blocks]
    for rs, out in zip(blocks, outs):
        x1 = _layer_norm(alpha * x_ref[rs, :] + out, g_ref[...], b_ref[...])
        x1_ref[rs, :] = x1
        xpk_ref[rs, :] = _pack_bf16_pairs(x1)
        xh = x1.astype(BF16)
        xl = (x1 - xh.astype(F32)).astype(BF16)
        logits = (jnp.dot(xh, wrh_ref[...], preferred_element_type=F32)
                  + jnp.dot(xl, wrh_ref[...], preferred_element_type=F32)
                  + jnp.dot(xh, wrl_ref[...], preferred_element_type=F32)) + br_ref[...]
        vals, idxs = [], []
        for _ in range(TOP_K):
            m = jnp.max(logits, axis=-1, keepdims=True)
            idx = jnp.min(jnp.where(logits == m, lane, float(LANES)), axis=-1, keepdims=True)
            vals.append(m)
            idxs.append(idx)
            logits = jnp.where(lane == idx, -jnp.inf, logits)
        es = [jnp.exp(v - vals[0]) for v in vals]
        den = es[0] + es[1] + es[2] + es[3]
        onehot = jnp.zeros(lane.shape, F32)
        for k in range(TOP_K):
            onehot = onehot + jnp.where(lane == idxs[k], 1.0, 0.0)
        before = run_ref[...] + jnp.dot(tri_ref[...], onehot.astype(BF16), preferred_element_type=F32)
        run_ref[...] = run_ref[...] + jnp.sum(onehot, axis=0, keepdims=True)
        idx_tile = jnp.zeros(lane.shape, F32)
        gate_tile = jnp.zeros(lane.shape, F32)
        rank_tile = jnp.zeros(lane.shape, F32)
        for k in range(TOP_K):
            rank = jnp.sum(jnp.where(lane == idxs[k], before, 0.0), axis=-1, keepdims=True)
            idx_tile = jnp.where(lane == float(k), idxs[k], idx_tile)
            gate_tile = jnp.where(lane == float(k), es[k] / den, gate_tile)
            rank_tile = jnp.where(lane == float(k), rank, rank_tile)
        idx_ref[rs, :] = idx_tile.astype(jnp.int32)
        gate_ref[rs, :] = gate_tile
        rank_ref[rs, :] = rank_tile.astype(jnp.int32)
    cnt_ref[...] = run_ref[...].astype(jnp.int32)


def _ln1_router(merged, x, w_out, ln_g, ln_b, w_rh, w_rl, b_r, alpha):
    n, d = x.shape
    tm = min(TM_LN, n)
    full = lambda a: pl.BlockSpec(a.shape, lambda i: (0,) * a.ndim)
    row = lambda width: pl.BlockSpec((tm, width), lambda i: (i, 0))
    return pl.pallas_call(
        functools.partial(_ln1_router_kernel, alpha=alpha),
        grid=(n // tm,),
        in_specs=[row(d), row(d), full(w_out), full(ln_g), full(ln_b), full(w_rh), full(w_rl), full(b_r)],
        out_specs=[row(d), row(d // 2), row(LANES), row(LANES), row(LANES),
                   pl.BlockSpec((1, LANES), lambda i: (0, 0))],
        out_shape=[jax.ShapeDtypeStruct((n, d), F32),
                   jax.ShapeDtypeStruct((n, d // 2), jnp.uint32),
                   jax.ShapeDtypeStruct((n, LANES), jnp.int32),
                   jax.ShapeDtypeStruct((n, LANES), F32),
                   jax.ShapeDtypeStruct((n, LANES), jnp.int32),
                   jax.ShapeDtypeStruct((1, LANES), jnp.int32)],
        scratch_shapes=[pltpu.VMEM((min(LN_SUB, tm),) * 2, BF16), pltpu.VMEM((1, LANES), F32)],
        compiler_params=_cparams(1),
        name="out_proj_ln1_router",
    )(merged, x, w_out, ln_g, ln_b, w_rh, w_rl, b_r)


def _scatter_kernel(dest_ref, zrow_ref, xpk_ref, xs_hbm, zbuf, sem, zsem):
    t = pl.program_id(0)
    rows = TM_SCATTER

    @pl.when(t == 0)
    def _():
        zbuf[...] = jnp.zeros(zbuf.shape, zbuf.dtype)

        def zero_copy(i):
            row = pl.multiple_of(jnp.maximum(zrow_ref[i], 0), MOE_SUB)
            return pltpu.make_async_copy(zbuf, xs_hbm.at[pl.ds(row, MOE_SUB)], zsem)

        def zstart(i, carry):
            @pl.when(zrow_ref[i] >= 0)
            def _():
                zero_copy(i).start()
            return carry

        def zwait(i, carry):
            @pl.when(zrow_ref[i] >= 0)
            def _():
                zero_copy(i).wait()
            return carry

        lax.fori_loop(0, zrow_ref.shape[0], zstart, 0)
        lax.fori_loop(0, zrow_ref.shape[0], zwait, 0)

    def issue(r8, carry):
        for j in range(SUBLANES):
            for k in range(TOP_K):
                dst = dest_ref[(t * rows + r8 * SUBLANES + j) * TOP_K + k]
                pltpu.make_async_copy(xpk_ref.at[r8, pl.ds(j, 1)], xs_hbm.at[pl.ds(dst, 1)], sem).start()
        return carry

    groups = rows // SUBLANES
    lax.fori_loop(0, groups, issue, 0)
    pltpu.make_async_copy(xs_hbm.at[pl.ds(0, TOP_K * rows)], xs_hbm.at[pl.ds(0, TOP_K * rows)], sem).wait()


def _moe_scatter(xpk, dest, zrows, n_tiles):
    n, w = xpk.shape
    sub = SUBLANES
    xs = pl.pallas_call(
        _scatter_kernel,
        grid_spec=pltpu.PrefetchScalarGridSpec(
            num_scalar_prefetch=2,
            grid=(n // TM_SCATTER,),
            in_specs=[pl.BlockSpec((TM_SCATTER // sub, sub, w), lambda t, dest, zrows: (t, 0, 0))],
            out_specs=pl.BlockSpec(memory_space=pl.ANY),
            scratch_shapes=[pltpu.VMEM((MOE_SUB, w), xpk.dtype), pltpu.SemaphoreType.DMA,
                            pltpu.SemaphoreType.DMA]),
        out_shape=jax.ShapeDtypeStruct((n_tiles * TM_MOE, w), xpk.dtype),
        compiler_params=_cparams(1),
        name="moe_scatter",
    )(dest, zrows, xpk.reshape(n // sub, sub, w))
    return xs


def _new_expert(te_ref, t):
    return (t == 0) | (te_ref[t] != te_ref[jnp.maximum(t - 1, 0)])


def _for_sub_blocks(skip, n_rows, compute, fill):
    pair = 2 * MOE_SUB
    for s in range(n_rows // pair):
        r0 = s * pair
        both, first, second = slice(r0, r0 + pair), slice(r0, r0 + MOE_SUB), slice(r0 + MOE_SUB, r0 + pair)
        pl.when(skip < r0 + MOE_SUB)(functools.partial(compute, both))

        @pl.when((skip >= r0 + MOE_SUB) & (skip < r0 + pair))
        def _(first=first, second=second):
            fill(first)
            compute(second)

        pl.when(skip >= r0 + pair)(functools.partial(fill, both))


def _ffn_a_kernel(te_ref, nt_ref, skip_ref, xs_ref, w_ref, b_ref, a_ref, wb_ref):
    t = pl.program_id(1)

    @pl.when(_new_expert(te_ref, t))
    def _():
        wb_ref[...] = w_ref[...].astype(BF16)

    def compute(rs):
        x = jnp.concatenate(_unpack_bf16_pairs(xs_ref[rs, :]), axis=1)
        even = lax.broadcasted_iota(jnp.int32, (x.shape[0], LANES), 1) % 2 == 0
        pair = 2 * LANES

        def up(g):
            cs = slice(g * pair, (g + 1) * pair)
            return jnp.dot(x, wb_ref[:, cs], preferred_element_type=F32) + b_ref[:, cs]

        def act(g, h):
            h0, h1 = h[:, :LANES], h[:, LANES:]
            glu = jnp.where(even, h0, pltpu.roll(h1, 1, 1))
            lin = jnp.where(even, pltpu.roll(h0, LANES - 1, 1), h1)
            x_glu = jnp.minimum(glu, SWIGLU_LIMIT)
            x_lin = jnp.clip(lin, -SWIGLU_LIMIT, SWIGLU_LIMIT)
            a_ref[rs, g * LANES:(g + 1) * LANES] = (
                x_glu * _sigmoid(SWIGLU_ALPHA * x_glu) * (x_lin + 1.0)).astype(a_ref.dtype)

        n_groups = wb_ref.shape[1] // pair
        h = up(0)
        for g in range(n_groups):
            ahead = up(g + 1) if g + 1 < n_groups else None
            act(g, h)
            h = ahead

    def fill(rs):
        a_ref[rs, :] = jnp.zeros((rs.stop - rs.start, a_ref.shape[1]), a_ref.dtype)

    _for_sub_blocks(skip_ref[t], a_ref.shape[0], compute, fill)


def _ffn_b_kernel(te_ref, nt_ref, skip_ref, a_ref, w_ref, b_ref, o_ref, perm_ref, wb_ref):
    t = pl.program_id(1)

    @pl.when(_new_expert(te_ref, t))
    def _():
        for c in range(w_ref.shape[1] // LANES):
            cs = slice(c * LANES, (c + 1) * LANES)
            for g in range(w_ref.shape[0] // LANES):
                for p in range(2):
                    src = g * LANES + p * PAIR
                    perm_ref[c, pl.ds(g * LANES + p, PAIR, stride=2), :] = w_ref[src:src + PAIR, cs]
            wb_ref[:, cs] = perm_ref[c].astype(BF16)

    def compute(rs):
        o_ref[rs, :] = jnp.dot(a_ref[rs, :], wb_ref[...], preferred_element_type=F32) + b_ref[...]

    def fill(rs):
        o_ref[rs, :] = jnp.zeros((rs.stop - rs.start, o_ref.shape[1]), o_ref.dtype)

    _for_sub_blocks(skip_ref[t], o_ref.shape[0], compute, fill)


def _moe_ffn(xs, tile_expert, n_valid, tile_skip, w1, b1, w2, b2):
    rows = xs.shape[0]
    n_tiles = rows // TM_MOE
    _, d, cols = w1.shape
    d_ff = cols // 2
    wc = min(W1_COLS, cols)
    tile = lambda t, nt: jnp.minimum(t, nt[0] - 1)
    a = pl.pallas_call(
        _ffn_a_kernel,
        grid_spec=pltpu.PrefetchScalarGridSpec(
            num_scalar_prefetch=3,
            grid=(cols // wc, n_tiles),
            in_specs=[pl.BlockSpec((TM_MOE, d // 2), lambda f, t, te, nt, tr: (tile(t, nt), 0)),
                      pl.BlockSpec((None, d, wc), lambda f, t, te, nt, tr: (te[t], 0, f)),
                      pl.BlockSpec((None, 1, wc), lambda f, t, te, nt, tr: (te[t], 0, f))],
            out_specs=pl.BlockSpec((TM_MOE, wc // 2), lambda f, t, te, nt, tr: (t, f)),
            scratch_shapes=[pltpu.VMEM((d, wc), BF16)]),
        out_shape=jax.ShapeDtypeStruct((rows, d_ff), BF16),
        compiler_params=_cparams(2),
        name="moe_ffn_up",
    )(tile_expert, n_valid, tile_skip, xs, w1, b1)
    nc = min(W2_COLS, d)
    return pl.pallas_call(
        _ffn_b_kernel,
        grid_spec=pltpu.PrefetchScalarGridSpec(
            num_scalar_prefetch=3,
            grid=(d // nc, n_tiles),
            in_specs=[pl.BlockSpec((TM_MOE, d_ff), lambda c, t, te, nt, tr: (tile(t, nt), 0)),
                      pl.BlockSpec((None, d_ff, nc), lambda c, t, te, nt, tr: (te[t], 0, c)),
                      pl.BlockSpec((None, 1, nc), lambda c, t, te, nt, tr: (te[t], 0, c))],
            out_specs=pl.BlockSpec((TM_MOE, nc), lambda c, t, te, nt, tr: (t, c)),
            scratch_shapes=[pltpu.VMEM((nc // LANES, d_ff, LANES), F32), pltpu.VMEM((d_ff, nc), BF16)]),
        out_shape=jax.ShapeDtypeStruct((rows, d), F32),
        compiler_params=_cparams(2),
        name="moe_ffn_down",
    )(tile_expert, n_valid, tile_skip, a, w2, b2)


def _combine_kernel(pos_ref, y_hbm, x_ref, gate_ref, g_ref, b_ref, o_ref, buf, sem, *, alpha):
    t = pl.program_id(0)
    rows = x_ref.shape[0]
    groups = rows // SUBLANES
    slot = t % 2

    def issue(step, dst_slot):
        def body(r8, carry):
            for j in range(SUBLANES):
                for k in range(TOP_K):
                    p = pos_ref[(step * rows + r8 * SUBLANES + j) * TOP_K + k]
                    pltpu.make_async_copy(y_hbm.at[pl.ds(p, 1)],
                                          buf.at[dst_slot, k * groups + r8, pl.ds(j, 1)],
                                          sem.at[dst_slot]).start()
            return carry

        lax.fori_loop(0, groups, body, 0)

    @pl.when(t == 0)
    def _():
        issue(0, 0)

    @pl.when(t + 1 < pl.num_programs(0))
    def _():
        issue(t + 1, 1 - slot)

    pltpu.make_async_copy(y_hbm.at[pl.ds(0, TOP_K * rows)], y_hbm.at[pl.ds(0, TOP_K * rows)],
                          sem.at[slot]).wait()
    gate = gate_ref[...]
    ffn = None
    for k in range(TOP_K):
        yk = buf[slot, k * groups:(k + 1) * groups].reshape(rows, buf.shape[-1]) * gate[:, k:k + 1]
        ffn = yk if ffn is None else ffn + yk
    o_ref[...] = _layer_norm(alpha * x_ref[...] + ffn, g_ref[...], b_ref[...])


def _moe_combine(y_sorted, pos, x1, gate, ln_g, ln_b, alpha):
    n, d = x1.shape
    tm = min(TM_COMBINE, n)
    return pl.pallas_call(
        functools.partial(_combine_kernel, alpha=alpha),
        grid_spec=pltpu.PrefetchScalarGridSpec(
            num_scalar_prefetch=1,
            grid=(n // tm,),
            in_specs=[pl.BlockSpec(memory_space=pl.ANY),
                      pl.BlockSpec((tm, d), lambda t, pos: (t, 0)),
                      pl.BlockSpec((tm, LANES), lambda t, pos: (t, 0)),
                      pl.BlockSpec((1, d), lambda t, pos: (0, 0)),
                      pl.BlockSpec((1, d), lambda t, pos: (0, 0))],
            out_specs=pl.BlockSpec((tm, d), lambda t, pos: (t, 0)),
            scratch_shapes=[pltpu.VMEM((2, TOP_K * tm // SUBLANES, SUBLANES, d), F32),
                            pltpu.SemaphoreType.DMA((2,))]),
        out_shape=jax.ShapeDtypeStruct((n, d), F32),
        compiler_params=_cparams(1),
        name="moe_combine_ln2",
    )(pos, y_sorted, x1, gate, ln_g, ln_b)


def _route(top_idx, rank, counts, n):
    nk = n * TOP_K
    n_tiles = -(-nk // TM_MOE) + N_EXPERTS
    padded = (counts + TM_MOE - 1) // TM_MOE * TM_MOE
    pad_end = jnp.cumsum(padded)
    pad_start = pad_end - padded
    lead = padded - counts
    onehot = top_idx[:, :, None] == jnp.arange(N_EXPERTS, dtype=jnp.int32)
    dest = (rank + jnp.sum(jnp.where(onehot, pad_start + lead, 0), axis=-1)).reshape(nk)
    n_valid = (pad_end[-1] // TM_MOE).astype(jnp.int32)
    tiles = jnp.arange(n_tiles, dtype=jnp.int32)
    tile_start = jnp.minimum(tiles, n_valid - 1) * TM_MOE
    tile_expert = jnp.minimum(jnp.sum(pad_end[None, :] <= tile_start[:, None], axis=1),
                              N_EXPERTS - 1).astype(jnp.int32)
    skip_in = lead[tile_expert] - (tiles * TM_MOE - pad_start[tile_expert])
    tile_skip = jnp.where(tiles < n_valid, jnp.clip(skip_in, 0, TM_MOE), TM_MOE).astype(jnp.int32)
    per_tile = TM_MOE // MOE_SUB
    subs = jnp.arange(n_tiles * per_tile, dtype=jnp.int32)
    covered = (subs % per_tile) * MOE_SUB >= tile_skip[subs // per_tile]
    zrows = jnp.where(covered, -1, subs * MOE_SUB).astype(jnp.int32)
    return dest.astype(jnp.int32), zrows, tile_expert, n_valid.reshape(1), tile_skip, n_tiles


def _rope_tables(seq):
    half = MLA_ROPE_DIM // 2
    inv = ROPE_THETA ** (-jnp.arange(half, dtype=F32) / half)
    ang = jnp.arange(seq).astype(F32)[:, None] * inv[None, :]
    cos, sin = jnp.cos(ang), jnp.sin(ang)
    z = jnp.zeros((seq, LANES - MLA_ROPE_DIM), F32)
    zh = jnp.zeros((seq, half), F32)
    return (jnp.concatenate([cos, cos, z], axis=1),
            jnp.concatenate([-sin, zh, z], axis=1),
            jnp.concatenate([zh, sin, z], axis=1))


def _pad_cols(w, width):
    return jnp.pad(w, ((0, 0), (0, width - w.shape[1])))


def kernel(x, w_in, q_norm_g, kv_norm_g, w_uq, w_ukv, w_o_mla, w_o_dil, w_out, ln1_g, ln1_b,
           w_router, b_router, w1, b1, w2, b2, ln2_g, ln2_b):
    batch, seq, d_model = x.shape
    depth = w_in.shape[0]
    n = batch * seq
    alpha = (2.0 * depth) ** 0.25
    tabs = _rope_tables(seq)
    o_q, o_kv, o_pe = 0, MLA_Q_RANK, MLA_Q_RANK + MLA_KV_RANK
    o_dil = o_pe + MLA_ROPE_DIM
    o_gate = o_dil + 3 * DIL_WIDTH
    xt = x.reshape(n, d_model)
    for l in range(depth):
        wi = w_in[l]
        w_a = jnp.concatenate([wi[:, o_kv:o_pe], _pad_cols(wi[:, o_pe:o_dil], 2 * LANES),
                               wi[:, o_q:o_kv]], axis=1).astype(BF16)
        w_b = jnp.concatenate(
            [wi[:, o_dil + which * DIL_WIDTH + g * DIL_GROUP_WIDTH:][:, :DIL_GROUP_WIDTH]
             for g in range(len(DIL_GROUPS)) for which in range(3)], axis=1).astype(BF16)
        w_c = wi[:, o_gate:].astype(BF16)
        wq = w_uq[l].reshape(MLA_Q_RANK, MLA_HEADS, MLA_QK_DIM)
        wq = jnp.pad(wq, ((0, 0), (0, 0), (0, MLA_QK_PAD - MLA_QK_DIM)))
        wq = wq.reshape(MLA_Q_RANK, MLA_HEADS * MLA_QK_PAD).astype(BF16)

        proj_a = _proj(xt, w_a, F32, w_a.shape[1], False, "in_proj_latent")
        qkv_d = _proj_dilated(xt, w_b)
        gates = _proj(xt, w_c, F32, 1024, True, "in_proj_gates")

        q, k, v = _mla_up(proj_a, q_norm_g[l][None], kv_norm_g[l][None], wq,
                          w_ukv[l].astype(BF16), tabs, seq)
        o_mla = _mla_attention(q, k, v, batch, seq)
        dil = [_dilated_group(qkv_d[g], g, batch, seq) for g in range(len(DIL_GROUPS))]
        merged = _merge(o_mla, [o for o, _ in dil], [s for _, s in dil], gates,
                        w_o_mla[l].astype(BF16), w_o_dil[l].astype(BF16), d_model)

        w_r = _pad_cols(w_router[l], LANES)
        w_rh = w_r.astype(BF16)
        w_rl = (w_r - w_rh.astype(F32)).astype(BF16)
        b_r = jnp.concatenate([b_router[l], jnp.full((LANES - N_EXPERTS,), NEG_BIG, F32)])[None]
        x1, xpk, top_idx, top_gate, rank, counts = _ln1_router(
            merged, xt, w_out[l].astype(BF16), ln1_g[l][None], ln1_b[l][None], w_rh, w_rl, b_r, alpha)

        dest, zrows, tile_expert, n_valid, tile_skip, n_tiles = _route(
            top_idx[:, :TOP_K], rank[:, :TOP_K], counts[0, :N_EXPERTS], n)
        xs = _moe_scatter(xpk, dest, zrows, n_tiles)
        y_sorted = _moe_ffn(xs, tile_expert, n_valid, tile_skip, w1[l], b1[l][:, None, :], w2[l],
                            b2[l][:, None, :])
        xt = _moe_combine(y_sorted, dest, x1, top_gate, ln2_g[l][None], ln2_b[l][None], alpha)
    return xt.reshape(batch, seq, d_model)
```

```python
import functools

import numpy as np
import jax
import jax.numpy as jnp
from jax import lax
from jax.experimental import pallas as pl
from jax.experimental.pallas import tpu as pltpu

F32 = jnp.float32
BF16 = jnp.bfloat16

MLA_HEADS = 16
MLA_Q_RANK = 768
MLA_KV_RANK = 512
MLA_NOPE_DIM = 128
MLA_ROPE_DIM = 64
MLA_V_DIM = 128
MLA_QK_DIM = MLA_NOPE_DIM + MLA_ROPE_DIM
MLA_QK_PAD = 256
MLA_HEADS_PER_STEP = 2
MLA_LOOKAHEAD = 1
MLA_Q_SCALE =MLA_QK_DIM ** -0.5 * 1.4426950408889634
ROPE_THETA = 10000.0
DIL_GROUPS = ((128, 1), (512, 4), (2048, 16))
DIL_HEADS_PER_GROUP = 4
DIL_HEADS = DIL_HEADS_PER_GROUP * len(DIL_GROUPS)
DIL_HEAD_DIM = 128
DIL_GROUP_WIDTH = DIL_HEADS_PER_GROUP * DIL_HEAD_DIM
DIL_WIDTH = DIL_HEADS * DIL_HEAD_DIM
DIL_BLOCK = 128
DIL_RES_PER_STEP = 4
N_EXPERTS = 32
TOP_K = 4
SWIGLU_LIMIT = 7.0
SWIGLU_ALPHA = 1.702
LN_EPS = 1e-5
RMS_EPS = 1e-6
LANES = 128
SUBLANES = 8
NEG_BIG = -1e30

VMEM_LIMIT = 56 * 1024 * 1024
TM_PROJ = 1024
TM_PROJ_DIL = 512
TM_UP = 512
TM_MERGE = 256
MERGE_COLS = 2048
TM_LN = 512
LN_SUB = 512
TM_MOE = 1024
MOE_SUB = 256
TM_SCATTER = 512
TM_COMBINE = 256
W1_COLS = 1024
W2_COLS = 1024
PAIR = LANES // 2


def _cparams(n_axes):
    return pltpu.CompilerParams(dimension_semantics=("arbitrary",) * n_axes,
                                vmem_limit_bytes=VMEM_LIMIT)


def _sigmoid(x):
    return 1.0 / (1.0 + jnp.exp(-x))


def _proj_kernel(x_ref, w_ref, o_ref, xb_ref, *, act):
    @pl.when(pl.program_id(1) == 0)
    def _():
        xb_ref[...] = x_ref[...].astype(BF16)

    acc = jnp.dot(xb_ref[...], w_ref[...], preferred_element_type=F32)
    if act:
        acc = _sigmoid(acc)
    o_ref[...] = acc.astype(o_ref.dtype)


def _proj(x, w, out_dtype, tn, act, name):
    m, k = x.shape
    nc = w.shape[1]
    tm = min(TM_PROJ, m)
    return pl.pallas_call(
        functools.partial(_proj_kernel, act=act),
        grid=(m // tm, nc // tn),
        in_specs=[pl.BlockSpec((tm, k), lambda i, j: (i, 0)),
                  pl.BlockSpec((k, tn), lambda i, j: (0, j))],
        out_specs=pl.BlockSpec((tm, tn), lambda i, j: (i, j)),
        out_shape=jax.ShapeDtypeStruct((m, nc), out_dtype),
        scratch_shapes=[pltpu.VMEM((tm, k), BF16)],
        compiler_params=_cparams(2),
        name=name,
    )(x, w)


def _proj_dil_kernel(x_ref, w_ref, o0_ref, o1_ref, o2_ref, xb_ref, acc_ref):
    j = pl.program_id(1)
    tm = x_ref.shape[0]
    width = w_ref.shape[1]

    @pl.when(j == 0)
    def _():
        xb_ref[...] = x_ref[...].astype(BF16)

    acc = jnp.dot(xb_ref[...], w_ref[...], preferred_element_type=F32)
    n_cb = width // LANES
    for c in range(n_cb):
        acc_ref[c] = acc[:, c * LANES:(c + 1) * LANES]
    for g, o_ref in enumerate((o0_ref, o1_ref, o2_ref)):
        dil = DIL_GROUPS[g][1]

        @pl.when(j == g)
        def _(o_ref=o_ref, dil=dil):
            for r in range(dil):
                for c in range(n_cb):
                    o_ref[:, r * width + c * LANES:r * width + (c + 1) * LANES] = (
                        acc_ref[c, pl.ds(r, tm // dil, stride=dil), :].astype(BF16))


def _proj_dilated(x, w):
    m, k = x.shape
    tm = min(TM_PROJ_DIL, m)
    width = 3 * DIL_GROUP_WIDTH
    n_groups = len(DIL_GROUPS)
    dils = [d for _, d in DIL_GROUPS]
    return pl.pallas_call(
        _proj_dil_kernel,
        grid=(m // tm, n_groups),
        in_specs=[pl.BlockSpec((tm, k), lambda i, j: (i, 0)),
                  pl.BlockSpec((k, width), lambda i, j: (0, j))],
        out_specs=[pl.BlockSpec((tm // d, d * width), lambda i, j: (i, 0)) for d in dils],
        out_shape=[jax.ShapeDtypeStruct((m // d, d * width), BF16) for d in dils],
        scratch_shapes=[pltpu.VMEM((tm, k), BF16), pltpu.VMEM((width // LANES, tm, LANES), F32)],
        compiler_params=_cparams(2),
        name="in_proj_dilated",
    )(x, w)


def _rope_tile(r, cos, sin_lo, sin_hi):
    half = MLA_ROPE_DIM // 2
    return (r * cos + pltpu.roll(r, LANES - half, 1) * sin_lo + pltpu.roll(r, half, 1) * sin_hi)


def _rms(x, g):
    ms = jnp.mean(x * x, axis=-1, keepdims=True)
    return x * lax.rsqrt(ms + RMS_EPS) * g


def _q_up_kernel(cq_ref, g_ref, w_ref, cos_ref, slo_ref, shi_ref, o_ref):
    cqn = _rms(cq_ref[...], g_ref[...]).astype(BF16)
    cos, slo, shi = cos_ref[...], slo_ref[...], shi_ref[...]
    for h in range(MLA_HEADS):
        c0 = h * MLA_QK_PAD
        qh = jnp.dot(cqn, w_ref[:, c0:c0 + MLA_QK_PAD], preferred_element_type=F32) * MLA_Q_SCALE
        o_ref[:, c0:c0 + LANES] = qh[:, :LANES].astype(BF16)
        o_ref[:, c0 + LANES:c0 + MLA_QK_PAD] = _rope_tile(qh[:, LANES:], cos, slo, shi).astype(BF16)


def _kv_up_kernel(ckv_ref, kpe_ref, g_ref, w_ref, cos_ref, slo_ref, shi_ref, k_ref, v_ref):
    ckvn = _rms(ckv_ref[...], g_ref[...]).astype(BF16)
    k_rot = _rope_tile(kpe_ref[...], cos_ref[...], slo_ref[...], shi_ref[...]).astype(BF16)
    lane = lax.broadcasted_iota(jnp.int32, kpe_ref.shape, 1)
    ones_col = jnp.where(lane == 0, 1.0, 0.0).astype(BF16)
    for h in range(MLA_HEADS):
        c0 = h * (MLA_NOPE_DIM + MLA_V_DIM)
        kv = jnp.dot(ckvn, w_ref[:, c0:c0 + MLA_NOPE_DIM + MLA_V_DIM], preferred_element_type=F32)
        k0 = h * MLA_QK_PAD
        k_ref[:, k0:k0 + LANES] = kv[:, :MLA_NOPE_DIM].astype(BF16)
        k_ref[:, k0 + LANES:k0 + MLA_QK_PAD] = k_rot
        v_ref[:, k0:k0 + MLA_V_DIM] = kv[:, MLA_NOPE_DIM:].astype(BF16)
        v_ref[:, k0 + MLA_V_DIM:k0 + MLA_QK_PAD] = ones_col


def _mla_up(proj_a, q_g, kv_g, w_uq, w_ukv, tabs, seq):
    n = proj_a.shape[0]
    tm = min(TM_UP, seq)
    sb = seq // tm
    tab_specs = [pl.BlockSpec((tm, LANES), lambda i: (i % sb, 0))] * 3
    qw = MLA_HEADS * MLA_QK_PAD
    q = pl.pallas_call(
        _q_up_kernel,
        grid=(n // tm,),
        in_specs=[pl.BlockSpec((tm, MLA_Q_RANK), lambda i: (i, 1)),
                  pl.BlockSpec((1, MLA_Q_RANK), lambda i: (0, 0)),
                  pl.BlockSpec((MLA_Q_RANK, qw), lambda i: (0, 0))] + tab_specs,
        out_specs=pl.BlockSpec((tm, qw), lambda i: (i, 0)),
        out_shape=jax.ShapeDtypeStruct((n, qw), BF16),
        compiler_params=_cparams(1),
        name="mla_q_up",
    )(proj_a, q_g, w_uq, *tabs)
    kvw = MLA_HEADS * (MLA_NOPE_DIM + MLA_V_DIM)
    k, v = pl.pallas_call(
        _kv_up_kernel,
        grid=(n // tm,),
        in_specs=[pl.BlockSpec((tm, MLA_KV_RANK), lambda i: (i, 0)),
                  pl.BlockSpec((tm, LANES), lambda i: (i, MLA_KV_RANK // LANES)),
                  pl.BlockSpec((1, MLA_KV_RANK), lambda i: (0, 0)),
                  pl.BlockSpec((MLA_KV_RANK, kvw), lambda i: (0, 0))] + tab_specs,
        out_specs=[pl.BlockSpec((tm, qw), lambda i: (i, 0)),
                   pl.BlockSpec((tm, qw), lambda i: (i, 0))],
        out_shape=[jax.ShapeDtypeStruct((n, qw), BF16),
                   jax.ShapeDtypeStruct((n, qw), BF16)],
        compiler_params=_cparams(1),
        name="mla_kv_up",
    )(proj_a, proj_a, kv_g, w_ukv, *tabs)
    return q, k, v


def _nt_dot(a, b):
    return lax.dot_general(a, b, (((1,), (1,)), ((), ())), preferred_element_type=F32)


def _mla_attn_kernel(q_ref, k_ref, v_ref, o_ref, *, tq):
    seq = q_ref.shape[0]
    row = lax.broadcasted_iota(jnp.int32, (tq, tq), 0)
    col = lax.broadcasted_iota(jnp.int32, (tq, tq), 1)
    causal = col <= row
    def scores(hh, qi):
        cq = slice(hh * MLA_QK_PAD, (hh + 1) * MLA_QK_PAD)
        r0 = qi * tq
        q = q_ref[r0:r0 + tq, cq]
        s_d = jnp.where(causal, _nt_dot(q, k_ref[r0:r0 + tq, cq]), -jnp.inf)
        s_p = _nt_dot(q, k_ref[0:r0, cq]) if qi > 0 else None
        return s_d, s_p

    def finish(hh, qi, s_d, s_p):
        cq = slice(hh * MLA_QK_PAD, (hh + 1) * MLA_QK_PAD)
        r0 = qi * tq
        m = jnp.max(s_d, axis=-1, keepdims=True)
        if qi > 0:
            m = jnp.maximum(m, jnp.max(s_p, axis=-1, keepdims=True))
        acc = jnp.dot(jnp.exp2(s_d - m).astype(BF16), v_ref[r0:r0 + tq, cq], preferred_element_type=F32)
        if qi > 0:
            acc = acc + jnp.dot(jnp.exp2(s_p - m).astype(BF16), v_ref[0:r0, cq],
                                preferred_element_type=F32)
        o_ref[r0:r0 + tq, hh * MLA_V_DIM:(hh + 1) * MLA_V_DIM] = (
            acc[:, :MLA_V_DIM] / acc[:, MLA_V_DIM:MLA_V_DIM + 1]).astype(o_ref.dtype)

    items = [(hh, qi) for hh in range(q_ref.shape[1] // MLA_QK_PAD) for qi in range(seq // tq)]
    pending = [scores(*it) for it in items[:MLA_LOOKAHEAD]]
    for k, cur in enumerate(items):
        if k + MLA_LOOKAHEAD < len(items):
            pending.append(scores(*items[k + MLA_LOOKAHEAD]))
        finish(*cur, *pending.pop(0))


def _mla_attention(q, k, v, batch, seq):
    n = q.shape[0]
    return pl.pallas_call(
        functools.partial(_mla_attn_kernel, tq=min(256, seq)),
        grid=(batch, MLA_HEADS // MLA_HEADS_PER_STEP),
        in_specs=[pl.BlockSpec((seq, MLA_HEADS_PER_STEP * MLA_QK_PAD), lambda b, h: (b, h))] * 3,
        out_specs=pl.BlockSpec((seq, MLA_HEADS_PER_STEP * MLA_V_DIM), lambda b, h: (b, h)),
        out_shape=jax.ShapeDtypeStruct((n, MLA_HEADS * MLA_V_DIM), BF16),
        compiler_params=_cparams(2),
        name="mla_attention",
    )(q, k, v)


def _dil_kernel(qkv_ref, o_ref, lse_ref, *, dilation, slopes, scale):
    n = qkv_ref.shape[0]
    w = DIL_GROUP_WIDTH
    res = qkv_ref.shape[1] // (3 * w)
    blk = DIL_BLOCK
    ri = lax.broadcasted_iota(jnp.int32, (blk, 2 * blk), 0)
    cj = lax.broadcasted_iota(jnp.int32, (blk, 2 * blk), 1) - blk
    steps = ri - cj
    valid2 = (steps >= 0) & (steps <= blk)
    dist2 = (steps * dilation).astype(F32)
    valid1, dist1 = valid2[:, blk:], dist2[:, blk:]
    lane = lax.broadcasted_iota(jnp.int32, (blk, LANES), 1)

    def block(rr, q0, k0, klen, valid, dist):
        heads = range(DIL_HEADS_PER_GROUP)
        c0 = [rr * 3 * w + j * DIL_HEAD_DIM for j in heads]
        s = [_nt_dot(qkv_ref[pl.ds(q0, blk), c0[j]:c0[j] + DIL_HEAD_DIM],
                     qkv_ref[pl.ds(k0, klen), c0[j] + w:c0[j] + w + DIL_HEAD_DIM]) for j in heads]
        s = [jnp.where(valid, s[j] * scale - slopes[j] * dist, -jnp.inf) for j in heads]
        m = [jnp.max(s[j], axis=-1, keepdims=True) for j in heads]
        p = [jnp.exp(s[j] - m[j]) for j in heads]
        l = [jnp.sum(p[j], axis=-1, keepdims=True) for j in heads]
        lse_tile = jnp.zeros((blk, LANES), F32)
        for j in heads:
            vv = qkv_ref[pl.ds(k0, klen), c0[j] + 2 * w:c0[j] + 2 * w + DIL_HEAD_DIM]
            o = jnp.dot(p[j].astype(BF16), vv, preferred_element_type=F32) / l[j]
            o0 = rr * w + j * DIL_HEAD_DIM
            o_ref[pl.ds(q0, blk), o0:o0 + DIL_HEAD_DIM] = o
            lse_tile = jnp.where(lane == j, m[j] + jnp.log(l[j]), lse_tile)
        lse_ref[pl.ds(q0, blk), rr * LANES:(rr + 1) * LANES] = lse_tile

    for rr in range(res):
        block(rr, 0, 0, blk, valid1, dist1)

        def body(i, carry, rr=rr):
            q0 = pl.multiple_of(i * blk, blk)
            k0 = pl.multiple_of(i * blk - blk, blk)
            block(rr, q0, k0, 2 * blk, valid2, dist2)
            return carry

        lax.fori_loop(1, n // blk, body, 0, unroll=3)


def _dilated_group(qkv, g, batch, seq):
    window, dilation = DIL_GROUPS[g]
    assert window // dilation == DIL_BLOCK and seq % (dilation * DIL_BLOCK) == 0
    n = seq // dilation
    w = DIL_GROUP_WIDTH
    qkv_v = qkv.reshape(batch, n, dilation * 3 * w)
    slopes = np.float32(2.0) ** (np.float32(-8.0) * np.arange(1, DIL_HEADS + 1, dtype=np.float32)
                                 / np.float32(DIL_HEADS))
    slopes = tuple(float(s) for s in slopes[g * DIL_HEADS_PER_GROUP:(g + 1) * DIL_HEADS_PER_GROUP])

    res = min(dilation, DIL_RES_PER_STEP)
    o, lse = pl.pallas_call(
        functools.partial(_dil_kernel, dilation=dilation, slopes=slopes, scale=DIL_HEAD_DIM ** -0.5),
        grid=(batch, dilation // res),
        in_specs=[pl.BlockSpec((None, n, res * 3 * w), lambda b, r: (b, 0, r))],
        out_specs=[pl.BlockSpec((None, n, res * w), lambda b, r: (b, 0, r)),
                   pl.BlockSpec((None, n, res * LANES), lambda b, r: (b, 0, r))],
        out_shape=[jax.ShapeDtypeStruct((batch, n, dilation * w), F32),
                   jax.ShapeDtypeStruct((batch, n, dilation * LANES), F32)],
        compiler_params=_cparams(2),
        name=f"dilated_attn_g{g}",
    )(qkv_v)
    return o.reshape(batch * n, dilation * w), lse.reshape(batch * n, dilation * LANES)


def _merge_kernel(omla_ref, o0_ref, o1_ref, o2_ref, l0_ref, l1_ref, l2_ref, sgm_ref, sgd_ref,
                  wm_ref, wd_ref, out_ref, od_ref, otok_ref, ltok_ref):
    tm = omla_ref.shape[0]
    hpg = DIL_HEADS_PER_GROUP

    @pl.when(pl.program_id(1) == 0)
    def _():
        for g, (o_ref, l_ref) in enumerate(((o0_ref, l0_ref), (o1_ref, l1_ref), (o2_ref, l2_ref))):
            dil = DIL_GROUPS[g][1]
            for r in range(dil):
                rows = pl.ds(r, tm // dil, stride=dil)
                ltok_ref[g, rows, :] = l_ref[:, r * LANES:(r + 1) * LANES]
                for j in range(hpg):
                    c0 = (r * hpg + j) * DIL_HEAD_DIM
                    otok_ref[g * hpg + j, rows, :] = o_ref[:, c0:c0 + DIL_HEAD_DIM]
        lses = [ltok_ref[g] for g in range(len(DIL_GROUPS))]
        m = jnp.maximum(jnp.maximum(lses[0], lses[1]), lses[2])
        es = [jnp.exp(l - m) for l in lses]
        den = es[0] + es[1] + es[2]
        for g in range(len(DIL_GROUPS)):
            wg = es[g] / den
            for j in range(hpg):
                c0 = (g * hpg + j) * DIL_HEAD_DIM
                od_ref[:, c0:c0 + DIL_HEAD_DIM] = (otok_ref[g * hpg + j] * wg[:, j:j + 1]).astype(BF16)

    a = jnp.dot(omla_ref[...], wm_ref[...], preferred_element_type=F32)
    b = jnp.dot(od_ref[...], wd_ref[...], preferred_element_type=F32)
    out_ref[...] = (sgm_ref[...] * a + sgd_ref[...] * b).astype(out_ref.dtype)


def _merge(o_mla, o_dil, lse_dil, gates, w_o_mla, w_o_dil, d_model):
    n = o_mla.shape[0]
    tm = min(TM_MERGE, n)
    tn = min(MERGE_COLS, d_model)
    nj = d_model // tn
    row = lambda width: pl.BlockSpec((tm, width), lambda i, j: (i, 0))
    dils = [d for _, d in DIL_GROUPS]
    dil_row = lambda width: [pl.BlockSpec((tm // d, d * width), lambda i, j: (i, 0)) for d in dils]
    return pl.pallas_call(
        _merge_kernel,
        grid=(n // tm, nj),
        in_specs=[row(o_mla.shape[1])] + dil_row(DIL_GROUP_WIDTH) + dil_row(LANES) + [
            pl.BlockSpec((tm, tn), lambda i, j: (i, j)),
            pl.BlockSpec((tm, tn), lambda i, j: (i, j + nj)),
            pl.BlockSpec((w_o_mla.shape[0], tn), lambda i, j: (0, j)),
            pl.BlockSpec((w_o_dil.shape[0], tn), lambda i, j: (0, j))],
        out_specs=pl.BlockSpec((tm, tn), lambda i, j: (i, j)),
        out_shape=jax.ShapeDtypeStruct((n, d_model), BF16),
        scratch_shapes=[pltpu.VMEM((tm, DIL_WIDTH), BF16),
                        pltpu.VMEM((DIL_HEADS, tm, DIL_HEAD_DIM), F32),
                        pltpu.VMEM((len(DIL_GROUPS), tm, LANES), F32)],
        compiler_params=_cparams(2),
        name="branch_merge",
    )(o_mla, *o_dil, *lse_dil, gates, gates, w_o_mla, w_o_dil)


def _layer_norm(y, g, b):
    mu = jnp.mean(y, axis=-1, keepdims=True)
    yc = y - mu
    var = jnp.mean(yc * yc, axis=-1, keepdims=True)
    return yc * lax.rsqrt(var + LN_EPS) * g + b


def _pack_bf16_pairs(x):
    half = x.shape[1] // 2
    xb = x.astype(BF16).astype(F32)
    lo = lax.bitcast_convert_type(xb[:, :half], jnp.uint32) >> 16
    hi = lax.bitcast_convert_type(xb[:, half:], jnp.uint32) & jnp.uint32(0xFFFF0000)
    return lo | hi


def _unpack_bf16_pairs(w):
    lo = lax.bitcast_convert_type(w << 16, F32).astype(BF16)
    hi = lax.bitcast_convert_type(w & jnp.uint32(0xFFFF0000), F32).astype(BF16)
    return lo, hi


def _ln1_router_kernel(mg_ref, x_ref, wo_ref, g_ref, b_ref, wrh_ref, wrl_ref, br_ref,
                       x1_ref, xpk_ref, idx_ref, gate_ref, rank_ref, cnt_ref, tri_ref, run_ref, *, alpha):
    sub = tri_ref.shape[0]

    @pl.when(pl.program_id(0) == 0)
    def _():
        r = lax.broadcasted_iota(jnp.int32, (sub, sub), 0)
        c = lax.broadcasted_iota(jnp.int32, (sub, sub), 1)
        tri_ref[...] = jnp.where(c < r, 1.0, 0.0).astype(BF16)
        run_ref[...] = jnp.zeros(run_ref.shape, F32)

    lane = lax.broadcasted_iota(jnp.int32, (sub, LANES), 1).astype(F32)
    blocks = [slice(s * sub, (s + 1) * sub) for s in range(x_ref.shape[0] // sub)]
    outs = [jnp.dot(mg_ref[rs, :], wo_ref[...], preferred_element_type=F32) for rs in blocks]
    for rs, out in zip(blocks, outs):
        x1 = _layer_norm(alpha * x_ref[rs, :] + out, g_ref[...], b_ref[...])
        x1_ref[rs, :] = x1
        xpk_ref[rs, :] = _pack_bf16_pairs(x1)
        xh = x1.astype(BF16)
        xl = (x1 - xh.astype(F32)).astype(BF16)
        logits = (jnp.dot(xh, wrh_ref[...], preferred_element_type=F32)
                  + jnp.dot(xl, wrh_ref[...], preferred_element_type=F32)
                  + jnp.dot(xh, wrl_ref[...], preferred_element_type=F32)) + br_ref[...]
        vals, idxs = [], []
        for _ in range(TOP_K):
            m = jnp.max(logits, axis=-1, keepdims=True)
            idx = jnp.min(jnp.where(logits == m, lane, float(LANES)), axis=-1, keepdims=True)
            vals.append(m)
            idxs.append(idx)
            logits = jnp.where(lane == idx, -jnp.inf, logits)
        es = [jnp.exp(v - vals[0]) for v in vals]
        den = es[0] + es[1] + es[2] + es[3]
        onehot = jnp.zeros(lane.shape, F32)
        for k in range(TOP_K):
            onehot = onehot + jnp.where(lane == idxs[k], 1.0, 0.0)
        before = run_ref[...] + jnp.dot(tri_ref[...], onehot.astype(BF16), preferred_element_type=F32)
        run_ref[...] = run_ref[...] + jnp.sum(onehot, axis=0, keepdims=True)
        idx_tile = jnp.zeros(lane.shape, F32)
        gate_tile = jnp.zeros(lane.shape, F32)
        rank_tile = jnp.zeros(lane.shape, F32)
        for k in range(TOP_K):
            rank = jnp.sum(jnp.where(lane == idxs[k], before, 0.0), axis=-1, keepdims=True)
            idx_tile = jnp.where(lane == float(k), idxs[k], idx_tile)
            gate_tile = jnp.where(lane == float(k), es[k] / den, gate_tile)
            rank_tile = jnp.where(lane == float(k), rank, rank_tile)
        idx_ref[rs, :] = idx_tile.astype(jnp.int32)
        gate_ref[rs, :] = gate_tile
        rank_ref[rs, :] = rank_tile.astype(jnp.int32)
    cnt_ref[...] = run_ref[...].astype(jnp.int32)


def _ln1_router(merged, x, w_out, ln_g, ln_b, w_rh, w_rl, b_r, alpha):
    n, d = x.shape
    tm = min(TM_LN, n)
    full = lambda a: pl.BlockSpec(a.shape, lambda i: (0,) * a.ndim)
    row = lambda width: pl.BlockSpec((tm, width), lambda i: (i, 0))
    return pl.pallas_call(
        functools.partial(_ln1_router_kernel, alpha=alpha),
        grid=(n // tm,),
        in_specs=[row(d), row(d), full(w_out), full(ln_g), full(ln_b), full(w_rh), full(w_rl), full(b_r)],
        out_specs=[row(d), row(d // 2), row(LANES), row(LANES), row(LANES),
                   pl.BlockSpec((1, LANES), lambda i: (0, 0))],
        out_shape=[jax.ShapeDtypeStruct((n, d), F32),
                   jax.ShapeDtypeStruct((n, d // 2), jnp.uint32),
                   jax.ShapeDtypeStruct((n, LANES), jnp.int32),
                   jax.ShapeDtypeStruct((n, LANES), F32),
                   jax.ShapeDtypeStruct((n, LANES), jnp.int32),
                   jax.ShapeDtypeStruct((1, LANES), jnp.int32)],
        scratch_shapes=[pltpu.VMEM((min(LN_SUB, tm),) * 2, BF16), pltpu.VMEM((1, LANES), F32)],
        compiler_params=_cparams(1),
        name="out_proj_ln1_router",
    )(merged, x, w_out, ln_g, ln_b, w_rh, w_rl, b_r)


def _scatter_kernel(dest_ref, zrow_ref, xpk_ref, xs_hbm, zbuf, sem, zsem):
    t = pl.program_id(0)
    rows = TM_SCATTER

    @pl.when(t == 0)
    def _():
        zbuf[...] = jnp.zeros(zbuf.shape, zbuf.dtype)

        def zero_copy(i):
            row = pl.multiple_of(jnp.maximum(zrow_ref[i], 0), MOE_SUB)
            return pltpu.make_async_copy(zbuf, xs_hbm.at[pl.ds(row, MOE_SUB)], zsem)

        def zstart(i, carry):
            @pl.when(zrow_ref[i] >= 0)
            def _():
                zero_copy(i).start()
            return carry

        def zwait(i, carry):
            @pl.when(zrow_ref[i] >= 0)
            def _():
                zero_copy(i).wait()
            return carry

        lax.fori_loop(0, zrow_ref.shape[0], zstart, 0)
        lax.fori_loop(0, zrow_ref.shape[0], zwait, 0)

    def issue(r8, carry):
        for j in range(SUBLANES):
            for k in range(TOP_K):
                dst = dest_ref[(t * rows + r8 * SUBLANES + j) * TOP_K + k]
                pltpu.make_async_copy(xpk_ref.at[r8, pl.ds(j, 1)], xs_hbm.at[pl.ds(dst, 1)], sem).start()
        return carry

    groups = rows // SUBLANES
    lax.fori_loop(0, groups, issue, 0)
    pltpu.make_async_copy(xs_hbm.at[pl.ds(0, TOP_K * rows)], xs_hbm.at[pl.ds(0, TOP_K * rows)], sem).wait()


def _moe_scatter(xpk, dest, zrows, n_tiles):
    n, w = xpk.shape
    sub = SUBLANES
    xs = pl.pallas_call(
        _scatter_kernel,
        grid_spec=pltpu.PrefetchScalarGridSpec(
            num_scalar_prefetch=2,
            grid=(n // TM_SCATTER,),
            in_specs=[pl.BlockSpec((TM_SCATTER // sub, sub, w), lambda t, dest, zrows: (t, 0, 0))],
            out_specs=pl.BlockSpec(memory_space=pl.ANY),
            scratch_shapes=[pltpu.VMEM((MOE_SUB, w), xpk.dtype), pltpu.SemaphoreType.DMA,
                            pltpu.SemaphoreType.DMA]),
        out_shape=jax.ShapeDtypeStruct((n_tiles * TM_MOE, w), xpk.dtype),
        compiler_params=_cparams(1),
        name="moe_scatter",
    )(dest, zrows, xpk.reshape(n // sub, sub, w))
    return xs


def _new_expert(te_ref, t):
    return (t == 0) | (te_ref[t] != te_ref[jnp.maximum(t - 1, 0)])


def _for_sub_blocks(skip, n_rows, matmul, finish, fill):
    pair = 2 * MOE_SUB
    pairs = [slice(r0, r0 + pair) for r0 in range(0, n_rows, pair)]

    @pl.when(skip == 0)
    def _():
        results = [matmul(rs) for rs in pairs]
        for rs, res in zip(pairs, results):
            finish(rs, res)

    for both in pairs:
        r0 = both.start
        first, second = slice(r0, r0 + MOE_SUB), slice(r0 + MOE_SUB, r0 + pair)

        @pl.when((skip > 0) & (skip < r0 + MOE_SUB))
        def _(both=both):
            finish(both, matmul(both))

        @pl.when((skip >= r0 + MOE_SUB) & (skip < r0 + pair))
        def _(first=first, second=second):
            fill(first)
            finish(second, matmul(second))

        pl.when(skip >= r0 + pair)(functools.partial(fill, both))


def _ffn_a_kernel(te_ref, nt_ref, skip_ref, xs_ref, w_ref, b_ref, a_ref, wb_ref):
    t = pl.program_id(1)

    @pl.when(_new_expert(te_ref, t))
    def _():
        wb_ref[...] = w_ref[...].astype(BF16)

    def matmul(rs):
        x = jnp.concatenate(_unpack_bf16_pairs(xs_ref[rs, :]), axis=1)
        return jnp.dot(x, wb_ref[...], preferred_element_type=F32) + b_ref[...]

    def finish(rs, h):
        even = lax.broadcasted_iota(jnp.int32, (h.shape[0], LANES), 1) % 2 == 0
        for g in range(h.shape[1] // (2 * LANES)):
            h0 = h[:, 2 * g * LANES:(2 * g + 1) * LANES]
            h1 = h[:, (2 * g + 1) * LANES:(2 * g + 2) * LANES]
            glu = jnp.where(even, h0, pltpu.roll(h1, 1, 1))
            lin = jnp.where(even, pltpu.roll(h0, LANES - 1, 1), h1)
            x_glu = jnp.minimum(glu, SWIGLU_LIMIT)
            x_lin = jnp.clip(lin, -SWIGLU_LIMIT, SWIGLU_LIMIT)
            a_ref[rs, g * LANES:(g + 1) * LANES] = (
                x_glu * _sigmoid(SWIGLU_ALPHA * x_glu) * (x_lin + 1.0)).astype(a_ref.dtype)

    def fill(rs):
        a_ref[rs, :] = jnp.zeros((rs.stop - rs.start, a_ref.shape[1]), a_ref.dtype)

    _for_sub_blocks(skip_ref[t], a_ref.shape[0], matmul, finish, fill)


def _ffn_b_kernel(te_ref, nt_ref, skip_ref, a_ref, w_ref, b_ref, o_ref, perm_ref, wb_ref):
    t = pl.program_id(1)

    @pl.when(_new_expert(te_ref, t))
    def _():
        for c in range(w_ref.shape[1] // LANES):
            cs = slice(c * LANES, (c + 1) * LANES)
            for g in range(w_ref.shape[0] // LANES):
                for p in range(2):
                    src = g * LANES + p * PAIR
                    perm_ref[c, pl.ds(g * LANES + p, PAIR, stride=2), :] = w_ref[src:src + PAIR, cs]
            wb_ref[:, cs] = perm_ref[c].astype(BF16)

    def matmul(rs):
        return jnp.dot(a_ref[rs, :], wb_ref[...], preferred_element_type=F32) + b_ref[...]

    def finish(rs, res):
        o_ref[rs, :] = res

    def fill(rs):
        o_ref[rs, :] = jnp.zeros((rs.stop - rs.start, o_ref.shape[1]), o_ref.dtype)

    _for_sub_blocks(skip_ref[t], o_ref.shape[0], matmul, finish, fill)


def _moe_ffn(xs, tile_expert, n_valid, tile_skip, w1, b1, w2, b2):
    rows = xs.shape[0]
    n_tiles = rows // TM_MOE
    _, d, cols = w1.shape
    d_ff = cols // 2
    wc = min(W1_COLS, cols)
    tile = lambda t, nt: jnp.minimum(t, nt[0] - 1)
    a = pl.pallas_call(
        _ffn_a_kernel,
        grid_spec=pltpu.PrefetchScalarGridSpec(
            num_scalar_prefetch=3,
            grid=(cols // wc, n_tiles),
            in_specs=[pl.BlockSpec((TM_MOE, d // 2), lambda f, t, te, nt, tr: (tile(t, nt), 0)),
                      pl.BlockSpec((None, d, wc), lambda f, t, te, nt, tr: (te[t], 0, f)),
                      pl.BlockSpec((None, 1, wc), lambda f, t, te, nt, tr: (te[t], 0, f))],
            out_specs=pl.BlockSpec((TM_MOE, wc // 2), lambda f, t, te, nt, tr: (t, f)),
            scratch_shapes=[pltpu.VMEM((d, wc), BF16)]),
        out_shape=jax.ShapeDtypeStruct((rows, d_ff), BF16),
        compiler_params=_cparams(2),
        name="moe_ffn_up",
    )(tile_expert, n_valid, tile_skip, xs, w1, b1)
    nc = min(W2_COLS, d)
    return pl.pallas_call(
        _ffn_b_kernel,
        grid_spec=pltpu.PrefetchScalarGridSpec(
            num_scalar_prefetch=3,
            grid=(d // nc, n_tiles),
            in_specs=[pl.BlockSpec((TM_MOE, d_ff), lambda c, t, te, nt, tr: (tile(t, nt), 0)),
                      pl.BlockSpec((None, d_ff, nc), lambda c, t, te, nt, tr: (te[t], 0, c)),
                      pl.BlockSpec((None, 1, nc), lambda c, t, te, nt, tr: (te[t], 0, c))],
            out_specs=pl.BlockSpec((TM_MOE, nc), lambda c, t, te, nt, tr: (t, c)),
            scratch_shapes=[pltpu.VMEM((nc // LANES, d_ff, LANES), F32), pltpu.VMEM((d_ff, nc), BF16)]),
        out_shape=jax.ShapeDtypeStruct((rows, d), F32),
        compiler_params=_cparams(2),
        name="moe_ffn_down",
    )(tile_expert, n_valid, tile_skip, a, w2, b2)


def _combine_kernel(pos_ref, y_hbm, x_ref, gate_ref, g_ref, b_ref, o_ref, buf, sem, *, alpha):
    t = pl.program_id(0)
    rows = x_ref.shape[0]
    groups = rows // SUBLANES
    slot = t % 2

    def issue(step, dst_slot):
        def body(r8, carry):
            for j in range(SUBLANES):
                for k in range(TOP_K):
                    p = pos_ref[(step * rows + r8 * SUBLANES + j) * TOP_K + k]
                    pltpu.make_async_copy(y_hbm.at[pl.ds(p, 1)],
                                          buf.at[dst_slot, k * groups + r8, pl.ds(j, 1)],
                                          sem.at[dst_slot]).start()
            return carry

        lax.fori_loop(0, groups, body, 0)

    @pl.when(t == 0)
    def _():
        issue(0, 0)

    @pl.when(t + 1 < pl.num_programs(0))
    def _():
        issue(t + 1, 1 - slot)

    pltpu.make_async_copy(y_hbm.at[pl.ds(0, TOP_K * rows)], y_hbm.at[pl.ds(0, TOP_K * rows)],
                          sem.at[slot]).wait()
    gate = gate_ref[...]
    ffn = None
    for k in range(TOP_K):
        yk = buf[slot, k * groups:(k + 1) * groups].reshape(rows, buf.shape[-1]) * gate[:, k:k + 1]
        ffn = yk if ffn is None else ffn + yk
    o_ref[...] = _layer_norm(alpha * x_ref[...] + ffn, g_ref[...], b_ref[...])


def _moe_combine(y_sorted, pos, x1, gate, ln_g, ln_b, alpha):
    n, d = x1.shape
    tm = min(TM_COMBINE, n)
    return pl.pallas_call(
        functools.partial(_combine_kernel, alpha=alpha),
        grid_spec=pltpu.PrefetchScalarGridSpec(
            num_scalar_prefetch=1,
            grid=(n // tm,),
            in_specs=[pl.BlockSpec(memory_space=pl.ANY),
                      pl.BlockSpec((tm, d), lambda t, pos: (t, 0)),
                      pl.BlockSpec((tm, LANES), lambda t, pos: (t, 0)),
                      pl.BlockSpec((1, d), lambda t, pos: (0, 0)),
                      pl.BlockSpec((1, d), lambda t, pos: (0, 0))],
            out_specs=pl.BlockSpec((tm, d), lambda t, pos: (t, 0)),
            scratch_shapes=[pltpu.VMEM((2, TOP_K * tm // SUBLANES, SUBLANES, d), F32),
                            pltpu.SemaphoreType.DMA((2,))]),
        out_shape=jax.ShapeDtypeStruct((n, d), F32),
        compiler_params=_cparams(1),
        name="moe_combine_ln2",
    )(pos, y_sorted, x1, gate, ln_g, ln_b)


def _route(top_idx, rank, counts, n):
    nk = n * TOP_K
    n_tiles = -(-nk // TM_MOE) + N_EXPERTS
    padded = (counts + TM_MOE - 1) // TM_MOE * TM_MOE
    pad_end = jnp.cumsum(padded)
    pad_start = pad_end - padded
    lead = padded - counts
    onehot = top_idx[:, :, None] == jnp.arange(N_EXPERTS, dtype=jnp.int32)
    dest = (rank + jnp.sum(jnp.where(onehot, pad_start + lead, 0), axis=-1)).reshape(nk)
    n_valid = (pad_end[-1] // TM_MOE).astype(jnp.int32)
    tiles = jnp.arange(n_tiles, dtype=jnp.int32)
    tile_start = jnp.minimum(tiles, n_valid - 1) * TM_MOE
    tile_expert = jnp.minimum(jnp.sum(pad_end[None, :] <= tile_start[:, None], axis=1),
                              N_EXPERTS - 1).astype(jnp.int32)
    skip_in = lead[tile_expert] - (tiles * TM_MOE - pad_start[tile_expert])
    tile_skip = jnp.where(tiles < n_valid, jnp.clip(skip_in, 0, TM_MOE), TM_MOE).astype(jnp.int32)
    per_tile = TM_MOE // MOE_SUB
    subs = jnp.arange(n_tiles * per_tile, dtype=jnp.int32)
    covered = (subs % per_tile) * MOE_SUB >= tile_skip[subs // per_tile]
    zrows = jnp.where(covered, -1, subs * MOE_SUB).astype(jnp.int32)
    return dest.astype(jnp.int32), zrows, tile_expert, n_valid.reshape(1), tile_skip, n_tiles


def _rope_tables(seq):
    half = MLA_ROPE_DIM // 2
    inv = ROPE_THETA ** (-jnp.arange(half, dtype=F32) / half)
    ang = jnp.arange(seq).astype(F32)[:, None] * inv[None, :]
    cos, sin = jnp.cos(ang), jnp.sin(ang)
    z = jnp.zeros((seq, LANES - MLA_ROPE_DIM), F32)
    zh = jnp.zeros((seq, half), F32)
    return (jnp.concatenate([cos, cos, z], axis=1),
            jnp.concatenate([-sin, zh, z], axis=1),
            jnp.concatenate([zh, sin, z], axis=1))


def _pad_cols(w, width):
    return jnp.pad(w, ((0, 0), (0, width - w.shape[1])))


def kernel(x, w_in, q_norm_g, kv_norm_g, w_uq, w_ukv, w_o_mla, w_o_dil, w_out, ln1_g, ln1_b,
           w_router, b_router, w1, b1, w2, b2, ln2_g, ln2_b):
    batch, seq, d_model = x.shape
    depth = w_in.shape[0]
    n = batch * seq
    alpha = (2.0 * depth) ** 0.25
    tabs = _rope_tables(seq)
    o_q, o_kv, o_pe = 0, MLA_Q_RANK, MLA_Q_RANK + MLA_KV_RANK
    o_dil = o_pe + MLA_ROPE_DIM
    o_gate = o_dil + 3 * DIL_WIDTH
    xt = x.reshape(n, d_model)
    for l in range(depth):
        wi = w_in[l]
        w_a = jnp.concatenate([wi[:, o_kv:o_pe], _pad_cols(wi[:, o_pe:o_dil], 2 * LANES),
                               wi[:, o_q:o_kv]], axis=1).astype(BF16)
        w_b = jnp.concatenate(
            [wi[:, o_dil + which * DIL_WIDTH + g * DIL_GROUP_WIDTH:][:, :DIL_GROUP_WIDTH]
             for g in range(len(DIL_GROUPS)) for which in range(3)], axis=1).astype(BF16)
        w_c = wi[:, o_gate:].astype(BF16)
        wq = w_uq[l].reshape(MLA_Q_RANK, MLA_HEADS, MLA_QK_DIM)
        wq = jnp.pad(wq, ((0, 0), (0, 0), (0, MLA_QK_PAD - MLA_QK_DIM)))
        wq = wq.reshape(MLA_Q_RANK, MLA_HEADS * MLA_QK_PAD).astype(BF16)

        proj_a = _proj(xt, w_a, F32, w_a.shape[1], False, "in_proj_latent")
        qkv_d = _proj_dilated(xt, w_b)
        gates = _proj(xt, w_c, F32, 1024, True, "in_proj_gates")

        q, k, v = _mla_up(proj_a, q_norm_g[l][None], kv_norm_g[l][None], wq,
                          w_ukv[l].astype(BF16), tabs, seq)
        o_mla = _mla_attention(q, k, v, batch, seq)
        dil = [_dilated_group(qkv_d[g], g, batch, seq) for g in range(len(DIL_GROUPS))]
        merged = _merge(o_mla, [o for o, _ in dil], [s for _, s in dil], gates,
                        w_o_mla[l].astype(BF16), w_o_dil[l].astype(BF16), d_model)

        w_r = _pad_cols(w_router[l], LANES)
        w_rh = w_r.astype(BF16)
        w_rl = (w_r - w_rh.astype(F32)).astype(BF16)
        b_r = jnp.concatenate([b_router[l], jnp.full((LANES - N_EXPERTS,), NEG_BIG, F32)])[None]
        x1, xpk, top_idx, top_gate, rank, counts = _ln1_router(
            merged, xt, w_out[l].astype(BF16), ln1_g[l][None], ln1_b[l][None], w_rh, w_rl, b_r, alpha)

        dest, zrows, tile_expert, n_valid, tile_skip, n_tiles = _route(
            top_idx[:, :TOP_K], rank[:, :TOP_K], counts[0, :N_EXPERTS], n)
        xs = _moe_scatter(xpk, dest, zrows, n_tiles)
        y_sorted = _moe_ffn(xs, tile_expert, n_valid, tile_skip, w1[l], b1[l][:, None, :], w2[l],
                            b2[l][:, None, :])
        xt = _moe_combine(y_sorted, dest, x1, top_gate, ln2_g[l][None], ln2_b[l][None], alpha)
    return xt.reshape(batch, seq, d_model)
```

```python
import functools

import numpy as np
import jax
import jax.numpy as jnp
from jax import lax
from jax.experimental import pallas as pl
from jax.experimental.pallas import tpu as pltpu

F32 = jnp.float32
BF16 = jnp.bfloat16

MLA_HEADS = 16
MLA_Q_RANK = 768
MLA_KV_RANK = 512
MLA_NOPE_DIM = 128
MLA_ROPE_DIM = 64
MLA_V_DIM = 128
MLA_QK_DIM = MLA_NOPE_DIM + MLA_ROPE_DIM
MLA_QK_PAD = 256
MLA_HEADS_PER_STEP = 2
MLA_LOOKAHEAD = 1
MLA_Q_SCALE =MLA_QK_DIM ** -0.5 * 1.4426950408889634
ROPE_THETA = 10000.0
DIL_GROUPS = ((128, 1), (512, 4), (2048, 16))
DIL_HEADS_PER_GROUP = 4
DIL_HEADS = DIL_HEADS_PER_GROUP * len(DIL_GROUPS)
DIL_HEAD_DIM = 128
DIL_GROUP_WIDTH = DIL_HEADS_PER_GROUP * DIL_HEAD_DIM
DIL_WIDTH = DIL_HEADS * DIL_HEAD_DIM
DIL_BLOCK = 128
DIL_RES_PER_STEP = 4
N_EXPERTS = 32
TOP_K = 4
SWIGLU_LIMIT = 7.0
SWIGLU_ALPHA = 1.702
LN_EPS = 1e-5
RMS_EPS = 1e-6
LANES = 128
SUBLANES = 8
NEG_BIG = -1e30

VMEM_LIMIT = 56 * 1024 * 1024
TM_PROJ = 1024
TM_PROJ_DIL = 1024
TM_UP = 512
TM_MERGE = 256
MERGE_COLS = 2048
TM_LN = 512
LN_SUB = 512
TM_MOE = 1024
MOE_SUB = 256
TM_SCATTER = 512
TM_COMBINE = 256
W1_COLS = 1024
W2_COLS = 1024
PAIR = LANES // 2


def _cparams(n_axes):
    return pltpu.CompilerParams(dimension_semantics=("arbitrary",) * n_axes,
                                vmem_limit_bytes=VMEM_LIMIT)


def _sigmoid(x):
    return 1.0 / (1.0 + jnp.exp(-x))


def _proj_cast_kernel(x_ref, w_ref, o_ref, xb_ref):
    xb = x_ref[...].astype(BF16)
    xb_ref[...] = xb
    o_ref[...] = jnp.dot(xb, w_ref[...], preferred_element_type=F32).astype(o_ref.dtype)


def _proj_kernel(x_ref, w_ref, o_ref, *, act):
    acc = jnp.dot(x_ref[...], w_ref[...], preferred_element_type=F32)
    if act:
        acc = _sigmoid(acc)
    o_ref[...] = acc.astype(o_ref.dtype)


def _proj_cast(x, w, out_dtype, name):
    m, k = x.shape
    nc = w.shape[1]
    tm = min(TM_PROJ, m)
    return pl.pallas_call(
        _proj_cast_kernel,
        grid=(m // tm,),
        in_specs=[pl.BlockSpec((tm, k), lambda i: (i, 0)),
                  pl.BlockSpec((k, nc), lambda i: (0, 0))],
        out_specs=[pl.BlockSpec((tm, nc), lambda i: (i, 0)),
                   pl.BlockSpec((tm, k), lambda i: (i, 0))],
        out_shape=[jax.ShapeDtypeStruct((m, nc), out_dtype),
                   jax.ShapeDtypeStruct((m, k), BF16)],
        compiler_params=_cparams(1),
        name=name,
    )(x, w)


def _proj(xb, w, out_dtype, tn, act, name):
    m, k = xb.shape
    nc = w.shape[1]
    tm = min(TM_PROJ, m)
    return pl.pallas_call(
        functools.partial(_proj_kernel, act=act),
        grid=(m // tm, nc // tn),
        in_specs=[pl.BlockSpec((tm, k), lambda i, j: (i, 0)),
                  pl.BlockSpec((k, tn), lambda i, j: (0, j))],
        out_specs=pl.BlockSpec((tm, tn), lambda i, j: (i, j)),
        out_shape=jax.ShapeDtypeStruct((m, nc), out_dtype),
        compiler_params=_cparams(2),
        name=name,
    )(xb, w)


def _proj_dil_kernel(x_ref, w_ref, o0_ref, o1_ref, o2_ref, acc_ref):
    j = pl.program_id(1)
    tm = x_ref.shape[0]
    width = w_ref.shape[1]

    acc = jnp.dot(x_ref[...], w_ref[...], preferred_element_type=F32)
    n_cb = width // LANES
    for c in range(n_cb):
        acc_ref[c] = acc[:, c * LANES:(c + 1) * LANES]
    for g, o_ref in enumerate((o0_ref, o1_ref, o2_ref)):
        dil = DIL_GROUPS[g][1]

        @pl.when(j == g)
        def _(o_ref=o_ref, dil=dil):
            for r in range(dil):
                for c in range(n_cb):
                    o_ref[:, r * width + c * LANES:r * width + (c + 1) * LANES] = (
                        acc_ref[c, pl.ds(r, tm // dil, stride=dil), :].astype(BF16))


def _proj_dilated(x, w):
    m, k = x.shape
    tm = min(TM_PROJ_DIL, m)
    width = 3 * DIL_GROUP_WIDTH
    n_groups = len(DIL_GROUPS)
    dils = [d for _, d in DIL_GROUPS]
    return pl.pallas_call(
        _proj_dil_kernel,
        grid=(m // tm, n_groups),
        in_specs=[pl.BlockSpec((tm, k), lambda i, j: (i, 0)),
                  pl.BlockSpec((k, width), lambda i, j: (0, j))],
        out_specs=[pl.BlockSpec((tm // d, d * width), lambda i, j: (i, 0)) for d in dils],
        out_shape=[jax.ShapeDtypeStruct((m // d, d * width), BF16) for d in dils],
        scratch_shapes=[pltpu.VMEM((width // LANES, tm, LANES), F32)],
        compiler_params=_cparams(2),
        name="in_proj_dilated",
    )(x, w)


def _rope_tile(r, cos, sin_lo, sin_hi):
    half = MLA_ROPE_DIM // 2
    return (r * cos + pltpu.roll(r, LANES - half, 1) * sin_lo + pltpu.roll(r, half, 1) * sin_hi)


def _rms(x, g):
    ms = jnp.mean(x * x, axis=-1, keepdims=True)
    return x * lax.rsqrt(ms + RMS_EPS) * g


def _q_up_kernel(cq_ref, g_ref, w_ref, cos_ref, slo_ref, shi_ref, o_ref):
    cqn = _rms(cq_ref[...], g_ref[...]).astype(BF16)
    cos, slo, shi = cos_ref[...], slo_ref[...], shi_ref[...]
    for h in range(MLA_HEADS):
        c0 = h * MLA_QK_PAD
        qh = jnp.dot(cqn, w_ref[:, c0:c0 + MLA_QK_PAD], preferred_element_type=F32) * MLA_Q_SCALE
        o_ref[:, c0:c0 + LANES] = qh[:, :LANES].astype(BF16)
        o_ref[:, c0 + LANES:c0 + MLA_QK_PAD] = _rope_tile(qh[:, LANES:], cos, slo, shi).astype(BF16)


def _kv_up_kernel(ckv_ref, kpe_ref, g_ref, w_ref, cos_ref, slo_ref, shi_ref, k_ref, v_ref):
    ckvn = _rms(ckv_ref[...], g_ref[...]).astype(BF16)
    k_rot = _rope_tile(kpe_ref[...], cos_ref[...], slo_ref[...], shi_ref[...]).astype(BF16)
    lane = lax.broadcasted_iota(jnp.int32, kpe_ref.shape, 1)
    ones_col = jnp.where(lane == 0, 1.0, 0.0).astype(BF16)
    for h in range(MLA_HEADS):
        c0 = h * (MLA_NOPE_DIM + MLA_V_DIM)
        kv = jnp.dot(ckvn, w_ref[:, c0:c0 + MLA_NOPE_DIM + MLA_V_DIM], preferred_element_type=F32)
        k0 = h * MLA_QK_PAD
        k_ref[:, k0:k0 + LANES] = kv[:, :MLA_NOPE_DIM].astype(BF16)
        k_ref[:, k0 + LANES:k0 + MLA_QK_PAD] = k_rot
        v_ref[:, k0:k0 + MLA_V_DIM] = kv[:, MLA_NOPE_DIM:].astype(BF16)
        v_ref[:, k0 + MLA_V_DIM:k0 + MLA_QK_PAD] = ones_col


def _mla_up(proj_a, q_g, kv_g, w_uq, w_ukv, tabs, seq):
    n = proj_a.shape[0]
    tm = min(TM_UP, seq)
    sb = seq // tm
    tab_specs = [pl.BlockSpec((tm, LANES), lambda i: (i % sb, 0))] * 3
    qw = MLA_HEADS * MLA_QK_PAD
    q = pl.pallas_call(
        _q_up_kernel,
        grid=(n // tm,),
        in_specs=[pl.BlockSpec((tm, MLA_Q_RANK), lambda i: (i, 1)),
                  pl.BlockSpec((1, MLA_Q_RANK), lambda i: (0, 0)),
                  pl.BlockSpec((MLA_Q_RANK, qw), lambda i: (0, 0))] + tab_specs,
        out_specs=pl.BlockSpec((tm, qw), lambda i: (i, 0)),
        out_shape=jax.ShapeDtypeStruct((n, qw), BF16),
        compiler_params=_cparams(1),
        name="mla_q_up",
    )(proj_a, q_g, w_uq, *tabs)
    kvw = MLA_HEADS * (MLA_NOPE_DIM + MLA_V_DIM)
    k, v = pl.pallas_call(
        _kv_up_kernel,
        grid=(n // tm,),
        in_specs=[pl.BlockSpec((tm, MLA_KV_RANK), lambda i: (i, 0)),
                  pl.BlockSpec((tm, LANES), lambda i: (i, MLA_KV_RANK // LANES)),
                  pl.BlockSpec((1, MLA_KV_RANK), lambda i: (0, 0)),
                  pl.BlockSpec((MLA_KV_RANK, kvw), lambda i: (0, 0))] + tab_specs,
        out_specs=[pl.BlockSpec((tm, qw), lambda i: (i, 0)),
                   pl.BlockSpec((tm, qw), lambda i: (i, 0))],
        out_shape=[jax.ShapeDtypeStruct((n, qw), BF16),
                   jax.ShapeDtypeStruct((n, qw), BF16)],
        compiler_params=_cparams(1),
        name="mla_kv_up",
    )(proj_a, proj_a, kv_g, w_ukv, *tabs)
    return q, k, v


def _nt_dot(a, b):
    return lax.dot_general(a, b, (((1,), (1,)), ((), ())), preferred_element_type=F32)


def _mla_attn_kernel(q_ref, k_ref, v_ref, o_ref, *, tq):
    seq = q_ref.shape[0]
    row = lax.broadcasted_iota(jnp.int32, (tq, tq), 0)
    col = lax.broadcasted_iota(jnp.int32, (tq, tq), 1)
    causal = col <= row
    def scores(hh, qi):
        cq = slice(hh * MLA_QK_PAD, (hh + 1) * MLA_QK_PAD)
        r0 = qi * tq
        q = q_ref[r0:r0 + tq, cq]
        s_d = jnp.where(causal, _nt_dot(q, k_ref[r0:r0 + tq, cq]), -jnp.inf)
        s_p = _nt_dot(q, k_ref[0:r0, cq]) if qi > 0 else None
        return s_d, s_p

    def finish(hh, qi, s_d, s_p):
        cq = slice(hh * MLA_QK_PAD, (hh + 1) * MLA_QK_PAD)
        r0 = qi * tq
        m = jnp.max(s_d, axis=-1, keepdims=True)
        if qi > 0:
            m = jnp.maximum(m, jnp.max(s_p, axis=-1, keepdims=True))
        acc = jnp.dot(jnp.exp2(s_d - m).astype(BF16), v_ref[r0:r0 + tq, cq], preferred_element_type=F32)
        if qi > 0:
            acc = acc + jnp.dot(jnp.exp2(s_p - m).astype(BF16), v_ref[0:r0, cq],
                                preferred_element_type=F32)
        o_ref[r0:r0 + tq, hh * MLA_V_DIM:(hh + 1) * MLA_V_DIM] = (
            acc[:, :MLA_V_DIM] / acc[:, MLA_V_DIM:MLA_V_DIM + 1]).astype(o_ref.dtype)

    items = [(hh, qi) for hh in range(q_ref.shape[1] // MLA_QK_PAD) for qi in range(seq // tq)]
    pending = [scores(*it) for it in items[:MLA_LOOKAHEAD]]
    for k, cur in enumerate(items):
        if k + MLA_LOOKAHEAD < len(items):
            pending.append(scores(*items[k + MLA_LOOKAHEAD]))
        finish(*cur, *pending.pop(0))


def _mla_attention(q, k, v, batch, seq):
    n = q.shape[0]
    return pl.pallas_call(
        functools.partial(_mla_attn_kernel, tq=min(256, seq)),
        grid=(batch, MLA_HEADS // MLA_HEADS_PER_STEP),
        in_specs=[pl.BlockSpec((seq, MLA_HEADS_PER_STEP * MLA_QK_PAD), lambda b, h: (b, h))] * 3,
        out_specs=pl.BlockSpec((seq, MLA_HEADS_PER_STEP * MLA_V_DIM), lambda b, h: (b, h)),
        out_shape=jax.ShapeDtypeStruct((n, MLA_HEADS * MLA_V_DIM), BF16),
        compiler_params=_cparams(2),
        name="mla_attention",
    )(q, k, v)


def _dil_kernel(qkv_ref, o_ref, lse_ref, *, dilation, slopes, scale):
    n = qkv_ref.shape[0]
    w = DIL_GROUP_WIDTH
    res = qkv_ref.shape[1] // (3 * w)
    blk = DIL_BLOCK
    ri = lax.broadcasted_iota(jnp.int32, (blk, 2 * blk), 0)
    cj = lax.broadcasted_iota(jnp.int32, (blk, 2 * blk), 1) - blk
    steps = ri - cj
    valid2 = (steps >= 0) & (steps <= blk)
    dist2 = (steps * dilation).astype(F32)
    valid1, dist1 = valid2[:, blk:], dist2[:, blk:]
    lane = lax.broadcasted_iota(jnp.int32, (blk, LANES), 1)

    def block(rr, q0, k0, klen, valid, dist):
        heads = range(DIL_HEADS_PER_GROUP)
        c0 = [rr * 3 * w + j * DIL_HEAD_DIM for j in heads]
        s = [_nt_dot(qkv_ref[pl.ds(q0, blk), c0[j]:c0[j] + DIL_HEAD_DIM],
                     qkv_ref[pl.ds(k0, klen), c0[j] + w:c0[j] + w + DIL_HEAD_DIM]) for j in heads]
        s = [jnp.where(valid, s[j] * scale - slopes[j] * dist, -jnp.inf) for j in heads]
        m = [jnp.max(s[j], axis=-1, keepdims=True) for j in heads]
        p = [jnp.exp(s[j] - m[j]) for j in heads]
        l = [jnp.sum(p[j], axis=-1, keepdims=True) for j in heads]
        lse_tile = jnp.zeros((blk, LANES), F32)
        for j in heads:
            vv = qkv_ref[pl.ds(k0, klen), c0[j] + 2 * w:c0[j] + 2 * w + DIL_HEAD_DIM]
            o = jnp.dot(p[j].astype(BF16), vv, preferred_element_type=F32) / l[j]
            o0 = rr * w + j * DIL_HEAD_DIM
            o_ref[pl.ds(q0, blk), o0:o0 + DIL_HEAD_DIM] = o
            lse_tile = jnp.where(lane == j, m[j] + jnp.log(l[j]), lse_tile)
        lse_ref[pl.ds(q0, blk), rr * LANES:(rr + 1) * LANES] = lse_tile

    for rr in range(res):
        block(rr, 0, 0, blk, valid1, dist1)

        def body(i, carry, rr=rr):
            q0 = pl.multiple_of(i * blk, blk)
            k0 = pl.multiple_of(i * blk - blk, blk)
            block(rr, q0, k0, 2 * blk, valid2, dist2)
            return carry

        lax.fori_loop(1, n // blk, body, 0, unroll=3)


def _dilated_group(qkv, g, batch, seq):
    window, dilation = DIL_GROUPS[g]
    assert window // dilation == DIL_BLOCK and seq % (dilation * DIL_BLOCK) == 0
    n = seq // dilation
    w = DIL_GROUP_WIDTH
    qkv_v = qkv.reshape(batch, n, dilation * 3 * w)
    slopes = np.float32(2.0) ** (np.float32(-8.0) * np.arange(1, DIL_HEADS + 1, dtype=np.float32)
                                 / np.float32(DIL_HEADS))
    slopes = tuple(float(s) for s in slopes[g * DIL_HEADS_PER_GROUP:(g + 1) * DIL_HEADS_PER_GROUP])

    res = min(dilation, DIL_RES_PER_STEP)
    o, lse = pl.pallas_call(
        functools.partial(_dil_kernel, dilation=dilation, slopes=slopes, scale=DIL_HEAD_DIM ** -0.5),
        grid=(batch, dilation // res),
        in_specs=[pl.BlockSpec((None, n, res * 3 * w), lambda b, r: (b, 0, r))],
        out_specs=[pl.BlockSpec((None, n, res * w), lambda b, r: (b, 0, r)),
                   pl.BlockSpec((None, n, res * LANES), lambda b, r: (b, 0, r))],
        out_shape=[jax.ShapeDtypeStruct((batch, n, dilation * w), F32),
                   jax.ShapeDtypeStruct((batch, n, dilation * LANES), F32)],
        compiler_params=_cparams(2),
        name=f"dilated_attn_g{g}",
    )(qkv_v)
    return o.reshape(batch * n, dilation * w), lse.reshape(batch * n, dilation * LANES)


def _merge_kernel(omla_ref, o0_ref, o1_ref, o2_ref, l0_ref, l1_ref, l2_ref, sgm_ref, sgd_ref,
                  wm_ref, wd_ref, out_ref, od_ref, otok_ref, ltok_ref):
    tm = omla_ref.shape[0]
    hpg = DIL_HEADS_PER_GROUP

    @pl.when(pl.program_id(1) == 0)
    def _():
        for g, (o_ref, l_ref) in enumerate(((o0_ref, l0_ref), (o1_ref, l1_ref), (o2_ref, l2_ref))):
            dil = DIL_GROUPS[g][1]
            for r in range(dil):
                rows = pl.ds(r, tm // dil, stride=dil)
                ltok_ref[g, rows, :] = l_ref[:, r * LANES:(r + 1) * LANES]
                for j in range(hpg):
                    c0 = (r * hpg + j) * DIL_HEAD_DIM
                    otok_ref[g * hpg + j, rows, :] = o_ref[:, c0:c0 + DIL_HEAD_DIM]
        lses = [ltok_ref[g] for g in range(len(DIL_GROUPS))]
        m = jnp.maximum(jnp.maximum(lses[0], lses[1]), lses[2])
        es = [jnp.exp(l - m) for l in lses]
        den = es[0] + es[1] + es[2]
        for g in range(len(DIL_GROUPS)):
            wg = es[g] / den
            for j in range(hpg):
                c0 = (g * hpg + j) * DIL_HEAD_DIM
                od_ref[:, c0:c0 + DIL_HEAD_DIM] = (otok_ref[g * hpg + j] * wg[:, j:j + 1]).astype(BF16)

    a = jnp.dot(omla_ref[...], wm_ref[...], preferred_element_type=F32)
    b = jnp.dot(od_ref[...], wd_ref[...], preferred_element_type=F32)
    out_ref[...] = (sgm_ref[...] * a + sgd_ref[...] * b).astype(out_ref.dtype)


def _merge(o_mla, o_dil, lse_dil, gates, w_o_mla, w_o_dil, d_model):
    n = o_mla.shape[0]
    tm = min(TM_MERGE, n)
    tn = min(MERGE_COLS, d_model)
    nj = d_model // tn
    row = lambda width: pl.BlockSpec((tm, width), lambda i, j: (i, 0))
    dils = [d for _, d in DIL_GROUPS]
    dil_row = lambda width: [pl.BlockSpec((tm // d, d * width), lambda i, j: (i, 0)) for d in dils]
    return pl.pallas_call(
        _merge_kernel,
        grid=(n // tm, nj),
        in_specs=[row(o_mla.shape[1])] + dil_row(DIL_GROUP_WIDTH) + dil_row(LANES) + [
            pl.BlockSpec((tm, tn), lambda i, j: (i, j)),
            pl.BlockSpec((tm, tn), lambda i, j: (i, j + nj)),
            pl.BlockSpec((w_o_mla.shape[0], tn), lambda i, j: (0, j)),
            pl.BlockSpec((w_o_dil.shape[0], tn), lambda i, j: (0, j))],
        out_specs=pl.BlockSpec((tm, tn), lambda i, j: (i, j)),
        out_shape=jax.ShapeDtypeStruct((n, d_model), BF16),
        scratch_shapes=[pltpu.VMEM((tm, DIL_WIDTH), BF16),
                        pltpu.VMEM((DIL_HEADS, tm, DIL_HEAD_DIM), F32),
                        pltpu.VMEM((len(DIL_GROUPS), tm, LANES), F32)],
        compiler_params=_cparams(2),
        name="branch_merge",
    )(o_mla, *o_dil, *lse_dil, gates, gates, w_o_mla, w_o_dil)


def _layer_norm(y, g, b):
    mu = jnp.mean(y, axis=-1, keepdims=True)
    yc = y - mu
    var = jnp.mean(yc * yc, axis=-1, keepdims=True)
    return yc * lax.rsqrt(var + LN_EPS) * g + b


def _pack_bf16_pairs(x):
    half = x.shape[1] // 2
    xb = x.astype(BF16).astype(F32)
    lo = lax.bitcast_convert_type(xb[:, :half], jnp.uint32) >> 16
    hi = lax.bitcast_convert_type(xb[:, half:], jnp.uint32) & jnp.uint32(0xFFFF0000)
    return lo | hi


def _unpack_bf16_pairs(w):
    lo = lax.bitcast_convert_type(w << 16, F32).astype(BF16)
    hi = lax.bitcast_convert_type(w & jnp.uint32(0xFFFF0000), F32).astype(BF16)
    return lo, hi


def _ln1_router_kernel(mg_ref, x_ref, wo_ref, g_ref, b_ref, wrc_ref, br_ref,
                       x1_ref, xpk_ref, idx_ref, gate_ref, rank_ref, cnt_ref, tri_ref, run_ref, *, alpha):
    sub = tri_ref.shape[0]

    @pl.when(pl.program_id(0) == 0)
    def _():
        r = lax.broadcasted_iota(jnp.int32, (sub, sub), 0)
        c = lax.broadcasted_iota(jnp.int32, (sub, sub), 1)
        tri_ref[...] = jnp.where(c < r, 1.0, 0.0).astype(BF16)
        run_ref[...] = jnp.zeros(run_ref.shape, F32)

    lane = lax.broadcasted_iota(jnp.int32, (sub, LANES), 1).astype(F32)
    blocks = [slice(s * sub, (s + 1) * sub) for s in range(x_ref.shape[0] // sub)]
    outs = [jnp.dot(mg_ref[rs, :], wo_ref[...], preferred_element_type=F32) for rs in blocks]
    for rs, out in zip(blocks, outs):
        x1 = _layer_norm(alpha * x_ref[rs, :] + out, g_ref[...], b_ref[...])
        x1_ref[rs, :] = x1
        xpk_ref[rs, :] = _pack_bf16_pairs(x1)
        xh = x1.astype(BF16)
        xl = (x1 - xh.astype(F32)).astype(BF16)
        hh_hl = jnp.dot(xh, wrc_ref[...], preferred_element_type=F32)
        logits = (hh_hl[:, :LANES] + hh_hl[:, LANES:]
                  + jnp.dot(xl, wrc_ref[:, :LANES], preferred_element_type=F32)) + br_ref[...]
        vals, idxs = [], []
        for _ in range(TOP_K):
            m = jnp.max(logits, axis=-1, keepdims=True)
            idx = jnp.min(jnp.where(logits == m, lane, float(LANES)), axis=-1, keepdims=True)
            vals.append(m)
            idxs.append(idx)
            logits = jnp.where(lane == idx, -jnp.inf, logits)
        es = [jnp.exp(v - vals[0]) for v in vals]
        den = es[0] + es[1] + es[2] + es[3]
        onehot = jnp.zeros(lane.shape, F32)
        for k in range(TOP_K):
            onehot = onehot + jnp.where(lane == idxs[k], 1.0, 0.0)
        before = run_ref[...] + jnp.dot(tri_ref[...], onehot.astype(BF16), preferred_element_type=F32)
        run_ref[...] = run_ref[...] + jnp.sum(onehot, axis=0, keepdims=True)
        idx_tile = jnp.zeros(lane.shape, F32)
        gate_tile = jnp.zeros(lane.shape, F32)
        rank_tile = jnp.zeros(lane.shape, F32)
        for k in range(TOP_K):
            rank = jnp.sum(jnp.where(lane == idxs[k], before, 0.0), axis=-1, keepdims=True)
            idx_tile = jnp.where(lane == float(k), idxs[k], idx_tile)
            gate_tile = jnp.where(lane == float(k), es[k] / den, gate_tile)
            rank_tile = jnp.where(lane == float(k), rank, rank_tile)
        idx_ref[rs, :] = idx_tile.astype(jnp.int32)
        gate_ref[rs, :] = gate_tile
        rank_ref[rs, :] = rank_tile.astype(jnp.int32)
    cnt_ref[...] = run_ref[...].astype(jnp.int32)


def _ln1_router(merged, x, w_out, ln_g, ln_b, w_rc, b_r, alpha):
    n, d = x.shape
    tm = min(TM_LN, n)
    full = lambda a: pl.BlockSpec(a.shape, lambda i: (0,) * a.ndim)
    row = lambda width: pl.BlockSpec((tm, width), lambda i: (i, 0))
    return pl.pallas_call(
        functools.partial(_ln1_router_kernel, alpha=alpha),
        grid=(n // tm,),
        in_specs=[row(d), row(d), full(w_out), full(ln_g), full(ln_b), full(w_rc), full(b_r)],
        out_specs=[row(d), row(d // 2), row(LANES), row(LANES), row(LANES),
                   pl.BlockSpec((1, LANES), lambda i: (0, 0))],
        out_shape=[jax.ShapeDtypeStruct((n, d), F32),
                   jax.ShapeDtypeStruct((n, d // 2), jnp.uint32),
                   jax.ShapeDtypeStruct((n, LANES), jnp.int32),
                   jax.ShapeDtypeStruct((n, LANES), F32),
                   jax.ShapeDtypeStruct((n, LANES), jnp.int32),
                   jax.ShapeDtypeStruct((1, LANES), jnp.int32)],
        scratch_shapes=[pltpu.VMEM((min(LN_SUB, tm),) * 2, BF16), pltpu.VMEM((1, LANES), F32)],
        compiler_params=_cparams(1),
        name="out_proj_ln1_router",
    )(merged, x, w_out, ln_g, ln_b, w_rc, b_r)


def _scatter_kernel(dest_ref, zrow_ref, xpk_ref, xs_hbm, zbuf, sem, zsem):
    t = pl.program_id(0)
    rows = TM_SCATTER

    @pl.when(t == 0)
    def _():
        zbuf[...] = jnp.zeros(zbuf.shape, zbuf.dtype)

        def zero_copy(i):
            row = pl.multiple_of(jnp.maximum(zrow_ref[i], 0), MOE_SUB)
            return pltpu.make_async_copy(zbuf, xs_hbm.at[pl.ds(row, MOE_SUB)], zsem)

        def zstart(i, carry):
            @pl.when(zrow_ref[i] >= 0)
            def _():
                zero_copy(i).start()
            return carry

        def zwait(i, carry):
            @pl.when(zrow_ref[i] >= 0)
            def _():
                zero_copy(i).wait()
            return carry

        lax.fori_loop(0, zrow_ref.shape[0], zstart, 0)
        lax.fori_loop(0, zrow_ref.shape[0], zwait, 0)

    def issue(r8, carry):
        for j in range(SUBLANES):
            for k in range(TOP_K):
                dst = dest_ref[(t * rows + r8 * SUBLANES + j) * TOP_K + k]
                pltpu.make_async_copy(xpk_ref.at[r8, pl.ds(j, 1)], xs_hbm.at[pl.ds(dst, 1)], sem).start()
        return carry

    groups = rows // SUBLANES
    lax.fori_loop(0, groups, issue, 0)
    pltpu.make_async_copy(xs_hbm.at[pl.ds(0, TOP_K * rows)], xs_hbm.at[pl.ds(0, TOP_K * rows)], sem).wait()


def _moe_scatter(xpk, dest, zrows, n_tiles):
    n, w = xpk.shape
    sub = SUBLANES
    xs = pl.pallas_call(
        _scatter_kernel,
        grid_spec=pltpu.PrefetchScalarGridSpec(
            num_scalar_prefetch=2,
            grid=(n // TM_SCATTER,),
            in_specs=[pl.BlockSpec((TM_SCATTER // sub, sub, w), lambda t, dest, zrows: (t, 0, 0))],
            out_specs=pl.BlockSpec(memory_space=pl.ANY),
            scratch_shapes=[pltpu.VMEM((MOE_SUB, w), xpk.dtype), pltpu.SemaphoreType.DMA,
                            pltpu.SemaphoreType.DMA]),
        out_shape=jax.ShapeDtypeStruct((n_tiles * TM_MOE, w), xpk.dtype),
        compiler_params=_cparams(1),
        name="moe_scatter",
    )(dest, zrows, xpk.reshape(n // sub, sub, w))
    return xs


def _new_expert(te_ref, t):
    return (t == 0) | (te_ref[t] != te_ref[jnp.maximum(t - 1, 0)])


def _for_sub_blocks(skip, n_rows, matmul, finish, fill):
    pair = 2 * MOE_SUB
    pairs = [slice(r0, r0 + pair) for r0 in range(0, n_rows, pair)]

    @pl.when(skip == 0)
    def _():
        results = [matmul(rs) for rs in pairs]
        for rs, res in zip(pairs, results):
            finish(rs, res)

    for both in pairs:
        r0 = both.start
        first, second = slice(r0, r0 + MOE_SUB), slice(r0 + MOE_SUB, r0 + pair)

        @pl.when((skip > 0) & (skip < r0 + MOE_SUB))
        def _(both=both):
            finish(both, matmul(both))

        @pl.when((skip >= r0 + MOE_SUB) & (skip < r0 + pair))
        def _(first=first, second=second):
            fill(first)
            finish(second, matmul(second))

        pl.when(skip >= r0 + pair)(functools.partial(fill, both))


def _ffn_a_kernel(te_ref, nt_ref, skip_ref, xs_ref, w_ref, b_ref, a_ref, wb_ref):
    t = pl.program_id(1)

    @pl.when(_new_expert(te_ref, t))
    def _():
        wb_ref[...] = w_ref[...].astype(BF16)

    def matmul(rs):
        x = jnp.concatenate(_unpack_bf16_pairs(xs_ref[rs, :]), axis=1)
        return jnp.dot(x, wb_ref[...], preferred_element_type=F32) + b_ref[...]

    def finish(rs, h):
        even = lax.broadcasted_iota(jnp.int32, (h.shape[0], LANES), 1) % 2 == 0
        for g in range(h.shape[1] // (2 * LANES)):
            h0 = h[:, 2 * g * LANES:(2 * g + 1) * LANES]
            h1 = h[:, (2 * g + 1) * LANES:(2 * g + 2) * LANES]
            glu = jnp.where(even, h0, pltpu.roll(h1, 1, 1))
            lin = jnp.where(even, pltpu.roll(h0, LANES - 1, 1), h1)
            x_glu = jnp.minimum(glu, SWIGLU_LIMIT)
            x_lin = jnp.clip(lin, -SWIGLU_LIMIT, SWIGLU_LIMIT)
            a_ref[rs, g * LANES:(g + 1) * LANES] = (
                x_glu * _sigmoid(SWIGLU_ALPHA * x_glu) * (x_lin + 1.0)).astype(a_ref.dtype)

    def fill(rs):
        a_ref[rs, :] = jnp.zeros((rs.stop - rs.start, a_ref.shape[1]), a_ref.dtype)

    _for_sub_blocks(skip_ref[t], a_ref.shape[0], matmul, finish, fill)


def _ffn_b_kernel(te_ref, nt_ref, skip_ref, a_ref, w_ref, b_ref, o_ref, perm_ref, wb_ref):
    t = pl.program_id(1)

    @pl.when(_new_expert(te_ref, t))
    def _():
        for c in range(w_ref.shape[1] // LANES):
            cs = slice(c * LANES, (c + 1) * LANES)
            for g in range(w_ref.shape[0] // LANES):
                for p in range(2):
                    src = g * LANES + p * PAIR
                    perm_ref[c, pl.ds(g * LANES + p, PAIR, stride=2), :] = w_ref[src:src + PAIR, cs]
            wb_ref[:, cs] = perm_ref[c].astype(BF16)

    def matmul(rs):
        return jnp.dot(a_ref[rs, :], wb_ref[...], preferred_element_type=F32) + b_ref[...]

    def finish(rs, res):
        o_ref[rs, :] = res

    def fill(rs):
        o_ref[rs, :] = jnp.zeros((rs.stop - rs.start, o_ref.shape[1]), o_ref.dtype)

    _for_sub_blocks(skip_ref[t], o_ref.shape[0], matmul, finish, fill)


def _moe_ffn(xs, tile_expert, n_valid, tile_skip, w1, b1, w2, b2):
    rows = xs.shape[0]
    n_tiles = rows // TM_MOE
    _, d, cols = w1.shape
    d_ff = cols // 2
    wc = min(W1_COLS, cols)
    tile = lambda t, nt: jnp.minimum(t, nt[0] - 1)
    a = pl.pallas_call(
        _ffn_a_kernel,
        grid_spec=pltpu.PrefetchScalarGridSpec(
            num_scalar_prefetch=3,
            grid=(cols // wc, n_tiles),
            in_specs=[pl.BlockSpec((TM_MOE, d // 2), lambda f, t, te, nt, tr: (tile(t, nt), 0)),
                      pl.BlockSpec((None, d, wc), lambda f, t, te, nt, tr: (te[t], 0, f)),
                      pl.BlockSpec((None, 1, wc), lambda f, t, te, nt, tr: (te[t], 0, f))],
            out_specs=pl.BlockSpec((TM_MOE, wc // 2), lambda f, t, te, nt, tr: (t, f)),
            scratch_shapes=[pltpu.VMEM((d, wc), BF16)]),
        out_shape=jax.ShapeDtypeStruct((rows, d_ff), BF16),
        compiler_params=_cparams(2),
        name="moe_ffn_up",
    )(tile_expert, n_valid, tile_skip, xs, w1, b1)
    nc = min(W2_COLS, d)
    return pl.pallas_call(
        _ffn_b_kernel,
        grid_spec=pltpu.PrefetchScalarGridSpec(
            num_scalar_prefetch=3,
            grid=(d // nc, n_tiles),
            in_specs=[pl.BlockSpec((TM_MOE, d_ff), lambda c, t, te, nt, tr: (tile(t, nt), 0)),
                      pl.BlockSpec((None, d_ff, nc), lambda c, t, te, nt, tr: (te[t], 0, c)),
                      pl.BlockSpec((None, 1, nc), lambda c, t, te, nt, tr: (te[t], 0, c))],
            out_specs=pl.BlockSpec((TM_MOE, nc), lambda c, t, te, nt, tr: (t, c)),
            scratch_shapes=[pltpu.VMEM((nc // LANES, d_ff, LANES), F32), pltpu.VMEM((d_ff, nc), BF16)]),
        out_shape=jax.ShapeDtypeStruct((rows, d), F32),
        compiler_params=_cparams(2),
        name="moe_ffn_down",
    )(tile_expert, n_valid, tile_skip, a, w2, b2)


def _combine_kernel(pos_ref, y_hbm, x_ref, gate_ref, g_ref, b_ref, o_ref, buf, sem, *, alpha):
    t = pl.program_id(0)
    rows = x_ref.shape[0]
    groups = rows // SUBLANES
    slot = t % 2

    def issue(step, dst_slot):
        def body(r8, carry):
            for j in range(SUBLANES):
                for k in range(TOP_K):
                    p = pos_ref[(step * rows + r8 * SUBLANES + j) * TOP_K + k]
                    pltpu.make_async_copy(y_hbm.at[pl.ds(p, 1)],
                                          buf.at[dst_slot, k * groups + r8, pl.ds(j, 1)],
                                          sem.at[dst_slot]).start()
            return carry

        lax.fori_loop(0, groups, body, 0)

    @pl.when(t == 0)
    def _():
        issue(0, 0)

    @pl.when(t + 1 < pl.num_programs(0))
    def _():
        issue(t + 1, 1 - slot)

    pltpu.make_async_copy(y_hbm.at[pl.ds(0, TOP_K * rows)], y_hbm.at[pl.ds(0, TOP_K * rows)],
                          sem.at[slot]).wait()
    gate = gate_ref[...]
    ffn = None
    for k in range(TOP_K):
        yk = buf[slot, k * groups:(k + 1) * groups].reshape(rows, buf.shape[-1]) * gate[:, k:k + 1]
        ffn = yk if ffn is None else ffn + yk
    o_ref[...] = _layer_norm(alpha * x_ref[...] + ffn, g_ref[...], b_ref[...])


def _moe_combine(y_sorted, pos, x1, gate, ln_g, ln_b, alpha):
    n, d = x1.shape
    tm = min(TM_COMBINE, n)
    return pl.pallas_call(
        functools.partial(_combine_kernel, alpha=alpha),
        grid_spec=pltpu.PrefetchScalarGridSpec(
            num_scalar_prefetch=1,
            grid=(n // tm,),
            in_specs=[pl.BlockSpec(memory_space=pl.ANY),
                      pl.BlockSpec((tm, d), lambda t, pos: (t, 0)),
                      pl.BlockSpec((tm, LANES), lambda t, pos: (t, 0)),
                      pl.BlockSpec((1, d), lambda t, pos: (0, 0)),
                      pl.BlockSpec((1, d), lambda t, pos: (0, 0))],
            out_specs=pl.BlockSpec((tm, d), lambda t, pos: (t, 0)),
            scratch_shapes=[pltpu.VMEM((2, TOP_K * tm // SUBLANES, SUBLANES, d), F32),
                            pltpu.SemaphoreType.DMA((2,))]),
        out_shape=jax.ShapeDtypeStruct((n, d), F32),
        compiler_params=_cparams(1),
        name="moe_combine_ln2",
    )(pos, y_sorted, x1, gate, ln_g, ln_b)


def _route(top_idx, rank, counts, n):
    nk = n * TOP_K
    n_tiles = -(-nk // TM_MOE) + N_EXPERTS
    padded = (counts + TM_MOE - 1) // TM_MOE * TM_MOE
    pad_end = jnp.cumsum(padded)
    pad_start = pad_end - padded
    lead = padded - counts
    onehot = top_idx[:, :, None] == jnp.arange(N_EXPERTS, dtype=jnp.int32)
    dest = (rank + jnp.sum(jnp.where(onehot, pad_start + lead, 0), axis=-1)).reshape(nk)
    n_valid = (pad_end[-1] // TM_MOE).astype(jnp.int32)
    tiles = jnp.arange(n_tiles, dtype=jnp.int32)
    tile_start = jnp.minimum(tiles, n_valid - 1) * TM_MOE
    tile_expert = jnp.minimum(jnp.sum(pad_end[None, :] <= tile_start[:, None], axis=1),
                              N_EXPERTS - 1).astype(jnp.int32)
    skip_in = lead[tile_expert] - (tiles * TM_MOE - pad_start[tile_expert])
    tile_skip = jnp.where(tiles < n_valid, jnp.clip(skip_in, 0, TM_MOE), TM_MOE).astype(jnp.int32)
    per_tile = TM_MOE // MOE_SUB
    subs = jnp.arange(n_tiles * per_tile, dtype=jnp.int32)
    covered = (subs % per_tile) * MOE_SUB >= tile_skip[subs // per_tile]
    zrows = jnp.where(covered, -1, subs * MOE_SUB).astype(jnp.int32)
    return dest.astype(jnp.int32), zrows, tile_expert, n_valid.reshape(1), tile_skip, n_tiles


def _rope_tables(seq):
    half = MLA_ROPE_DIM // 2
    inv = ROPE_THETA ** (-jnp.arange(half, dtype=F32) / half)
    ang = jnp.arange(seq).astype(F32)[:, None] * inv[None, :]
    cos, sin = jnp.cos(ang), jnp.sin(ang)
    z = jnp.zeros((seq, LANES - MLA_ROPE_DIM), F32)
    zh = jnp.zeros((seq, half), F32)
    return (jnp.concatenate([cos, cos, z], axis=1),
            jnp.concatenate([-sin, zh, z], axis=1),
            jnp.concatenate([zh, sin, z], axis=1))


def _pad_cols(w, width):
    return jnp.pad(w, ((0, 0), (0, width - w.shape[1])))


def kernel(x, w_in, q_norm_g, kv_norm_g, w_uq, w_ukv, w_o_mla, w_o_dil, w_out, ln1_g, ln1_b,
           w_router, b_router, w1, b1, w2, b2, ln2_g, ln2_b):
    batch, seq, d_model = x.shape
    depth = w_in.shape[0]
    n = batch * seq
    alpha = (2.0 * depth) ** 0.25
    tabs = _rope_tables(seq)
    o_q, o_kv, o_pe = 0, MLA_Q_RANK, MLA_Q_RANK + MLA_KV_RANK
    o_dil = o_pe + MLA_ROPE_DIM
    o_gate = o_dil + 3 * DIL_WIDTH
    xt = x.reshape(n, d_model)
    for l in range(depth):
        wi = w_in[l]
        w_a = jnp.concatenate([wi[:, o_kv:o_pe], _pad_cols(wi[:, o_pe:o_dil], 2 * LANES),
                               wi[:, o_q:o_kv]], axis=1).astype(BF16)
        w_b = jnp.concatenate(
            [wi[:, o_dil + which * DIL_WIDTH + g * DIL_GROUP_WIDTH:][:, :DIL_GROUP_WIDTH]
             for g in range(len(DIL_GROUPS)) for which in range(3)], axis=1).astype(BF16)
        w_c = wi[:, o_gate:].astype(BF16)
        wq = w_uq[l].reshape(MLA_Q_RANK, MLA_HEADS, MLA_QK_DIM)
        wq = jnp.pad(wq, ((0, 0), (0, 0), (0, MLA_QK_PAD - MLA_QK_DIM)))
        wq = wq.reshape(MLA_Q_RANK, MLA_HEADS * MLA_QK_PAD).astype(BF16)

        proj_a, xb = _proj_cast(xt, w_a, F32, "in_proj_latent")
        qkv_d = _proj_dilated(xb, w_b)
        gates = _proj(xb, w_c, F32, 1024, True, "in_proj_gates")

        q, k, v = _mla_up(proj_a, q_norm_g[l][None], kv_norm_g[l][None], wq,
                          w_ukv[l].astype(BF16), tabs, seq)
        o_mla = _mla_attention(q, k, v, batch, seq)
        dil = [_dilated_group(qkv_d[g], g, batch, seq) for g in range(len(DIL_GROUPS))]
        merged = _merge(o_mla, [o for o, _ in dil], [s for _, s in dil], gates,
                        w_o_mla[l].astype(BF16), w_o_dil[l].astype(BF16), d_model)

        w_r = _pad_cols(w_router[l], LANES)
        w_rh = w_r.astype(BF16)
        w_rl = (w_r - w_rh.astype(F32)).astype(BF16)
        w_rc = jnp.concatenate([w_rh, w_rl], axis=1)
        b_r = jnp.concatenate([b_router[l], jnp.full((LANES - N_EXPERTS,), NEG_BIG, F32)])[None]
        x1, xpk, top_idx, top_gate, rank, counts = _ln1_router(
            merged, xt, w_out[l].astype(BF16), ln1_g[l][None], ln1_b[l][None], w_rc, b_r, alpha)

        dest, zrows, tile_expert, n_valid, tile_skip, n_tiles = _route(
            top_idx[:, :TOP_K], rank[:, :TOP_K], counts[0, :N_EXPERTS], n)
        xs = _moe_scatter(xpk, dest, zrows, n_tiles)
        y_sorted = _moe_ffn(xs, tile_expert, n_valid, tile_skip, w1[l], b1[l][:, None, :], w2[l],
                            b2[l][:, None, :])
        xt = _moe_combine(y_sorted, dest, x1, top_gate, ln2_g[l][None], ln2_b[l][None], alpha)
    return xt.reshape(batch, seq, d_model)
```

```python
import functools

import numpy as np
import jax
import jax.numpy as jnp
from jax import lax
from jax.experimental import pallas as pl
from jax.experimental.pallas import tpu as pltpu

F32 = jnp.float32
BF16 = jnp.bfloat16

MLA_HEADS = 16
MLA_Q_RANK = 768
MLA_KV_RANK = 512
MLA_NOPE_DIM = 128
MLA_ROPE_DIM = 64
MLA_V_DIM = 128
MLA_QK_DIM = MLA_NOPE_DIM + MLA_ROPE_DIM
MLA_QK_PAD = 256
MLA_HEADS_PER_STEP = 4
MLA_LOOKAHEAD = 1
MLA_Q_SCALE =MLA_QK_DIM ** -0.5 * 1.4426950408889634
ROPE_THETA = 10000.0
DIL_GROUPS = ((128, 1), (512, 4), (2048, 16))
DIL_HEADS_PER_GROUP = 4
DIL_HEADS = DIL_HEADS_PER_GROUP * len(DIL_GROUPS)
DIL_HEAD_DIM = 128
DIL_GROUP_WIDTH = DIL_HEADS_PER_GROUP * DIL_HEAD_DIM
DIL_WIDTH = DIL_HEADS * DIL_HEAD_DIM
DIL_BLOCK = 128
DIL_RES_PER_STEP = 8
N_EXPERTS = 32
TOP_K = 4
SWIGLU_LIMIT = 7.0
SWIGLU_ALPHA = 1.702
LN_EPS = 1e-5
RMS_EPS = 1e-6
LANES = 128
SUBLANES = 8
NEG_BIG = -1e30

VMEM_LIMIT = 56 * 1024 * 1024
TM_PROJ = 1024
TM_PROJ_DIL = 1024
TM_UP = 512
TM_MERGE = 256
MERGE_COLS = 2048
TM_LN = 512
LN_SUB = 512
TM_MOE = 1024
MOE_SUB = 256
TM_SCATTER = 512
TM_COMBINE = 256
W1_COLS = 1024
W2_COLS = 1024
PAIR = LANES // 2


def _cparams(n_axes):
    return pltpu.CompilerParams(dimension_semantics=("arbitrary",) * n_axes,
                                vmem_limit_bytes=VMEM_LIMIT)


def _sigmoid(x):
    return 1.0 / (1.0 + jnp.exp(-x))


def _proj_cast_kernel(x_ref, w_ref, o_ref, xb_ref):
    xb = x_ref[...].astype(BF16)
    xb_ref[...] = xb
    o_ref[...] = jnp.dot(xb, w_ref[...], preferred_element_type=F32).astype(o_ref.dtype)


def _proj_kernel(x_ref, w_ref, o_ref, *, act):
    acc = jnp.dot(x_ref[...], w_ref[...], preferred_element_type=F32)
    if act:
        acc = _sigmoid(acc)
    o_ref[...] = acc.astype(o_ref.dtype)


def _proj_cast(x, w, out_dtype, name):
    m, k = x.shape
    nc = w.shape[1]
    tm = min(TM_PROJ, m)
    return pl.pallas_call(
        _proj_cast_kernel,
        grid=(m // tm,),
        in_specs=[pl.BlockSpec((tm, k), lambda i: (i, 0)),
                  pl.BlockSpec((k, nc), lambda i: (0, 0))],
        out_specs=[pl.BlockSpec((tm, nc), lambda i: (i, 0)),
                   pl.BlockSpec((tm, k), lambda i: (i, 0))],
        out_shape=[jax.ShapeDtypeStruct((m, nc), out_dtype),
                   jax.ShapeDtypeStruct((m, k), BF16)],
        compiler_params=_cparams(1),
        name=name,
    )(x, w)


def _proj(xb, w, out_dtype, tn, act, name):
    m, k = xb.shape
    nc = w.shape[1]
    tm = min(TM_PROJ, m)
    return pl.pallas_call(
        functools.partial(_proj_kernel, act=act),
        grid=(m // tm, nc // tn),
        in_specs=[pl.BlockSpec((tm, k), lambda i, j: (i, 0)),
                  pl.BlockSpec((k, tn), lambda i, j: (0, j))],
        out_specs=pl.BlockSpec((tm, tn), lambda i, j: (i, j)),
        out_shape=jax.ShapeDtypeStruct((m, nc), out_dtype),
        compiler_params=_cparams(2),
        name=name,
    )(xb, w)


def _proj_dil_kernel(x_ref, w_ref, o0_ref, o1_ref, o2_ref, acc_ref):
    j = pl.program_id(1)
    tm = x_ref.shape[0]
    width = w_ref.shape[1]

    acc = jnp.dot(x_ref[...], w_ref[...], preferred_element_type=F32)
    n_cb = width // LANES
    for c in range(n_cb):
        acc_ref[c] = acc[:, c * LANES:(c + 1) * LANES]
    for g, o_ref in enumerate((o0_ref, o1_ref, o2_ref)):
        dil = DIL_GROUPS[g][1]

        @pl.when(j == g)
        def _(o_ref=o_ref, dil=dil):
            for r in range(dil):
                for c in range(n_cb):
                    o_ref[:, r * width + c * LANES:r * width + (c + 1) * LANES] = (
                        acc_ref[c, pl.ds(r, tm // dil, stride=dil), :].astype(BF16))


def _proj_dilated(x, w):
    m, k = x.shape
    tm = min(TM_PROJ_DIL, m)
    width = 3 * DIL_GROUP_WIDTH
    n_groups = len(DIL_GROUPS)
    dils = [d for _, d in DIL_GROUPS]
    return pl.pallas_call(
        _proj_dil_kernel,
        grid=(m // tm, n_groups),
        in_specs=[pl.BlockSpec((tm, k), lambda i, j: (i, 0)),
                  pl.BlockSpec((k, width), lambda i, j: (0, j))],
        out_specs=[pl.BlockSpec((tm // d, d * width), lambda i, j: (i, 0)) for d in dils],
        out_shape=[jax.ShapeDtypeStruct((m // d, d * width), BF16) for d in dils],
        scratch_shapes=[pltpu.VMEM((width // LANES, tm, LANES), F32)],
        compiler_params=_cparams(2),
        name="in_proj_dilated",
    )(x, w)


def _rope_tile(r, cos, sin_lo, sin_hi):
    half = MLA_ROPE_DIM // 2
    return (r * cos + pltpu.roll(r, LANES - half, 1) * sin_lo + pltpu.roll(r, half, 1) * sin_hi)


def _rms(x, g):
    ms = jnp.mean(x * x, axis=-1, keepdims=True)
    return x * lax.rsqrt(ms + RMS_EPS) * g


def _q_up_kernel(cq_ref, g_ref, w_ref, cos_ref, slo_ref, shi_ref, o_ref):
    cqn = _rms(cq_ref[...], g_ref[...]).astype(BF16)
    cos, slo, shi = cos_ref[...], slo_ref[...], shi_ref[...]
    for h in range(MLA_HEADS):
        c0 = h * MLA_QK_PAD
        qh = jnp.dot(cqn, w_ref[:, c0:c0 + MLA_QK_PAD], preferred_element_type=F32) * MLA_Q_SCALE
        o_ref[:, c0:c0 + LANES] = qh[:, :LANES].astype(BF16)
        o_ref[:, c0 + LANES:c0 + MLA_QK_PAD] = _rope_tile(qh[:, LANES:], cos, slo, shi).astype(BF16)


def _kv_up_kernel(ckv_ref, kpe_ref, g_ref, w_ref, cos_ref, slo_ref, shi_ref, k_ref, v_ref):
    ckvn = _rms(ckv_ref[...], g_ref[...]).astype(BF16)
    k_rot = _rope_tile(kpe_ref[...], cos_ref[...], slo_ref[...], shi_ref[...]).astype(BF16)
    lane = lax.broadcasted_iota(jnp.int32, kpe_ref.shape, 1)
    ones_col = jnp.where(lane == 0, 1.0, 0.0).astype(BF16)
    for h in range(MLA_HEADS):
        c0 = h * (MLA_NOPE_DIM + MLA_V_DIM)
        kv = jnp.dot(ckvn, w_ref[:, c0:c0 + MLA_NOPE_DIM + MLA_V_DIM], preferred_element_type=F32)
        k0 = h * MLA_QK_PAD
        k_ref[:, k0:k0 + LANES] = kv[:, :MLA_NOPE_DIM].astype(BF16)
        k_ref[:, k0 + LANES:k0 + MLA_QK_PAD] = k_rot
        v_ref[:, k0:k0 + MLA_V_DIM] = kv[:, MLA_NOPE_DIM:].astype(BF16)
        v_ref[:, k0 + MLA_V_DIM:k0 + MLA_QK_PAD] = ones_col


def _mla_up(proj_a, q_g, kv_g, w_uq, w_ukv, tabs, seq):
    n = proj_a.shape[0]
    tm = min(TM_UP, seq)
    sb = seq // tm
    tab_specs = [pl.BlockSpec((tm, LANES), lambda i: (i % sb, 0))] * 3
    qw = MLA_HEADS * MLA_QK_PAD
    q = pl.pallas_call(
        _q_up_kernel,
        grid=(n // tm,),
        in_specs=[pl.BlockSpec((tm, MLA_Q_RANK), lambda i: (i, 1)),
                  pl.BlockSpec((1, MLA_Q_RANK), lambda i: (0, 0)),
                  pl.BlockSpec((MLA_Q_RANK, qw), lambda i: (0, 0))] + tab_specs,
        out_specs=pl.BlockSpec((tm, qw), lambda i: (i, 0)),
        out_shape=jax.ShapeDtypeStruct((n, qw), BF16),
        compiler_params=_cparams(1),
        name="mla_q_up",
    )(proj_a, q_g, w_uq, *tabs)
    kvw = MLA_HEADS * (MLA_NOPE_DIM + MLA_V_DIM)
    k, v = pl.pallas_call(
        _kv_up_kernel,
        grid=(n // tm,),
        in_specs=[pl.BlockSpec((tm, MLA_KV_RANK), lambda i: (i, 0)),
                  pl.BlockSpec((tm, LANES), lambda i: (i, MLA_KV_RANK // LANES)),
                  pl.BlockSpec((1, MLA_KV_RANK), lambda i: (0, 0)),
                  pl.BlockSpec((MLA_KV_RANK, kvw), lambda i: (0, 0))] + tab_specs,
        out_specs=[pl.BlockSpec((tm, qw), lambda i: (i, 0)),
                   pl.BlockSpec((tm, qw), lambda i: (i, 0))],
        out_shape=[jax.ShapeDtypeStruct((n, qw), BF16),
                   jax.ShapeDtypeStruct((n, qw), BF16)],
        compiler_params=_cparams(1),
        name="mla_kv_up",
    )(proj_a, proj_a, kv_g, w_ukv, *tabs)
    return q, k, v


def _nt_dot(a, b):
    return lax.dot_general(a, b, (((1,), (1,)), ((), ())), preferred_element_type=F32)


def _mla_attn_kernel(q_ref, k_ref, v_ref, o_ref, *, tq):
    seq = q_ref.shape[0]
    row = lax.broadcasted_iota(jnp.int32, (tq, tq), 0)
    col = lax.broadcasted_iota(jnp.int32, (tq, tq), 1)
    causal = col <= row
    def scores(hh, qi):
        cq = slice(hh * MLA_QK_PAD, (hh + 1) * MLA_QK_PAD)
        r0 = qi * tq
        q = q_ref[r0:r0 + tq, cq]
        s_d = jnp.where(causal, _nt_dot(q, k_ref[r0:r0 + tq, cq]), -jnp.inf)
        s_p = _nt_dot(q, k_ref[0:r0, cq]) if qi > 0 else None
        return s_d, s_p

    def finish(hh, qi, s_d, s_p):
        cq = slice(hh * MLA_QK_PAD, (hh + 1) * MLA_QK_PAD)
        r0 = qi * tq
        m = jnp.max(s_d, axis=-1, keepdims=True)
        if qi > 0:
            m = jnp.maximum(m, jnp.max(s_p, axis=-1, keepdims=True))
        acc = jnp.dot(jnp.exp2(s_d - m).astype(BF16), v_ref[r0:r0 + tq, cq], preferred_element_type=F32)
        if qi > 0:
            acc = acc + jnp.dot(jnp.exp2(s_p - m).astype(BF16), v_ref[0:r0, cq],
                                preferred_element_type=F32)
        o_ref[r0:r0 + tq, hh * MLA_V_DIM:(hh + 1) * MLA_V_DIM] = (
            acc[:, :MLA_V_DIM] / acc[:, MLA_V_DIM:MLA_V_DIM + 1]).astype(o_ref.dtype)

    items = [(hh, qi) for hh in range(q_ref.shape[1] // MLA_QK_PAD) for qi in range(seq // tq)]
    pending = [scores(*it) for it in items[:MLA_LOOKAHEAD]]
    for k, cur in enumerate(items):
        if k + MLA_LOOKAHEAD < len(items):
            pending.append(scores(*items[k + MLA_LOOKAHEAD]))
        finish(*cur, *pending.pop(0))


def _mla_attention(q, k, v, batch, seq):
    n = q.shape[0]
    return pl.pallas_call(
        functools.partial(_mla_attn_kernel, tq=min(256, seq)),
        grid=(batch, MLA_HEADS // MLA_HEADS_PER_STEP),
        in_specs=[pl.BlockSpec((seq, MLA_HEADS_PER_STEP * MLA_QK_PAD), lambda b, h: (b, h))] * 3,
        out_specs=pl.BlockSpec((seq, MLA_HEADS_PER_STEP * MLA_V_DIM), lambda b, h: (b, h)),
        out_shape=jax.ShapeDtypeStruct((n, MLA_HEADS * MLA_V_DIM), BF16),
        compiler_params=_cparams(2),
        name="mla_attention",
    )(q, k, v)


def _dil_kernel(qkv_ref, o_ref, lse_ref, *, dilation, slopes, scale):
    n = qkv_ref.shape[0]
    w = DIL_GROUP_WIDTH
    res = qkv_ref.shape[1] // (3 * w)
    blk = DIL_BLOCK
    ri = lax.broadcasted_iota(jnp.int32, (blk, 2 * blk), 0)
    cj = lax.broadcasted_iota(jnp.int32, (blk, 2 * blk), 1) - blk
    steps = ri - cj
    valid2 = (steps >= 0) & (steps <= blk)
    dist2 = (steps * dilation).astype(F32)
    valid1, dist1 = valid2[:, blk:], dist2[:, blk:]
    lane = lax.broadcasted_iota(jnp.int32, (blk, LANES), 1)

    def block(rr, q0, k0, klen, valid, dist):
        heads = range(DIL_HEADS_PER_GROUP)
        c0 = [rr * 3 * w + j * DIL_HEAD_DIM for j in heads]
        s = [_nt_dot(qkv_ref[pl.ds(q0, blk), c0[j]:c0[j] + DIL_HEAD_DIM],
                     qkv_ref[pl.ds(k0, klen), c0[j] + w:c0[j] + w + DIL_HEAD_DIM]) for j in heads]
        s = [jnp.where(valid, s[j] * scale - slopes[j] * dist, -jnp.inf) for j in heads]
        m = [jnp.max(s[j], axis=-1, keepdims=True) for j in heads]
        p = [jnp.exp(s[j] - m[j]) for j in heads]
        l = [jnp.sum(p[j], axis=-1, keepdims=True) for j in heads]
        lse_tile = jnp.zeros((blk, LANES), F32)
        for j in heads:
            vv = qkv_ref[pl.ds(k0, klen), c0[j] + 2 * w:c0[j] + 2 * w + DIL_HEAD_DIM]
            o = jnp.dot(p[j].astype(BF16), vv, preferred_element_type=F32) / l[j]
            o0 = rr * w + j * DIL_HEAD_DIM
            o_ref[pl.ds(q0, blk), o0:o0 + DIL_HEAD_DIM] = o
            lse_tile = jnp.where(lane == j, m[j] + jnp.log(l[j]), lse_tile)
        lse_ref[pl.ds(q0, blk), rr * LANES:(rr + 1) * LANES] = lse_tile

    for rr in range(res):
        block(rr, 0, 0, blk, valid1, dist1)

        def body(i, carry, rr=rr):
            q0 = pl.multiple_of(i * blk, blk)
            k0 = pl.multiple_of(i * blk - blk, blk)
            block(rr, q0, k0, 2 * blk, valid2, dist2)
            return carry

        lax.fori_loop(1, n // blk, body, 0, unroll=3)


def _dilated_group(qkv, g, batch, seq):
    window, dilation = DIL_GROUPS[g]
    assert window // dilation == DIL_BLOCK and seq % (dilation * DIL_BLOCK) == 0
    n = seq // dilation
    w = DIL_GROUP_WIDTH
    qkv_v = qkv.reshape(batch, n, dilation * 3 * w)
    slopes = np.float32(2.0) ** (np.float32(-8.0) * np.arange(1, DIL_HEADS + 1, dtype=np.float32)
                                 / np.float32(DIL_HEADS))
    slopes = tuple(float(s) for s in slopes[g * DIL_HEADS_PER_GROUP:(g + 1) * DIL_HEADS_PER_GROUP])

    res = min(dilation, DIL_RES_PER_STEP)
    o, lse = pl.pallas_call(
        functools.partial(_dil_kernel, dilation=dilation, slopes=slopes, scale=DIL_HEAD_DIM ** -0.5),
        grid=(batch, dilation // res),
        in_specs=[pl.BlockSpec((None, n, res * 3 * w), lambda b, r: (b, 0, r))],
        out_specs=[pl.BlockSpec((None, n, res * w), lambda b, r: (b, 0, r)),
                   pl.BlockSpec((None, n, res * LANES), lambda b, r: (b, 0, r))],
        out_shape=[jax.ShapeDtypeStruct((batch, n, dilation * w), F32),
                   jax.ShapeDtypeStruct((batch, n, dilation * LANES), F32)],
        compiler_params=_cparams(2),
        name=f"dilated_attn_g{g}",
    )(qkv_v)
    return o.reshape(batch * n, dilation * w), lse.reshape(batch * n, dilation * LANES)


def _merge_kernel(omla_ref, o0_ref, o1_ref, o2_ref, l0_ref, l1_ref, l2_ref, sgm_ref, sgd_ref,
                  wm_ref, wd_ref, out_ref, od_ref, otok_ref, ltok_ref):
    tm = omla_ref.shape[0]
    hpg = DIL_HEADS_PER_GROUP

    @pl.when(pl.program_id(1) == 0)
    def _():
        for g, (o_ref, l_ref) in enumerate(((o0_ref, l0_ref), (o1_ref, l1_ref), (o2_ref, l2_ref))):
            dil = DIL_GROUPS[g][1]
            for r in range(dil):
                rows = pl.ds(r, tm // dil, stride=dil)
                ltok_ref[g, rows, :] = l_ref[:, r * LANES:(r + 1) * LANES]
                for j in range(hpg):
                    c0 = (r * hpg + j) * DIL_HEAD_DIM
                    otok_ref[g * hpg + j, rows, :] = o_ref[:, c0:c0 + DIL_HEAD_DIM]
        lses = [ltok_ref[g] for g in range(len(DIL_GROUPS))]
        m = jnp.maximum(jnp.maximum(lses[0], lses[1]), lses[2])
        es = [jnp.exp(l - m) for l in lses]
        den = es[0] + es[1] + es[2]
        for g in range(len(DIL_GROUPS)):
            wg = es[g] / den
            for j in range(hpg):
                c0 = (g * hpg + j) * DIL_HEAD_DIM
                od_ref[:, c0:c0 + DIL_HEAD_DIM] = (otok_ref[g * hpg + j] * wg[:, j:j + 1]).astype(BF16)

    a = jnp.dot(omla_ref[...], wm_ref[...], preferred_element_type=F32)
    b = jnp.dot(od_ref[...], wd_ref[...], preferred_element_type=F32)
    out_ref[...] = (sgm_ref[...] * a + sgd_ref[...] * b).astype(out_ref.dtype)


def _merge(o_mla, o_dil, lse_dil, gates, w_o_mla, w_o_dil, d_model):
    n = o_mla.shape[0]
    tm = min(TM_MERGE, n)
    tn = min(MERGE_COLS, d_model)
    nj = d_model // tn
    row = lambda width: pl.BlockSpec((tm, width), lambda i, j: (i, 0))
    dils = [d for _, d in DIL_GROUPS]
    dil_row = lambda width: [pl.BlockSpec((tm // d, d * width), lambda i, j: (i, 0)) for d in dils]
    return pl.pallas_call(
        _merge_kernel,
        grid=(n // tm, nj),
        in_specs=[row(o_mla.shape[1])] + dil_row(DIL_GROUP_WIDTH) + dil_row(LANES) + [
            pl.BlockSpec((tm, tn), lambda i, j: (i, j)),
            pl.BlockSpec((tm, tn), lambda i, j: (i, j + nj)),
            pl.BlockSpec((w_o_mla.shape[0], tn), lambda i, j: (0, j)),
            pl.BlockSpec((w_o_dil.shape[0], tn), lambda i, j: (0, j))],
        out_specs=pl.BlockSpec((tm, tn), lambda i, j: (i, j)),
        out_shape=jax.ShapeDtypeStruct((n, d_model), BF16),
        scratch_shapes=[pltpu.VMEM((tm, DIL_WIDTH), BF16),
                        pltpu.VMEM((DIL_HEADS, tm, DIL_HEAD_DIM), F32),
                        pltpu.VMEM((len(DIL_GROUPS), tm, LANES), F32)],
        compiler_params=_cparams(2),
        name="branch_merge",
    )(o_mla, *o_dil, *lse_dil, gates, gates, w_o_mla, w_o_dil)


def _layer_norm(y, g, b):
    mu = jnp.mean(y, axis=-1, keepdims=True)
    yc = y - mu
    var = jnp.mean(yc * yc, axis=-1, keepdims=True)
    return yc * lax.rsqrt(var + LN_EPS) * g + b


def _pack_bf16_pairs(x):
    half = x.shape[1] // 2
    xb = x.astype(BF16).astype(F32)
    lo = lax.bitcast_convert_type(xb[:, :half], jnp.uint32) >> 16
    hi = lax.bitcast_convert_type(xb[:, half:], jnp.uint32) & jnp.uint32(0xFFFF0000)
    return lo | hi


def _unpack_bf16_pairs(w):
    lo = lax.bitcast_convert_type(w << 16, F32).astype(BF16)
    hi = lax.bitcast_convert_type(w & jnp.uint32(0xFFFF0000), F32).astype(BF16)
    return lo, hi


def _ln1_router_kernel(mg_ref, x_ref, wo_ref, g_ref, b_ref, wrc_ref, br_ref,
                       x1_ref, xpk_ref, idx_ref, gate_ref, rank_ref, cnt_ref, tri_ref, run_ref, *, alpha):
    sub = tri_ref.shape[0]

    @pl.when(pl.program_id(0) == 0)
    def _():
        r = lax.broadcasted_iota(jnp.int32, (sub, sub), 0)
        c = lax.broadcasted_iota(jnp.int32, (sub, sub), 1)
        tri_ref[...] = jnp.where(c < r, 1.0, 0.0).astype(BF16)
        run_ref[...] = jnp.zeros(run_ref.shape, F32)

    lane = lax.broadcasted_iota(jnp.int32, (sub, LANES), 1).astype(F32)
    blocks = [slice(s * sub, (s + 1) * sub) for s in range(x_ref.shape[0] // sub)]
    outs = [jnp.dot(mg_ref[rs, :], wo_ref[...], preferred_element_type=F32) for rs in blocks]
    for rs, out in zip(blocks, outs):
        x1 = _layer_norm(alpha * x_ref[rs, :] + out, g_ref[...], b_ref[...])
        x1_ref[rs, :] = x1
        xpk_ref[rs, :] = _pack_bf16_pairs(x1)
        xh = x1.astype(BF16)
        xl = (x1 - xh.astype(F32)).astype(BF16)
        hh_hl = jnp.dot(xh, wrc_ref[...], preferred_element_type=F32)
        logits = (hh_hl[:, :LANES] + hh_hl[:, LANES:]
                  + jnp.dot(xl, wrc_ref[:, :LANES], preferred_element_type=F32)) + br_ref[...]
        vals, idxs = [], []
        for _ in range(TOP_K):
            m = jnp.max(logits, axis=-1, keepdims=True)
            idx = jnp.min(jnp.where(logits == m, lane, float(LANES)), axis=-1, keepdims=True)
            vals.append(m)
            idxs.append(idx)
            logits = jnp.where(lane == idx, -jnp.inf, logits)
        es = [jnp.exp(v - vals[0]) for v in vals]
        den = es[0] + es[1] + es[2] + es[3]
        onehot = jnp.zeros(lane.shape, F32)
        for k in range(TOP_K):
            onehot = onehot + jnp.where(lane == idxs[k], 1.0, 0.0)
        before = run_ref[...] + jnp.dot(tri_ref[...], onehot.astype(BF16), preferred_element_type=F32)
        run_ref[...] = run_ref[...] + jnp.sum(onehot, axis=0, keepdims=True)
        idx_tile = jnp.zeros(lane.shape, F32)
        gate_tile = jnp.zeros(lane.shape, F32)
        rank_tile = jnp.zeros(lane.shape, F32)
        for k in range(TOP_K):
            rank = jnp.sum(jnp.where(lane == idxs[k], before, 0.0), axis=-1, keepdims=True)
            idx_tile = jnp.where(lane == float(k), idxs[k], idx_tile)
            gate_tile = jnp.where(lane == float(k), es[k] / den, gate_tile)
            rank_tile = jnp.where(lane == float(k), rank, rank_tile)
        idx_ref[rs, :] = idx_tile.astype(jnp.int32)
        gate_ref[rs, :] = gate_tile
        rank_ref[rs, :] = rank_tile.astype(jnp.int32)
    cnt_ref[...] = run_ref[...].astype(jnp.int32)


def _ln1_router(merged, x, w_out, ln_g, ln_b, w_rc, b_r, alpha):
    n, d = x.shape
    tm = min(TM_LN, n)
    full = lambda a: pl.BlockSpec(a.shape, lambda i: (0,) * a.ndim)
    row = lambda width: pl.BlockSpec((tm, width), lambda i: (i, 0))
    return pl.pallas_call(
        functools.partial(_ln1_router_kernel, alpha=alpha),
        grid=(n // tm,),
        in_specs=[row(d), row(d), full(w_out), full(ln_g), full(ln_b), full(w_rc), full(b_r)],
        out_specs=[row(d), row(d // 2), row(LANES), row(LANES), row(LANES),
                   pl.BlockSpec((1, LANES), lambda i: (0, 0))],
        out_shape=[jax.ShapeDtypeStruct((n, d), F32),
                   jax.ShapeDtypeStruct((n, d // 2), jnp.uint32),
                   jax.ShapeDtypeStruct((n, LANES), jnp.int32),
                   jax.ShapeDtypeStruct((n, LANES), F32),
                   jax.ShapeDtypeStruct((n, LANES), jnp.int32),
                   jax.ShapeDtypeStruct((1, LANES), jnp.int32)],
        scratch_shapes=[pltpu.VMEM((min(LN_SUB, tm),) * 2, BF16), pltpu.VMEM((1, LANES), F32)],
        compiler_params=_cparams(1),
        name="out_proj_ln1_router",
    )(merged, x, w_out, ln_g, ln_b, w_rc, b_r)


def _scatter_kernel(dest_ref, zrow_ref, xpk_ref, xs_hbm, zbuf, sem, zsem):
    t = pl.program_id(0)
    rows = TM_SCATTER

    @pl.when(t == 0)
    def _():
        zbuf[...] = jnp.zeros(zbuf.shape, zbuf.dtype)

        def zero_copy(i):
            row = pl.multiple_of(jnp.maximum(zrow_ref[i], 0), MOE_SUB)
            return pltpu.make_async_copy(zbuf, xs_hbm.at[pl.ds(row, MOE_SUB)], zsem)

        def zstart(i, carry):
            @pl.when(zrow_ref[i] >= 0)
            def _():
                zero_copy(i).start()
            return carry

        def zwait(i, carry):
            @pl.when(zrow_ref[i] >= 0)
            def _():
                zero_copy(i).wait()
            return carry

        lax.fori_loop(0, zrow_ref.shape[0], zstart, 0)
        lax.fori_loop(0, zrow_ref.shape[0], zwait, 0)

    def issue(r8, carry):
        for j in range(SUBLANES):
            for k in range(TOP_K):
                dst = dest_ref[(t * rows + r8 * SUBLANES + j) * TOP_K + k]
                pltpu.make_async_copy(xpk_ref.at[r8, pl.ds(j, 1)], xs_hbm.at[pl.ds(dst, 1)], sem).start()
        return carry

    groups = rows // SUBLANES
    lax.fori_loop(0, groups, issue, 0)
    pltpu.make_async_copy(xs_hbm.at[pl.ds(0, TOP_K * rows)], xs_hbm.at[pl.ds(0, TOP_K * rows)], sem).wait()


def _moe_scatter(xpk, dest, zrows, n_tiles):
    n, w = xpk.shape
    sub = SUBLANES
    xs = pl.pallas_call(
        _scatter_kernel,
        grid_spec=pltpu.PrefetchScalarGridSpec(
            num_scalar_prefetch=2,
            grid=(n // TM_SCATTER,),
            in_specs=[pl.BlockSpec((TM_SCATTER // sub, sub, w), lambda t, dest, zrows: (t, 0, 0))],
            out_specs=pl.BlockSpec(memory_space=pl.ANY),
            scratch_shapes=[pltpu.VMEM((MOE_SUB, w), xpk.dtype), pltpu.SemaphoreType.DMA,
                            pltpu.SemaphoreType.DMA]),
        out_shape=jax.ShapeDtypeStruct((n_tiles * TM_MOE, w), xpk.dtype),
        compiler_params=_cparams(1),
        name="moe_scatter",
    )(dest, zrows, xpk.reshape(n // sub, sub, w))
    return xs


def _new_expert(te_ref, t):
    return (t == 0) | (te_ref[t] != te_ref[jnp.maximum(t - 1, 0)])


def _for_sub_blocks(skip, n_rows, matmul, finish, fill):
    pair = 2 * MOE_SUB
    pairs = [slice(r0, r0 + pair) for r0 in range(0, n_rows, pair)]

    @pl.when(skip == 0)
    def _():
        results = [matmul(rs) for rs in pairs]
        for rs, res in zip(pairs, results):
            finish(rs, res)

    for both in pairs:
        r0 = both.start
        first, second = slice(r0, r0 + MOE_SUB), slice(r0 + MOE_SUB, r0 + pair)

        @pl.when((skip > 0) & (skip < r0 + MOE_SUB))
        def _(both=both):
            finish(both, matmul(both))

        @pl.when((skip >= r0 + MOE_SUB) & (skip < r0 + pair))
        def _(first=first, second=second):
            fill(first)
            finish(second, matmul(second))

        pl.when(skip >= r0 + pair)(functools.partial(fill, both))


def _ffn_a_kernel(te_ref, nt_ref, skip_ref, xs_ref, w_ref, b_ref, a_ref, wb_ref):
    t = pl.program_id(1)

    @pl.when(_new_expert(te_ref, t))
    def _():
        wb_ref[...] = w_ref[...].astype(BF16)

    def matmul(rs):
        x = jnp.concatenate(_unpack_bf16_pairs(xs_ref[rs, :]), axis=1)
        return jnp.dot(x, wb_ref[...], preferred_element_type=F32) + b_ref[...]

    def finish(rs, h):
        even = lax.broadcasted_iota(jnp.int32, (h.shape[0], LANES), 1) % 2 == 0
        for g in range(h.shape[1] // (2 * LANES)):
            h0 = h[:, 2 * g * LANES:(2 * g + 1) * LANES]
            h1 = h[:, (2 * g + 1) * LANES:(2 * g + 2) * LANES]
            glu = jnp.where(even, h0, pltpu.roll(h1, 1, 1))
            lin = jnp.where(even, pltpu.roll(h0, LANES - 1, 1), h1)
            x_glu = jnp.minimum(glu, SWIGLU_LIMIT)
            x_lin = jnp.clip(lin, -SWIGLU_LIMIT, SWIGLU_LIMIT)
            a_ref[rs, g * LANES:(g + 1) * LANES] = (
                x_glu * _sigmoid(SWIGLU_ALPHA * x_glu) * (x_lin + 1.0)).astype(a_ref.dtype)

    def fill(rs):
        a_ref[rs, :] = jnp.zeros((rs.stop - rs.start, a_ref.shape[1]), a_ref.dtype)

    _for_sub_blocks(skip_ref[t], a_ref.shape[0], matmul, finish, fill)


def _ffn_b_kernel(te_ref, nt_ref, skip_ref, a_ref, w_ref, b_ref, o_ref, perm_ref, wb_ref):
    t = pl.program_id(1)

    @pl.when(_new_expert(te_ref, t))
    def _():
        for c in range(w_ref.shape[1] // LANES):
            cs = slice(c * LANES, (c + 1) * LANES)
            for g in range(w_ref.shape[0] // LANES):
                for p in range(2):
                    src = g * LANES + p * PAIR
                    perm_ref[c, pl.ds(g * LANES + p, PAIR, stride=2), :] = w_ref[src:src + PAIR, cs]
            wb_ref[:, cs] = perm_ref[c].astype(BF16)

    def matmul(rs):
        return jnp.dot(a_ref[rs, :], wb_ref[...], preferred_element_type=F32) + b_ref[...]

    def finish(rs, res):
        o_ref[rs, :] = res

    def fill(rs):
        o_ref[rs, :] = jnp.zeros((rs.stop - rs.start, o_ref.shape[1]), o_ref.dtype)

    _for_sub_blocks(skip_ref[t], o_ref.shape[0], matmul, finish, fill)


def _moe_ffn(xs, tile_expert, n_valid, tile_skip, w1, b1, w2, b2):
    rows = xs.shape[0]
    n_tiles = rows // TM_MOE
    _, d, cols = w1.shape
    d_ff = cols // 2
    wc = min(W1_COLS, cols)
    tile = lambda t, nt: jnp.minimum(t, nt[0] - 1)
    a = pl.pallas_call(
        _ffn_a_kernel,
        grid_spec=pltpu.PrefetchScalarGridSpec(
            num_scalar_prefetch=3,
            grid=(cols // wc, n_tiles),
            in_specs=[pl.BlockSpec((TM_MOE, d // 2), lambda f, t, te, nt, tr: (tile(t, nt), 0)),
                      pl.BlockSpec((None, d, wc), lambda f, t, te, nt, tr: (te[t], 0, f)),
                      pl.BlockSpec((None, 1, wc), lambda f, t, te, nt, tr: (te[t], 0, f))],
            out_specs=pl.BlockSpec((TM_MOE, wc // 2), lambda f, t, te, nt, tr: (t, f)),
            scratch_shapes=[pltpu.VMEM((d, wc), BF16)]),
        out_shape=jax.ShapeDtypeStruct((rows, d_ff), BF16),
        compiler_params=_cparams(2),
        name="moe_ffn_up",
    )(tile_expert, n_valid, tile_skip, xs, w1, b1)
    nc = min(W2_COLS, d)
    return pl.pallas_call(
        _ffn_b_kernel,
        grid_spec=pltpu.PrefetchScalarGridSpec(
            num_scalar_prefetch=3,
            grid=(d // nc, n_tiles),
            in_specs=[pl.BlockSpec((TM_MOE, d_ff), lambda c, t, te, nt, tr: (tile(t, nt), 0)),
                      pl.BlockSpec((None, d_ff, nc), lambda c, t, te, nt, tr: (te[t], 0, c)),
                      pl.BlockSpec((None, 1, nc), lambda c, t, te, nt, tr: (te[t], 0, c))],
            out_specs=pl.BlockSpec((TM_MOE, nc), lambda c, t, te, nt, tr: (t, c)),
            scratch_shapes=[pltpu.VMEM((nc // LANES, d_ff, LANES), F32), pltpu.VMEM((d_ff, nc), BF16)]),
        out_shape=jax.ShapeDtypeStruct((rows, d), F32),
        compiler_params=_cparams(2),
        name="moe_ffn_down",
    )(tile_expert, n_valid, tile_skip, a, w2, b2)


def _combine_kernel(pos_ref, y_hbm, x_ref, gate_ref, g_ref, b_ref, o_ref, buf, sem, *, alpha):
    t = pl.program_id(0)
    rows = x_ref.shape[0]
    groups = rows // SUBLANES
    slot = t % 2

    def issue(step, dst_slot):
        def body(r8, carry):
            for j in range(SUBLANES):
                for k in range(TOP_K):
                    p = pos_ref[(step * rows + r8 * SUBLANES + j) * TOP_K + k]
                    pltpu.make_async_copy(y_hbm.at[pl.ds(p, 1)],
                                          buf.at[dst_slot, k * groups + r8, pl.ds(j, 1)],
                                          sem.at[dst_slot]).start()
            return carry

        lax.fori_loop(0, groups, body, 0)

    @pl.when(t == 0)
    def _():
        issue(0, 0)

    @pl.when(t + 1 < pl.num_programs(0))
    def _():
        issue(t + 1, 1 - slot)

    pltpu.make_async_copy(y_hbm.at[pl.ds(0, TOP_K * rows)], y_hbm.at[pl.ds(0, TOP_K * rows)],
                          sem.at[slot]).wait()
    gate = gate_ref[...]
    ffn = None
    for k in range(TOP_K):
        yk = buf[slot, k * groups:(k + 1) * groups].reshape(rows, buf.shape[-1]) * gate[:, k:k + 1]
        ffn = yk if ffn is None else ffn + yk
    o_ref[...] = _layer_norm(alpha * x_ref[...] + ffn, g_ref[...], b_ref[...])


def _moe_combine(y_sorted, pos, x1, gate, ln_g, ln_b, alpha):
    n, d = x1.shape
    tm = min(TM_COMBINE, n)
    return pl.pallas_call(
        functools.partial(_combine_kernel, alpha=alpha),
        grid_spec=pltpu.PrefetchScalarGridSpec(
            num_scalar_prefetch=1,
            grid=(n // tm,),
            in_specs=[pl.BlockSpec(memory_space=pl.ANY),
                      pl.BlockSpec((tm, d), lambda t, pos: (t, 0)),
                      pl.BlockSpec((tm, LANES), lambda t, pos: (t, 0)),
                      pl.BlockSpec((1, d), lambda t, pos: (0, 0)),
                      pl.BlockSpec((1, d), lambda t, pos: (0, 0))],
            out_specs=pl.BlockSpec((tm, d), lambda t, pos: (t, 0)),
            scratch_shapes=[pltpu.VMEM((2, TOP_K * tm // SUBLANES, SUBLANES, d), F32),
                            pltpu.SemaphoreType.DMA((2,))]),
        out_shape=jax.ShapeDtypeStruct((n, d), F32),
        compiler_params=_cparams(1),
        name="moe_combine_ln2",
    )(pos, y_sorted, x1, gate, ln_g, ln_b)


def _route(top_idx, rank, counts, n):
    nk = n * TOP_K
    n_tiles = -(-nk // TM_MOE) + N_EXPERTS
    padded = (counts + TM_MOE - 1) // TM_MOE * TM_MOE
    pad_end = jnp.cumsum(padded)
    pad_start = pad_end - padded
    lead = padded - counts
    onehot = top_idx[:, :, None] == jnp.arange(N_EXPERTS, dtype=jnp.int32)
    dest = (rank + jnp.sum(jnp.where(onehot, pad_start + lead, 0), axis=-1)).reshape(nk)
    n_valid = (pad_end[-1] // TM_MOE).astype(jnp.int32)
    tiles = jnp.arange(n_tiles, dtype=jnp.int32)
    tile_start = jnp.minimum(tiles, n_valid - 1) * TM_MOE
    tile_expert = jnp.minimum(jnp.sum(pad_end[None, :] <= tile_start[:, None], axis=1),
                              N_EXPERTS - 1).astype(jnp.int32)
    skip_in = lead[tile_expert] - (tiles * TM_MOE - pad_start[tile_expert])
    tile_skip = jnp.where(tiles < n_valid, jnp.clip(skip_in, 0, TM_MOE), TM_MOE).astype(jnp.int32)
    per_tile = TM_MOE // MOE_SUB
    subs = jnp.arange(n_tiles * per_tile, dtype=jnp.int32)
    covered = (subs % per_tile) * MOE_SUB >= tile_skip[subs // per_tile]
    zrows = jnp.where(covered, -1, subs * MOE_SUB).astype(jnp.int32)
    return dest.astype(jnp.int32), zrows, tile_expert, n_valid.reshape(1), tile_skip, n_tiles


def _rope_tables(seq):
    half = MLA_ROPE_DIM // 2
    inv = ROPE_THETA ** (-jnp.arange(half, dtype=F32) / half)
    ang = jnp.arange(seq).astype(F32)[:, None] * inv[None, :]
    cos, sin = jnp.cos(ang), jnp.sin(ang)
    z = jnp.zeros((seq, LANES - MLA_ROPE_DIM), F32)
    zh = jnp.zeros((seq, half), F32)
    return (jnp.concatenate([cos, cos, z], axis=1),
            jnp.concatenate([-sin, zh, z], axis=1),
            jnp.concatenate([zh, sin, z], axis=1))


def _pad_cols(w, width):
    return jnp.pad(w, ((0, 0), (0, width - w.shape[1])))


def kernel(x, w_in, q_norm_g, kv_norm_g, w_uq, w_ukv, w_o_mla, w_o_dil, w_out, ln1_g, ln1_b,
           w_router, b_router, w1, b1, w2, b2, ln2_g, ln2_b):
    batch, seq, d_model = x.shape
    depth = w_in.shape[0]
    n = batch * seq
    alpha = (2.0 * depth) ** 0.25
    tabs = _rope_tables(seq)
    o_q, o_kv, o_pe = 0, MLA_Q_RANK, MLA_Q_RANK + MLA_KV_RANK
    o_dil = o_pe + MLA_ROPE_DIM
    o_gate = o_dil + 3 * DIL_WIDTH
    xt = x.reshape(n, d_model)
    for l in range(depth):
        wi = w_in[l]
        w_a = jnp.concatenate([wi[:, o_kv:o_pe], _pad_cols(wi[:, o_pe:o_dil], 2 * LANES),
                               wi[:, o_q:o_kv]], axis=1).astype(BF16)
        w_b = jnp.concatenate(
            [wi[:, o_dil + which * DIL_WIDTH + g * DIL_GROUP_WIDTH:][:, :DIL_GROUP_WIDTH]
             for g in range(len(DIL_GROUPS)) for which in range(3)], axis=1).astype(BF16)
        w_c = wi[:, o_gate:].astype(BF16)
        wq = w_uq[l].reshape(MLA_Q_RANK, MLA_HEADS, MLA_QK_DIM)
        wq = jnp.pad(wq, ((0, 0), (0, 0), (0, MLA_QK_PAD - MLA_QK_DIM)))
        wq = wq.reshape(MLA_Q_RANK, MLA_HEADS * MLA_QK_PAD).astype(BF16)

        proj_a, xb = _proj_cast(xt, w_a, F32, "in_proj_latent")
        qkv_d = _proj_dilated(xb, w_b)
        gates = _proj(xb, w_c, F32, 1024, True, "in_proj_gates")

        q, k, v = _mla_up(proj_a, q_norm_g[l][None], kv_norm_g[l][None], wq,
                          w_ukv[l].astype(BF16), tabs, seq)
        o_mla = _mla_attention(q, k, v, batch, seq)
        dil = [_dilated_group(qkv_d[g], g, batch, seq) for g in range(len(DIL_GROUPS))]
        merged = _merge(o_mla, [o for o, _ in dil], [s for _, s in dil], gates,
                        w_o_mla[l].astype(BF16), w_o_dil[l].astype(BF16), d_model)

        w_r = _pad_cols(w_router[l], LANES)
        w_rh = w_r.astype(BF16)
        w_rl = (w_r - w_rh.astype(F32)).astype(BF16)
        w_rc = jnp.concatenate([w_rh, w_rl], axis=1)
        b_r = jnp.concatenate([b_router[l], jnp.full((LANES - N_EXPERTS,), NEG_BIG, F32)])[None]
        x1, xpk, top_idx, top_gate, rank, counts = _ln1_router(
            merged, xt, w_out[l].astype(BF16), ln1_g[l][None], ln1_b[l][None], w_rc, b_r, alpha)

        dest, zrows, tile_expert, n_valid, tile_skip, n_tiles = _route(
            top_idx[:, :TOP_K], rank[:, :TOP_K], counts[0, :N_EXPERTS], n)
        xs = _moe_scatter(xpk, dest, zrows, n_tiles)
        y_sorted = _moe_ffn(xs, tile_expert, n_valid, tile_skip, w1[l], b1[l][:, None, :], w2[l],
                            b2[l][:, None, :])
        xt = _moe_combine(y_sorted, dest, x1, top_gate, ln2_g[l][None], ln2_b[l][None], alpha)
    return xt.reshape(batch, seq, d_model)
```

```python
import functools

import numpy as np
import jax
import jax.numpy as jnp
from jax import lax
from jax.experimental import pallas as pl
from jax.experimental.pallas import tpu as pltpu

F32 = jnp.float32
BF16 = jnp.bfloat16

MLA_HEADS = 16
MLA_Q_RANK = 768
MLA_KV_RANK = 512
MLA_NOPE_DIM = 128
MLA_ROPE_DIM = 64
MLA_V_DIM = 128
MLA_QK_DIM = MLA_NOPE_DIM + MLA_ROPE_DIM
MLA_QK_PAD = 256
MLA_HEADS_PER_STEP = 4
MLA_LOOKAHEAD = 1
MLA_Q_SCALE = MLA_QK_DIM ** -0.5 * 1.4426950408889634
ROPE_THETA = 10000.0
DIL_GROUPS = ((128, 1), (512, 4), (2048, 16))
DIL_HEADS_PER_GROUP = 4
DIL_HEADS = DIL_HEADS_PER_GROUP * len(DIL_GROUPS)
DIL_HEAD_DIM = 128
DIL_GROUP_WIDTH = DIL_HEADS_PER_GROUP * DIL_HEAD_DIM
DIL_WIDTH = DIL_HEADS * DIL_HEAD_DIM
DIL_BLOCK = 128
DIL_RES_PER_STEP = 8
N_EXPERTS = 32
TOP_K = 4
SWIGLU_LIMIT = 7.0
SWIGLU_ALPHA = 1.702
LN_EPS = 1e-5
RMS_EPS = 1e-6
LANES = 128
SUBLANES = 8
NEG_BIG = -1e30

VMEM_LIMIT = 56 * 1024 * 1024
TM_PROJ = 1024
TM_PROJ_DIL = 1024
TM_UP = 512
TM_MERGE = 256
MERGE_COLS = 2048
TM_LN = 512
LN_SUB = 512
TM_MOE = 1024
MOE_SUB = 256
TM_SCATTER = 1024
TM_COMBINE = 256
W1_COLS = 1024
W2_COLS = 1024
PAIR = LANES // 2


def _cparams(n_axes):
    return pltpu.CompilerParams(dimension_semantics=("arbitrary",) * n_axes,
                                vmem_limit_bytes=VMEM_LIMIT)


def _sigmoid(x):
    return 1.0 / (1.0 + jnp.exp(-x))


def _proj_cast_kernel(x_ref, w_ref, o_ref, xb_ref):
    xb = x_ref[...].astype(BF16)
    xb_ref[...] = xb
    o_ref[...] = jnp.dot(xb, w_ref[...], preferred_element_type=F32).astype(o_ref.dtype)


def _proj_kernel(x_ref, w_ref, o_ref, *, act):
    acc = jnp.dot(x_ref[...], w_ref[...], preferred_element_type=F32)
    if act:
        acc = _sigmoid(acc)
    o_ref[...] = acc.astype(o_ref.dtype)


def _proj_cast(x, w, out_dtype, name):
    m, k = x.shape
    nc = w.shape[1]
    tm = min(TM_PROJ, m)
    return pl.pallas_call(
        _proj_cast_kernel,
        grid=(m // tm,),
        in_specs=[pl.BlockSpec((tm, k), lambda i: (i, 0)),
                  pl.BlockSpec((k, nc), lambda i: (0, 0))],
        out_specs=[pl.BlockSpec((tm, nc), lambda i: (i, 0)),
                   pl.BlockSpec((tm, k), lambda i: (i, 0))],
        out_shape=[jax.ShapeDtypeStruct((m, nc), out_dtype),
                   jax.ShapeDtypeStruct((m, k), BF16)],
        compiler_params=_cparams(1),
        name=name,
    )(x, w)


def _proj(xb, w, out_dtype, tn, act, name):
    m, k = xb.shape
    nc = w.shape[1]
    tm = min(TM_PROJ, m)
    return pl.pallas_call(
        functools.partial(_proj_kernel, act=act),
        grid=(m // tm, nc // tn),
        in_specs=[pl.BlockSpec((tm, k), lambda i, j: (i, 0)),
                  pl.BlockSpec((k, tn), lambda i, j: (0, j))],
        out_specs=pl.BlockSpec((tm, tn), lambda i, j: (i, j)),
        out_shape=jax.ShapeDtypeStruct((m, nc), out_dtype),
        compiler_params=_cparams(2),
        name=name,
    )(xb, w)


def _proj_dil_kernel(x_ref, w_ref, o0_ref, o1_ref, o2_ref, acc_ref):
    j = pl.program_id(1)
    tm = x_ref.shape[0]
    width = w_ref.shape[1]

    acc = jnp.dot(x_ref[...], w_ref[...], preferred_element_type=F32)
    n_cb = width // LANES
    for c in range(n_cb):
        acc_ref[c] = acc[:, c * LANES:(c + 1) * LANES]
    for g, o_ref in enumerate((o0_ref, o1_ref, o2_ref)):
        dil = DIL_GROUPS[g][1]

        @pl.when(j == g)
        def _(o_ref=o_ref, dil=dil):
            for r in range(dil):
                for c in range(n_cb):
                    o_ref[:, r * width + c * LANES:r * width + (c + 1) * LANES] = (
                        acc_ref[c, pl.ds(r, tm // dil, stride=dil), :].astype(BF16))


def _proj_dilated(x, w):
    m, k = x.shape
    tm = min(TM_PROJ_DIL, m)
    width = 3 * DIL_GROUP_WIDTH
    n_groups = len(DIL_GROUPS)
    dils = [d for _, d in DIL_GROUPS]
    return pl.pallas_call(
        _proj_dil_kernel,
        grid=(m // tm, n_groups),
        in_specs=[pl.BlockSpec((tm, k), lambda i, j: (i, 0)),
                  pl.BlockSpec((k, width), lambda i, j: (0, j))],
        out_specs=[pl.BlockSpec((tm // d, d * width), lambda i, j: (i, 0)) for d in dils],
        out_shape=[jax.ShapeDtypeStruct((m // d, d * width), BF16) for d in dils],
        scratch_shapes=[pltpu.VMEM((width // LANES, tm, LANES), F32)],
        compiler_params=_cparams(2),
        name="in_proj_dilated",
    )(x, w)


def _rope_tile(r, cos, sin_lo, sin_hi):
    half = MLA_ROPE_DIM // 2
    return (r * cos + pltpu.roll(r, LANES - half, 1) * sin_lo + pltpu.roll(r, half, 1) * sin_hi)


def _rms(x, g):
    ms = jnp.mean(x * x, axis=-1, keepdims=True)
    return x * lax.rsqrt(ms + RMS_EPS) * g


def _q_up_kernel(cq_ref, g_ref, w_ref, cos_ref, slo_ref, shi_ref, o_ref):
    cqn = _rms(cq_ref[...], g_ref[...]).astype(BF16)
    cos, slo, shi = cos_ref[...], slo_ref[...], shi_ref[...]
    for h in range(MLA_HEADS):
        c0 = h * MLA_QK_PAD
        qh = jnp.dot(cqn, w_ref[:, c0:c0 + MLA_QK_PAD], preferred_element_type=F32) * MLA_Q_SCALE
        o_ref[:, c0:c0 + LANES] = qh[:, :LANES].astype(BF16)
        o_ref[:, c0 + LANES:c0 + MLA_QK_PAD] = _rope_tile(qh[:, LANES:], cos, slo, shi).astype(BF16)


def _kv_up_kernel(ckv_ref, kpe_ref, g_ref, w_ref, cos_ref, slo_ref, shi_ref, k_ref, v_ref):
    ckvn = _rms(ckv_ref[...], g_ref[...]).astype(BF16)
    k_rot = _rope_tile(kpe_ref[...], cos_ref[...], slo_ref[...], shi_ref[...]).astype(BF16)
    lane = lax.broadcasted_iota(jnp.int32, kpe_ref.shape, 1)
    ones_col = jnp.where(lane == 0, 1.0, 0.0).astype(BF16)
    for h in range(MLA_HEADS):
        c0 = h * (MLA_NOPE_DIM + MLA_V_DIM)
        kv = jnp.dot(ckvn, w_ref[:, c0:c0 + MLA_NOPE_DIM + MLA_V_DIM], preferred_element_type=F32)
        k0 = h * MLA_QK_PAD
        k_ref[:, k0:k0 + LANES] = kv[:, :MLA_NOPE_DIM].astype(BF16)
        k_ref[:, k0 + LANES:k0 + MLA_QK_PAD] = k_rot
        v_ref[:, k0:k0 + MLA_V_DIM] = kv[:, MLA_NOPE_DIM:].astype(BF16)
        v_ref[:, k0 + MLA_V_DIM:k0 + MLA_QK_PAD] = ones_col


def _mla_up(proj_a, q_g, kv_g, w_uq, w_ukv, tabs, seq):
    n = proj_a.shape[0]
    tm = min(TM_UP, seq)
    sb = seq // tm
    tab_specs = [pl.BlockSpec((tm, LANES), lambda i: (i % sb, 0))] * 3
    qw = MLA_HEADS * MLA_QK_PAD
    q = pl.pallas_call(
        _q_up_kernel,
        grid=(n // tm,),
        in_specs=[pl.BlockSpec((tm, MLA_Q_RANK), lambda i: (i, 1)),
                  pl.BlockSpec((1, MLA_Q_RANK), lambda i: (0, 0)),
                  pl.BlockSpec((MLA_Q_RANK, qw), lambda i: (0, 0))] + tab_specs,
        out_specs=pl.BlockSpec((tm, qw), lambda i: (i, 0)),
        out_shape=jax.ShapeDtypeStruct((n, qw), BF16),
        compiler_params=_cparams(1),
        name="mla_q_up",
    )(proj_a, q_g, w_uq, *tabs)
    kvw = MLA_HEADS * (MLA_NOPE_DIM + MLA_V_DIM)
    k, v = pl.pallas_call(
        _kv_up_kernel,
        grid=(n // tm,),
        in_specs=[pl.BlockSpec((tm, MLA_KV_RANK), lambda i: (i, 0)),
                  pl.BlockSpec((tm, LANES), lambda i: (i, MLA_KV_RANK // LANES)),
                  pl.BlockSpec((1, MLA_KV_RANK), lambda i: (0, 0)),
                  pl.BlockSpec((MLA_KV_RANK, kvw), lambda i: (0, 0))] + tab_specs,
        out_specs=[pl.BlockSpec((tm, qw), lambda i: (i, 0)),
                   pl.BlockSpec((tm, qw), lambda i: (i, 0))],
        out_shape=[jax.ShapeDtypeStruct((n, qw), BF16),
                   jax.ShapeDtypeStruct((n, qw), BF16)],
        compiler_params=_cparams(1),
        name="mla_kv_up",
    )(proj_a, proj_a, kv_g, w_ukv, *tabs)
    return q, k, v


def _nt_dot(a, b):
    return lax.dot_general(a, b, (((1,), (1,)), ((), ())), preferred_element_type=F32)


def _mla_attn_kernel(q_ref, k_ref, v_ref, o_ref, *, tq):
    seq = q_ref.shape[0]
    row = lax.broadcasted_iota(jnp.int32, (tq, tq), 0)
    col = lax.broadcasted_iota(jnp.int32, (tq, tq), 1)
    causal = col <= row
    def scores(hh, qi):
        cq = slice(hh * MLA_QK_PAD, (hh + 1) * MLA_QK_PAD)
        r0 = qi * tq
        q = q_ref[r0:r0 + tq, cq]
        s_d = jnp.where(causal, _nt_dot(q, k_ref[r0:r0 + tq, cq]), -jnp.inf)
        s_p = _nt_dot(q, k_ref[0:r0, cq]) if qi > 0 else None
        return s_d, s_p

    def finish(hh, qi, s_d, s_p):
        cq = slice(hh * MLA_QK_PAD, (hh + 1) * MLA_QK_PAD)
        r0 = qi * tq
        m = jnp.max(s_d, axis=-1, keepdims=True)
        if qi > 0:
            m = jnp.maximum(m, jnp.max(s_p, axis=-1, keepdims=True))
        acc = jnp.dot(jnp.exp2(s_d - m).astype(BF16), v_ref[r0:r0 + tq, cq], preferred_element_type=F32)
        if qi > 0:
            acc = acc + jnp.dot(jnp.exp2(s_p - m).astype(BF16), v_ref[0:r0, cq],
                                preferred_element_type=F32)
        o_ref[r0:r0 + tq, hh * MLA_V_DIM:(hh + 1) * MLA_V_DIM] = (
            acc[:, :MLA_V_DIM] / acc[:, MLA_V_DIM:MLA_V_DIM + 1]).astype(o_ref.dtype)

    items = [(hh, qi) for hh in range(q_ref.shape[1] // MLA_QK_PAD) for qi in range(seq // tq)]
    pending = [scores(*it) for it in items[:MLA_LOOKAHEAD]]
    for k, cur in enumerate(items):
        if k + MLA_LOOKAHEAD < len(items):
            pending.append(scores(*items[k + MLA_LOOKAHEAD]))
        finish(*cur, *pending.pop(0))


def _mla_attention(q, k, v, batch, seq):
    n = q.shape[0]
    return pl.pallas_call(
        functools.partial(_mla_attn_kernel, tq=min(256, seq)),
        grid=(batch, MLA_HEADS // MLA_HEADS_PER_STEP),
        in_specs=[pl.BlockSpec((seq, MLA_HEADS_PER_STEP * MLA_QK_PAD), lambda b, h: (b, h))] * 3,
        out_specs=pl.BlockSpec((seq, MLA_HEADS_PER_STEP * MLA_V_DIM), lambda b, h: (b, h)),
        out_shape=jax.ShapeDtypeStruct((n, MLA_HEADS * MLA_V_DIM), BF16),
        compiler_params=_cparams(2),
        name="mla_attention",
    )(q, k, v)


def _dil_kernel(qkv_ref, o_ref, lse_ref, *, dilation, slopes, scale):
    n = qkv_ref.shape[0]
    w = DIL_GROUP_WIDTH
    res = qkv_ref.shape[1] // (3 * w)
    blk = DIL_BLOCK
    ri = lax.broadcasted_iota(jnp.int32, (blk, 2 * blk), 0)
    cj = lax.broadcasted_iota(jnp.int32, (blk, 2 * blk), 1) - blk
    steps = ri - cj
    valid2 = (steps >= 0) & (steps <= blk)
    dist2 = (steps * dilation).astype(F32)
    valid1, dist1 = valid2[:, blk:], dist2[:, blk:]
    lane = lax.broadcasted_iota(jnp.int32, (blk, LANES), 1)

    def block(rr, q0, k0, klen, valid, dist):
        heads = range(DIL_HEADS_PER_GROUP)
        c0 = [rr * 3 * w + j * DIL_HEAD_DIM for j in heads]
        s = [_nt_dot(qkv_ref[pl.ds(q0, blk), c0[j]:c0[j] + DIL_HEAD_DIM],
                     qkv_ref[pl.ds(k0, klen), c0[j] + w:c0[j] + w + DIL_HEAD_DIM]) for j in heads]
        s = [jnp.where(valid, s[j] * scale - slopes[j] * dist, -jnp.inf) for j in heads]
        m = [jnp.max(s[j], axis=-1, keepdims=True) for j in heads]
        p = [jnp.exp(s[j] - m[j]) for j in heads]
        l = [jnp.sum(p[j], axis=-1, keepdims=True) for j in heads]
        lse_tile = jnp.zeros((blk, LANES), F32)
        for j in heads:
            vv = qkv_ref[pl.ds(k0, klen), c0[j] + 2 * w:c0[j] + 2 * w + DIL_HEAD_DIM]
            o = jnp.dot(p[j].astype(BF16), vv, preferred_element_type=F32) / l[j]
            o0 = rr * w + j * DIL_HEAD_DIM
            o_ref[pl.ds(q0, blk), o0:o0 + DIL_HEAD_DIM] = o
            lse_tile = jnp.where(lane == j, m[j] + jnp.log(l[j]), lse_tile)
        lse_ref[pl.ds(q0, blk), rr * LANES:(rr + 1) * LANES] = lse_tile

    for rr in range(res):
        block(rr, 0, 0, blk, valid1, dist1)

        def body(i, carry, rr=rr):
            q0 = pl.multiple_of(i * blk, blk)
            k0 = pl.multiple_of(i * blk - blk, blk)
            block(rr, q0, k0, 2 * blk, valid2, dist2)
            return carry

        lax.fori_loop(1, n // blk, body, 0, unroll=3)


def _dilated_group(qkv, g, batch, seq):
    window, dilation = DIL_GROUPS[g]
    assert window // dilation == DIL_BLOCK and seq % (dilation * DIL_BLOCK) == 0
    n = seq // dilation
    w = DIL_GROUP_WIDTH
    qkv_v = qkv.reshape(batch, n, dilation * 3 * w)
    slopes = np.float32(2.0) ** (np.float32(-8.0) * np.arange(1, DIL_HEADS + 1, dtype=np.float32)
                                 / np.float32(DIL_HEADS))
    slopes = tuple(float(s) for s in slopes[g * DIL_HEADS_PER_GROUP:(g + 1) * DIL_HEADS_PER_GROUP])

    res = min(dilation, DIL_RES_PER_STEP)
    o, lse = pl.pallas_call(
        functools.partial(_dil_kernel, dilation=dilation, slopes=slopes, scale=DIL_HEAD_DIM ** -0.5),
        grid=(batch, dilation // res),
        in_specs=[pl.BlockSpec((None, n, res * 3 * w), lambda b, r: (b, 0, r))],
        out_specs=[pl.BlockSpec((None, n, res * w), lambda b, r: (b, 0, r)),
                   pl.BlockSpec((None, n, res * LANES), lambda b, r: (b, 0, r))],
        out_shape=[jax.ShapeDtypeStruct((batch, n, dilation * w), F32),
                   jax.ShapeDtypeStruct((batch, n, dilation * LANES), F32)],
        compiler_params=_cparams(2),
        name=f"dilated_attn_g{g}",
    )(qkv_v)
    return o.reshape(batch * n, dilation * w), lse.reshape(batch * n, dilation * LANES)


def _merge_kernel(omla_ref, o0_ref, o1_ref, o2_ref, l0_ref, l1_ref, l2_ref, sgm_ref, sgd_ref,
                  wm_ref, wd_ref, out_ref, od_ref, otok_ref, ltok_ref):
    tm = omla_ref.shape[0]
    hpg = DIL_HEADS_PER_GROUP

    @pl.when(pl.program_id(1) == 0)
    def _():
        for g, (o_ref, l_ref) in enumerate(((o0_ref, l0_ref), (o1_ref, l1_ref), (o2_ref, l2_ref))):
            dil = DIL_GROUPS[g][1]
            for r in range(dil):
                rows = pl.ds(r, tm // dil, stride=dil)
                ltok_ref[g, rows, :] = l_ref[:, r * LANES:(r + 1) * LANES]
                for j in range(hpg):
                    c0 = (r * hpg + j) * DIL_HEAD_DIM
                    otok_ref[g * hpg + j, rows, :] = o_ref[:, c0:c0 + DIL_HEAD_DIM]
        lses = [ltok_ref[g] for g in range(len(DIL_GROUPS))]
        m = jnp.maximum(jnp.maximum(lses[0], lses[1]), lses[2])
        es = [jnp.exp(l - m) for l in lses]
        den = es[0] + es[1] + es[2]
        for g in range(len(DIL_GROUPS)):
            wg = es[g] / den
            for j in range(hpg):
                c0 = (g * hpg + j) * DIL_HEAD_DIM
                od_ref[:, c0:c0 + DIL_HEAD_DIM] = (otok_ref[g * hpg + j] * wg[:, j:j + 1]).astype(BF16)

    a = jnp.dot(omla_ref[...], wm_ref[...], preferred_element_type=F32)
    b = jnp.dot(od_ref[...], wd_ref[...], preferred_element_type=F32)
    out_ref[...] = (sgm_ref[...] * a + sgd_ref[...] * b).astype(out_ref.dtype)


def _merge(o_mla, o_dil, lse_dil, gates, w_o_mla, w_o_dil, d_model):
    n = o_mla.shape[0]
    tm = min(TM_MERGE, n)
    tn = min(MERGE_COLS, d_model)
    nj = d_model // tn
    row = lambda width: pl.BlockSpec((tm, width), lambda i, j: (i, 0))
    dils = [d for _, d in DIL_GROUPS]
    dil_row = lambda width: [pl.BlockSpec((tm // d, d * width), lambda i, j: (i, 0)) for d in dils]
    return pl.pallas_call(
        _merge_kernel,
        grid=(n // tm, nj),
        in_specs=[row(o_mla.shape[1])] + dil_row(DIL_GROUP_WIDTH) + dil_row(LANES) + [
            pl.BlockSpec((tm, tn), lambda i, j: (i, j)),
            pl.BlockSpec((tm, tn), lambda i, j: (i, j + nj)),
            pl.BlockSpec((w_o_mla.shape[0], tn), lambda i, j: (0, j)),
            pl.BlockSpec((w_o_dil.shape[0], tn), lambda i, j: (0, j))],
        out_specs=pl.BlockSpec((tm, tn), lambda i, j: (i, j)),
        out_shape=jax.ShapeDtypeStruct((n, d_model), BF16),
        scratch_shapes=[pltpu.VMEM((tm, DIL_WIDTH), BF16),
                        pltpu.VMEM((DIL_HEADS, tm, DIL_HEAD_DIM), F32),
                        pltpu.VMEM((len(DIL_GROUPS), tm, LANES), F32)],
        compiler_params=_cparams(2),
        name="branch_merge",
    )(o_mla, *o_dil, *lse_dil, gates, gates, w_o_mla, w_o_dil)


def _layer_norm(y, g, b):
    mu = jnp.mean(y, axis=-1, keepdims=True)
    yc = y - mu
    var = jnp.mean(yc * yc, axis=-1, keepdims=True)
    return yc * lax.rsqrt(var + LN_EPS) * g + b


def _pack_bf16_pairs(x):
    half = x.shape[1] // 2
    xb = x.astype(BF16).astype(F32)
    lo = lax.bitcast_convert_type(xb[:, :half], jnp.uint32) >> 16
    hi = lax.bitcast_convert_type(xb[:, half:], jnp.uint32) & jnp.uint32(0xFFFF0000)
    return lo | hi


def _unpack_bf16_pairs(w):
    lo = lax.bitcast_convert_type(w << 16, F32).astype(BF16)
    hi = lax.bitcast_convert_type(w & jnp.uint32(0xFFFF0000), F32).astype(BF16)
    return lo, hi


def _ln1_router_kernel(mg_ref, x_ref, wo_ref, g_ref, b_ref, wrc_ref, br_ref,
                       x1_ref, xpk_ref, idx_ref, gate_ref, rank_ref, cnt_ref, tri_ref, run_ref, *, alpha):
    sub = tri_ref.shape[0]

    @pl.when(pl.program_id(0) == 0)
    def _():
        r = lax.broadcasted_iota(jnp.int32, (sub, sub), 0)
        c = lax.broadcasted_iota(jnp.int32, (sub, sub), 1)
        tri_ref[...] = jnp.where(c < r, 1.0, 0.0).astype(BF16)
        run_ref[...] = jnp.zeros(run_ref.shape, F32)

    lane = lax.broadcasted_iota(jnp.int32, (sub, LANES), 1).astype(F32)
    blocks = [slice(s * sub, (s + 1) * sub) for s in range(x_ref.shape[0] // sub)]
    outs = [jnp.dot(mg_ref[rs, :], wo_ref[...], preferred_element_type=F32) for rs in blocks]
    for rs, out in zip(blocks, outs):
        x1 = _layer_norm(alpha * x_ref[rs, :] + out, g_ref[...], b_ref[...])
        x1_ref[rs, :] = x1
        xpk_ref[rs, :] = _pack_bf16_pairs(x1)
        xh = x1.astype(BF16)
        xl = (x1 - xh.astype(F32)).astype(BF16)
        hh_hl = jnp.dot(xh, wrc_ref[...], preferred_element_type=F32)
        logits = (hh_hl[:, :LANES] + hh_hl[:, LANES:]
                  + jnp.dot(xl, wrc_ref[:, :LANES], preferred_element_type=F32)) + br_ref[...]
        vals, idxs = [], []
        for _ in range(TOP_K):
            m = jnp.max(logits, axis=-1, keepdims=True)
            idx = jnp.min(jnp.where(logits == m, lane, float(LANES)), axis=-1, keepdims=True)
            vals.append(m)
            idxs.append(idx)
            logits = jnp.where(lane == idx, -jnp.inf, logits)
        es = [jnp.exp(v - vals[0]) for v in vals]
        den = es[0] + es[1] + es[2] + es[3]
        onehot = jnp.zeros(lane.shape, F32)
        for k in range(TOP_K):
            onehot = onehot + jnp.where(lane == idxs[k], 1.0, 0.0)
        before = run_ref[...] + jnp.dot(tri_ref[...], onehot.astype(BF16), preferred_element_type=F32)
        run_ref[...] = run_ref[...] + jnp.sum(onehot, axis=0, keepdims=True)
        idx_tile = jnp.zeros(lane.shape, F32)
        gate_tile = jnp.zeros(lane.shape, F32)
        rank_tile = jnp.zeros(lane.shape, F32)
        for k in range(TOP_K):
            rank = jnp.sum(jnp.where(lane == idxs[k], before, 0.0), axis=-1, keepdims=True)
            idx_tile = jnp.where(lane == float(k), idxs[k], idx_tile)
            gate_tile = jnp.where(lane == float(k), es[k] / den, gate_tile)
            rank_tile = jnp.where(lane == float(k), rank, rank_tile)
        idx_ref[rs, :] = idx_tile.astype(jnp.int32)
        gate_ref[rs, :] = gate_tile
        rank_ref[rs, :] = rank_tile.astype(jnp.int32)
    cnt_ref[...] = run_ref[...].astype(jnp.int32)


def _ln1_router(merged, x, w_out, ln_g, ln_b, w_rc, b_r, alpha):
    n, d = x.shape
    tm = min(TM_LN, n)
    full = lambda a: pl.BlockSpec(a.shape, lambda i: (0,) * a.ndim)
    row = lambda width: pl.BlockSpec((tm, width), lambda i: (i, 0))
    return pl.pallas_call(
        functools.partial(_ln1_router_kernel, alpha=alpha),
        grid=(n // tm,),
        in_specs=[row(d), row(d), full(w_out), full(ln_g), full(ln_b), full(w_rc), full(b_r)],
        out_specs=[row(d), row(d // 2), row(LANES), row(LANES), row(LANES),
                   pl.BlockSpec((1, LANES), lambda i: (0, 0))],
        out_shape=[jax.ShapeDtypeStruct((n, d), F32),
                   jax.ShapeDtypeStruct((n, d // 2), jnp.uint32),
                   jax.ShapeDtypeStruct((n, LANES), jnp.int32),
                   jax.ShapeDtypeStruct((n, LANES), F32),
                   jax.ShapeDtypeStruct((n, LANES), jnp.int32),
                   jax.ShapeDtypeStruct((1, LANES), jnp.int32)],
        scratch_shapes=[pltpu.VMEM((min(LN_SUB, tm),) * 2, BF16), pltpu.VMEM((1, LANES), F32)],
        compiler_params=_cparams(1),
        name="out_proj_ln1_router",
    )(merged, x, w_out, ln_g, ln_b, w_rc, b_r)


def _scatter_kernel(dest_ref, zrow_ref, xpk_ref, xs_hbm, zbuf, sem, zsem):
    t = pl.program_id(0)
    rows = TM_SCATTER

    @pl.when(t == 0)
    def _():
        zbuf[...] = jnp.zeros(zbuf.shape, zbuf.dtype)

        def zero_copy(i):
            row = pl.multiple_of(jnp.maximum(zrow_ref[i], 0), MOE_SUB)
            return pltpu.make_async_copy(zbuf, xs_hbm.at[pl.ds(row, MOE_SUB)], zsem)

        def zstart(i, carry):
            @pl.when(zrow_ref[i] >= 0)
            def _():
                zero_copy(i).start()
            return carry

        def zwait(i, carry):
            @pl.when(zrow_ref[i] >= 0)
            def _():
                zero_copy(i).wait()
            return carry

        lax.fori_loop(0, zrow_ref.shape[0], zstart, 0)
        lax.fori_loop(0, zrow_ref.shape[0], zwait, 0)

    def issue(r8, carry):
        for j in range(SUBLANES):
            for k in range(TOP_K):
                dst = dest_ref[(t * rows + r8 * SUBLANES + j) * TOP_K + k]
                pltpu.make_async_copy(xpk_ref.at[r8, pl.ds(j, 1)], xs_hbm.at[pl.ds(dst, 1)], sem).start()
        return carry

    groups = rows // SUBLANES
    lax.fori_loop(0, groups, issue, 0)
    pltpu.make_async_copy(xs_hbm.at[pl.ds(0, TOP_K * rows)], xs_hbm.at[pl.ds(0, TOP_K * rows)], sem).wait()


def _moe_scatter(xpk, dest, zrows, n_tiles):
    n, w = xpk.shape
    sub = SUBLANES
    xs = pl.pallas_call(
        _scatter_kernel,
        grid_spec=pltpu.PrefetchScalarGridSpec(
            num_scalar_prefetch=2,
            grid=(n // TM_SCATTER,),
            in_specs=[pl.BlockSpec((TM_SCATTER // sub, sub, w), lambda t, dest, zrows: (t, 0, 0))],
            out_specs=pl.BlockSpec(memory_space=pl.ANY),
            scratch_shapes=[pltpu.VMEM((MOE_SUB, w), xpk.dtype), pltpu.SemaphoreType.DMA,
                            pltpu.SemaphoreType.DMA]),
        out_shape=jax.ShapeDtypeStruct((n_tiles * TM_MOE, w), xpk.dtype),
        compiler_params=_cparams(1),
        name="moe_scatter",
    )(dest, zrows, xpk.reshape(n // sub, sub, w))
    return xs


def _new_expert(te_ref, t):
    return (t == 0) | (te_ref[t] != te_ref[jnp.maximum(t - 1, 0)])


def _for_sub_blocks(skip, n_rows, matmul, finish, fill):
    pair = 2 * MOE_SUB
    pairs = [slice(r0, r0 + pair) for r0 in range(0, n_rows, pair)]

    @pl.when(skip == 0)
    def _():
        results = [matmul(rs) for rs in pairs]
        for rs, res in zip(pairs, results):
            finish(rs, res)

    for both in pairs:
        r0 = both.start
        first, second = slice(r0, r0 + MOE_SUB), slice(r0 + MOE_SUB, r0 + pair)

        @pl.when((skip > 0) & (skip < r0 + MOE_SUB))
        def _(both=both):
            finish(both, matmul(both))

        @pl.when((skip >= r0 + MOE_SUB) & (skip < r0 + pair))
        def _(first=first, second=second):
            fill(first)
            finish(second, matmul(second))

        pl.when(skip >= r0 + pair)(functools.partial(fill, both))


def _ffn_a_kernel(te_ref, nt_ref, skip_ref, xs_ref, w_ref, b_ref, a_ref, wb_ref):
    t = pl.program_id(1)

    @pl.when(_new_expert(te_ref, t))
    def _():
        wb_ref[...] = w_ref[...].astype(BF16)

    def matmul(rs):
        x = jnp.concatenate(_unpack_bf16_pairs(xs_ref[rs, :]), axis=1)
        return jnp.dot(x, wb_ref[...], preferred_element_type=F32) + b_ref[...]

    def finish(rs, h):
        even = lax.broadcasted_iota(jnp.int32, (h.shape[0], LANES), 1) % 2 == 0
        for g in range(h.shape[1] // (2 * LANES)):
            h0 = h[:, 2 * g * LANES:(2 * g + 1) * LANES]
            h1 = h[:, (2 * g + 1) * LANES:(2 * g + 2) * LANES]
            glu = jnp.where(even, h0, pltpu.roll(h1, 1, 1))
            lin = jnp.where(even, pltpu.roll(h0, LANES - 1, 1), h1)
            x_glu = jnp.minimum(glu, SWIGLU_LIMIT)
            x_lin = jnp.clip(lin, -SWIGLU_LIMIT, SWIGLU_LIMIT)
            a_ref[rs, g * LANES:(g + 1) * LANES] = (
                x_glu * _sigmoid(SWIGLU_ALPHA * x_glu) * (x_lin + 1.0)).astype(a_ref.dtype)

    def fill(rs):
        a_ref[rs, :] = jnp.zeros((rs.stop - rs.start, a_ref.shape[1]), a_ref.dtype)

    _for_sub_blocks(skip_ref[t], a_ref.shape[0], matmul, finish, fill)


def _ffn_b_kernel(te_ref, nt_ref, skip_ref, a_ref, w_ref, b_ref, o_ref, perm_ref, wb_ref):
    t = pl.program_id(1)

    @pl.when(_new_expert(te_ref, t))
    def _():
        for c in range(w_ref.shape[1] // LANES):
            cs = slice(c * LANES, (c + 1) * LANES)
            for g in range(w_ref.shape[0] // LANES):
                for p in range(2):
                    src = g * LANES + p * PAIR
                    perm_ref[c, pl.ds(g * LANES + p, PAIR, stride=2), :] = w_ref[src:src + PAIR, cs]
            wb_ref[:, cs] = perm_ref[c].astype(BF16)

    def matmul(rs):
        return jnp.dot(a_ref[rs, :], wb_ref[...], preferred_element_type=F32) + b_ref[...]

    def finish(rs, res):
        o_ref[rs, :] = res

    def fill(rs):
        o_ref[rs, :] = jnp.zeros((rs.stop - rs.start, o_ref.shape[1]), o_ref.dtype)

    _for_sub_blocks(skip_ref[t], o_ref.shape[0], matmul, finish, fill)


def _moe_ffn(xs, tile_expert, n_valid, tile_skip, w1, b1, w2, b2):
    rows = xs.shape[0]
    n_tiles = rows // TM_MOE
    _, d, cols = w1.shape
    d_ff = cols // 2
    wc = min(W1_COLS, cols)
    tile = lambda t, nt: jnp.minimum(t, nt[0] - 1)
    a = pl.pallas_call(
        _ffn_a_kernel,
        grid_spec=pltpu.PrefetchScalarGridSpec(
            num_scalar_prefetch=3,
            grid=(cols // wc, n_tiles),
            in_specs=[pl.BlockSpec((TM_MOE, d // 2), lambda f, t, te, nt, tr: (tile(t, nt), 0)),
                      pl.BlockSpec((None, d, wc), lambda f, t, te, nt, tr: (te[t], 0, f)),
                      pl.BlockSpec((None, 1, wc), lambda f, t, te, nt, tr: (te[t], 0, f))],
            out_specs=pl.BlockSpec((TM_MOE, wc // 2), lambda f, t, te, nt, tr: (t, f)),
            scratch_shapes=[pltpu.VMEM((d, wc), BF16)]),
        out_shape=jax.ShapeDtypeStruct((rows, d_ff), BF16),
        compiler_params=_cparams(2),
        name="moe_ffn_up",
    )(tile_expert, n_valid, tile_skip, xs, w1, b1)
    nc = min(W2_COLS, d)
    return pl.pallas_call(
        _ffn_b_kernel,
        grid_spec=pltpu.PrefetchScalarGridSpec(
            num_scalar_prefetch=3,
            grid=(d // nc, n_tiles),
            in_specs=[pl.BlockSpec((TM_MOE, d_ff), lambda c, t, te, nt, tr: (tile(t, nt), 0)),
                      pl.BlockSpec((None, d_ff, nc), lambda c, t, te, nt, tr: (te[t], 0, c)),
                      pl.BlockSpec((None, 1, nc), lambda c, t, te, nt, tr: (te[t], 0, c))],
            out_specs=pl.BlockSpec((TM_MOE, nc), lambda c, t, te, nt, tr: (t, c)),
            scratch_shapes=[pltpu.VMEM((nc // LANES, d_ff, LANES), F32), pltpu.VMEM((d_ff, nc), BF16)]),
        out_shape=jax.ShapeDtypeStruct((rows, d), F32),
        compiler_params=_cparams(2),
        name="moe_ffn_down",
    )(tile_expert, n_valid, tile_skip, a, w2, b2)


def _combine_kernel(pos_ref, y_hbm, x_ref, gate_ref, g_ref, b_ref, o_ref, buf, sem, *, alpha):
    t = pl.program_id(0)
    rows = x_ref.shape[0]
    groups = rows // SUBLANES
    slot = t % 2

    def issue(step, dst_slot):
        def body(r8, carry):
            for j in range(SUBLANES):
                for k in range(TOP_K):
                    p = pos_ref[(step * rows + r8 * SUBLANES + j) * TOP_K + k]
                    pltpu.make_async_copy(y_hbm.at[pl.ds(p, 1)],
                                          buf.at[dst_slot, k * groups + r8, pl.ds(j, 1)],
                                          sem.at[dst_slot]).start()
            return carry

        lax.fori_loop(0, groups, body, 0)

    @pl.when(t == 0)
    def _():
        issue(0, 0)

    @pl.when(t + 1 < pl.num_programs(0))
    def _():
        issue(t + 1, 1 - slot)

    pltpu.make_async_copy(y_hbm.at[pl.ds(0, TOP_K * rows)], y_hbm.at[pl.ds(0, TOP_K * rows)],
                          sem.at[slot]).wait()
    gate = gate_ref[...]
    ffn = None
    for k in range(TOP_K):
        yk = buf[slot, k * groups:(k + 1) * groups].reshape(rows, buf.shape[-1]) * gate[:, k:k + 1]
        ffn = yk if ffn is None else ffn + yk
    o_ref[...] = _layer_norm(alpha * x_ref[...] + ffn, g_ref[...], b_ref[...])


def _moe_combine(y_sorted, pos, x1, gate, ln_g, ln_b, alpha):
    n, d = x1.shape
    tm = min(TM_COMBINE, n)
    return pl.pallas_call(
        functools.partial(_combine_kernel, alpha=alpha),
        grid_spec=pltpu.PrefetchScalarGridSpec(
            num_scalar_prefetch=1,
            grid=(n // tm,),
            in_specs=[pl.BlockSpec(memory_space=pl.ANY),
                      pl.BlockSpec((tm, d), lambda t, pos: (t, 0)),
                      pl.BlockSpec((tm, LANES), lambda t, pos: (t, 0)),
                      pl.BlockSpec((1, d), lambda t, pos: (0, 0)),
                      pl.BlockSpec((1, d), lambda t, pos: (0, 0))],
            out_specs=pl.BlockSpec((tm, d), lambda t, pos: (t, 0)),
            scratch_shapes=[pltpu.VMEM((2, TOP_K * tm // SUBLANES, SUBLANES, d), F32),
                            pltpu.SemaphoreType.DMA((2,))]),
        out_shape=jax.ShapeDtypeStruct((n, d), F32),
        compiler_params=_cparams(1),
        name="moe_combine_ln2",
    )(pos, y_sorted, x1, gate, ln_g, ln_b)


def _route(top_idx, rank, counts, n):
    nk = n * TOP_K
    n_tiles = -(-nk // TM_MOE) + N_EXPERTS
    padded = (counts + TM_MOE - 1) // TM_MOE * TM_MOE
    pad_end = jnp.cumsum(padded)
    pad_start = pad_end - padded
    lead = padded - counts
    onehot = top_idx[:, :, None] == jnp.arange(N_EXPERTS, dtype=jnp.int32)
    dest = (rank + jnp.sum(jnp.where(onehot, pad_start + lead, 0), axis=-1)).reshape(nk)
    n_valid = (pad_end[-1] // TM_MOE).astype(jnp.int32)
    tiles = jnp.arange(n_tiles, dtype=jnp.int32)
    tile_start = jnp.minimum(tiles, n_valid - 1) * TM_MOE
    tile_expert = jnp.minimum(jnp.sum(pad_end[None, :] <= tile_start[:, None], axis=1),
                              N_EXPERTS - 1).astype(jnp.int32)
    skip_in = lead[tile_expert] - (tiles * TM_MOE - pad_start[tile_expert])
    tile_skip = jnp.where(tiles < n_valid, jnp.clip(skip_in, 0, TM_MOE), TM_MOE).astype(jnp.int32)
    per_tile = TM_MOE // MOE_SUB
    subs = jnp.arange(n_tiles * per_tile, dtype=jnp.int32)
    covered = (subs % per_tile) * MOE_SUB >= tile_skip[subs // per_tile]
    zrows = jnp.where(covered, -1, subs * MOE_SUB).astype(jnp.int32)
    return dest.astype(jnp.int32), zrows, tile_expert, n_valid.reshape(1), tile_skip, n_tiles


def _rope_tables(seq):
    half = MLA_ROPE_DIM // 2
    inv = ROPE_THETA ** (-jnp.arange(half, dtype=F32) / half)
    ang = jnp.arange(seq).astype(F32)[:, None] * inv[None, :]
    cos, sin = jnp.cos(ang), jnp.sin(ang)
    z = jnp.zeros((seq, LANES - MLA_ROPE_DIM), F32)
    zh = jnp.zeros((seq, half), F32)
    return (jnp.concatenate([cos, cos, z], axis=1),
            jnp.concatenate([-sin, zh, z], axis=1),
            jnp.concatenate([zh, sin, z], axis=1))


def _pad_cols(w, width):
    return jnp.pad(w, ((0, 0), (0, width - w.shape[1])))


def kernel(x, w_in, q_norm_g, kv_norm_g, w_uq, w_ukv, w_o_mla, w_o_dil, w_out, ln1_g, ln1_b,
           w_router, b_router, w1, b1, w2, b2, ln2_g, ln2_b):
    batch, seq, d_model = x.shape
    depth = w_in.shape[0]
    n = batch * seq
    alpha = (2.0 * depth) ** 0.25
    tabs = _rope_tables(seq)
    o_q, o_kv, o_pe = 0, MLA_Q_RANK, MLA_Q_RANK + MLA_KV_RANK
    o_dil = o_pe + MLA_ROPE_DIM
    o_gate = o_dil + 3 * DIL_WIDTH
    xt = x.reshape(n, d_model)
    for l in range(depth):
        wi = w_in[l]
        w_a = jnp.concatenate([wi[:, o_kv:o_pe], _pad_cols(wi[:, o_pe:o_dil], 2 * LANES),
                               wi[:, o_q:o_kv]], axis=1).astype(BF16)
        w_b = jnp.concatenate(
            [wi[:, o_dil + which * DIL_WIDTH + g * DIL_GROUP_WIDTH:][:, :DIL_GROUP_WIDTH]
             for g in range(len(DIL_GROUPS)) for which in range(3)], axis=1).astype(BF16)
        w_c = wi[:, o_gate:].astype(BF16)
        wq = w_uq[l].reshape(MLA_Q_RANK, MLA_HEADS, MLA_QK_DIM)
        wq = jnp.pad(wq, ((0, 0), (0, 0), (0, MLA_QK_PAD - MLA_QK_DIM)))
        wq = wq.reshape(MLA_Q_RANK, MLA_HEADS * MLA_QK_PAD).astype(BF16)

        proj_a, xb = _proj_cast(xt, w_a, F32, "in_proj_latent")
        qkv_d = _proj_dilated(xb, w_b)
        gates = _proj(xb, w_c, F32, 1024, True, "in_proj_gates")

        q, k, v = _mla_up(proj_a, q_norm_g[l][None], kv_norm_g[l][None], wq,
                          w_ukv[l].astype(BF16), tabs, seq)
        o_mla = _mla_attention(q, k, v, batch, seq)
        dil = [_dilated_group(qkv_d[g], g, batch, seq) for g in range(len(DIL_GROUPS))]
        merged = _merge(o_mla, [o for o, _ in dil], [s for _, s in dil], gates,
                        w_o_mla[l].astype(BF16), w_o_dil[l].astype(BF16), d_model)

        w_r = _pad_cols(w_router[l], LANES)
        w_rh = w_r.astype(BF16)
        w_rl = (w_r - w_rh.astype(F32)).astype(BF16)
        w_rc = jnp.concatenate([w_rh, w_rl], axis=1)
        b_r = jnp.concatenate([b_router[l], jnp.full((LANES - N_EXPERTS,), NEG_BIG, F32)])[None]
        x1, xpk, top_idx, top_gate, rank, counts = _ln1_router(
            merged, xt, w_out[l].astype(BF16), ln1_g[l][None], ln1_b[l][None], w_rc, b_r, alpha)

        dest, zrows, tile_expert, n_valid, tile_skip, n_tiles = _route(
            top_idx[:, :TOP_K], rank[:, :TOP_K], counts[0, :N_EXPERTS], n)
        xs = _moe_scatter(xpk, dest, zrows, n_tiles)
        y_sorted = _moe_ffn(xs, tile_expert, n_valid, tile_skip, w1[l], b1[l][:, None, :], w2[l],
                            b2[l][:, None, :])
        xt = _moe_combine(y_sorted, dest, x1, top_gate, ln2_g[l][None], ln2_b[l][None], alpha)
    return xt.reshape(batch, seq, d_model)
```
